```python
import math
import jax, jax.numpy as jnp
from jax import lax
import numpy as np

D_MODEL = 1024
BATCH = 8
SEQ = 2048
DEPTH = 4
DEC_BATCH = 128
DEC_SEQ = 1
PAST_LEN = 16384
PAGE_SIZE = 128

N_MIXERS = 2
N_RET_LAYERS = (DEPTH + N_MIXERS - 1) // N_MIXERS
N_GDN_LAYERS = DEPTH // N_MIXERS

RET_HEADS = 4
RET_DK = D_MODEL // RET_HEADS
RET_DV = 2 * RET_DK
RET_IN = 2 * RET_HEADS * RET_DK + 2 * RET_HEADS * RET_DV
RET_CHUNK = 64
ROPE_BASE = 10000.0

GDN_HEADS = 8
GDN_DK = 128
GDN_DV = 128
GDN_QKV = GDN_HEADS * (2 * GDN_DK + GDN_DV)
GDN_IN = GDN_QKV + GDN_HEADS * GDN_DV + 2 * GDN_HEADS
CONV_W = 4
GDN_CHUNK = 64

D_FF = 2816
EPS = 1e-6

kernel_name = 'retention_gdn_macaron_decode_step'


def _rmsnorm(x, g):
    xf = x.astype(jnp.float32)
    y = xf * lax.rsqrt(jnp.mean(xf * xf, axis=-1, keepdims=True) + EPS)
    return (y * g.astype(jnp.float32)).astype(x.dtype)


def _swiglu(x, w_gu, w_down):
    a, b = jnp.split(x @ w_gu, 2, axis=-1)
    return (jax.nn.silu(a) * b) @ w_down


def _l2norm(x):
    return x * lax.rsqrt(jnp.sum(x * x, axis=-1, keepdims=True) + EPS)


def _rotate(x, pos):
    theta = 1.0 / (ROPE_BASE ** jnp.linspace(0.0, 1.0, x.shape[-1] // 2, dtype=jnp.float32))
    ang = pos[:, None] * theta[None, :]
    cos = jnp.cos(ang)[None, :, None, :]
    sin = jnp.sin(ang)[None, :, None, :]
    x1 = x[..., 0::2]
    x2 = x[..., 1::2]
    return jnp.stack([x1 * cos - x2 * sin, x2 * cos + x1 * sin], axis=-1).reshape(x.shape)


def _to_chunks(x, c):
    B, T = x.shape[:2]
    n = -(-T // c)
    x = jnp.pad(x, [(0, 0), (0, n * c - T)] + [(0, 0)] * (x.ndim - 2))
    x = x.reshape((B, n, c) + x.shape[2:])
    return jnp.swapaxes(x, 2, 3)


def _from_chunks(o, T):
    B, n, H, c, e = o.shape
    return jnp.swapaxes(o, 2, 3).reshape(B, n * c, H, e)[:, :T]


def _retention(h, pos0, S0, w_in, w_out):
    f32 = jnp.float32
    B, T, _ = h.shape
    qd = RET_HEADS * RET_DK
    vd = RET_HEADS * RET_DV
    q, k, v, gate = jnp.split(h @ w_in, [qd, 2 * qd, 2 * qd + vd], axis=-1)
    pos = jnp.arange(T, dtype=f32) + pos0
    q = _rotate(q.reshape(B, T, RET_HEADS, RET_DK).astype(f32), pos)
    k = _rotate(k.reshape(B, T, RET_HEADS, RET_DK).astype(f32), pos) * (RET_DK ** -0.5)
    v = v.reshape(B, T, RET_HEADS, RET_DV).astype(f32)
    log_gamma = jnp.log(1.0 - 2.0 ** (-5.0 - jnp.arange(RET_HEADS, dtype=f32)))
    a = jnp.broadcast_to(log_gamma, (B, T, RET_HEADS))
    c = min(RET_CHUNK, T)
    qc, kc, vc, ac = _to_chunks(q, c), _to_chunks(k, c), _to_chunks(v, c), _to_chunks(a, c)
    b = jnp.cumsum(ac, axis=-1)
    causal = jnp.tril(jnp.ones((c, c), dtype=bool))
    decay = jnp.exp(jnp.where(causal, b[..., :, None] - b[..., None, :], -jnp.inf))
    P = jnp.einsum('bnhid,bnhjd->bnhij', qc, kc) * decay
    intra = jnp.einsum('bnhij,bnhje->bnhie', P, vc)
    b_last = b[..., -1:]
    q_dec = qc * jnp.exp(b)[..., None]
    k_dec = kc * jnp.exp(b_last - b)[..., None]
    c_dec = jnp.exp(b_last[..., 0])

    def step(S, xs):
        qd_, kd_, v_, cd_ = xs
        inter = jnp.einsum('bhcd,bhde->bhce', qd_, S)
        S = cd_[..., None, None] * S + jnp.einsum('bhcd,bhce->bhde', kd_, v_)
        return S, inter

    xs = (jnp.moveaxis(q_dec, 1, 0), jnp.moveaxis(k_dec, 1, 0), jnp.moveaxis(vc, 1, 0), jnp.moveaxis(c_dec, 1, 0))
    S, inter = lax.scan(step, S0.astype(f32), xs)
    o = _from_chunks(intra + jnp.moveaxis(inter, 0, 1), T)
    o = o * lax.rsqrt(jnp.mean(o * o, axis=-1, keepdims=True) + EPS)
    o = o.reshape(B, T, vd) * jax.nn.silu(gate.astype(f32))
    return o.astype(h.dtype) @ w_out, S


def _gated_delta(h, conv_buf, S0, w_in, conv_w, a_log, dt_bias, norm_w, w_out):
    f32 = jnp.float32
    B, T, _ = h.shape
    H, DK, DV = GDN_HEADS, GDN_DK, GDN_DV
    qkv, z, beta_raw, a_raw = jnp.split(h @ w_in, [GDN_QKV, GDN_QKV + H * DV, GDN_QKV + H * DV + H], axis=-1)
    xp = jnp.concatenate([conv_buf.astype(qkv.dtype), qkv], axis=1)
    new_buf = xp[:, T:]
    conv = xp[:, 0:T].astype(f32) * conv_w[0].astype(f32)
    for j in range(1, CONV_W):
        conv = conv + xp[:, j:j + T].astype(f32) * conv_w[j].astype(f32)
    qkv = jax.nn.silu(conv)
    q, k, v = jnp.split(qkv, [H * DK, 2 * H * DK], axis=-1)
    q = _l2norm(q.reshape(B, T, H, DK)) * (DK ** -0.5)
    k = _l2norm(k.reshape(B, T, H, DK))
    v = v.reshape(B, T, H, DV)
    beta = jax.nn.sigmoid(beta_raw.astype(f32))
    a = -jnp.exp(a_log.astype(f32)) * jax.nn.softplus(a_raw.astype(f32) + dt_bias.astype(f32))
    c = min(GDN_CHUNK, T)
    qc, kc, vc = _to_chunks(q, c), _to_chunks(k, c), _to_chunks(v, c)
    bc, ac = _to_chunks(beta, c), _to_chunks(a, c)
    b = jnp.cumsum(ac, axis=-1)
    diff = b[..., :, None] - b[..., None, :]
    strict = jnp.tril(jnp.ones((c, c), dtype=bool), -1)
    causal = jnp.tril(jnp.ones((c, c), dtype=bool))
    L = jnp.einsum('bnhid,bnhjd->bnhij', kc, kc) * jnp.exp(jnp.where(strict, diff, -jnp.inf)) * bc[..., :, None]
    rhs = jnp.concatenate([vc * bc[..., None], kc * (bc * jnp.exp(b))[..., None]], axis=-1)
    sol = lax.linalg.triangular_solve(L + jnp.eye(c, dtype=f32), rhs, left_side=True, lower=True, unit_diagonal=True)
    U, Wk = sol[..., :DV], sol[..., DV:]
    P = jnp.einsum('bnhid,bnhjd->bnhij', qc, kc) * jnp.exp(jnp.where(causal, diff, -jnp.inf))
    b_last = b[..., -1:]
    q_dec = qc * jnp.exp(b)[..., None]
    k_dec = kc * jnp.exp(b_last - b)[..., None]
    c_dec = jnp.exp(b_last[..., 0])

    def step(S, xs):
        qd_, kd_, u_, wk_, p_, cd_ = xs
        W = u_ - jnp.einsum('bhcd,bhde->bhce', wk_, S)
        o = jnp.einsum('bhcd,bhde->bhce', qd_, S) + jnp.einsum('bhij,bhje->bhie', p_, W)
        S = cd_[..., None, None] * S + jnp.einsum('bhcd,bhce->bhde', kd_, W)
        return S, o

    xs = tuple(jnp.moveaxis(t, 1, 0) for t in (q_dec, k_dec, U, Wk, P, c_dec))
    S, o = lax.scan(step, S0.astype(f32), xs)
    o = _from_chunks(jnp.moveaxis(o, 0, 1), T)
    o = o * lax.rsqrt(jnp.mean(o * o, axis=-1, keepdims=True) + EPS) * norm_w.astype(f32)
    o = o * jax.nn.silu(z.astype(f32).reshape(B, T, H, DV))
    return o.reshape(B, T, H * DV).astype(h.dtype) @ w_out, S, new_buf


def _trunk(x, pos0, s_ret, s_gdn, s_conv, norm_g, ffn_gu, ffn_down, ret_w_in, ret_w_out,
           gdn_w_in, gdn_conv_w, gdn_a_log, gdn_dt_bias, gdn_norm_w, gdn_w_out, final_g):
    new_ret, new_gdn, new_conv = [], [], []
    for i in range(DEPTH):
        x = x + 0.5 * _swiglu(_rmsnorm(x, norm_g[i, 0]), ffn_gu[i, 0], ffn_down[i, 0])
        hn = _rmsnorm(x, norm_g[i, 1])
        j = i // N_MIXERS
        if i % N_MIXERS == 0:
            m, s = _retention(hn, pos0, s_ret[j], ret_w_in[j], ret_w_out[j])
            new_ret.append(s.astype(s_ret.dtype))
        else:
            m, s, buf = _gated_delta(hn, s_conv[j], s_gdn[j], gdn_w_in[j], gdn_conv_w[j], gdn_a_log[j],
                                     gdn_dt_bias[j], gdn_norm_w[j], gdn_w_out[j])
            new_gdn.append(s.astype(s_gdn.dtype))
            new_conv.append(buf.astype(s_conv.dtype))
        x = x + m
        x = x + 0.5 * _swiglu(_rmsnorm(x, norm_g[i, 2]), ffn_gu[i, 1], ffn_down[i, 1])
    return _rmsnorm(x, final_g), jnp.stack(new_ret), jnp.stack(new_gdn), jnp.stack(new_conv)


def setup_inputs(seed: int = 0) -> dict:
    key = jax.random.key(seed)
    ks = jax.random.split(key, 17)
    f32 = jnp.float32

    def nrm(k, shape, scale):
        return jax.random.normal(k, shape, f32) * scale

    x_prompt = nrm(ks[0], (BATCH, SEQ, D_MODEL), 1.0)
    x_sample = nrm(ks[1], (DEC_BATCH, DEC_SEQ, D_MODEL), 1.0)
    state_ret = nrm(ks[2], (N_RET_LAYERS, DEC_BATCH, RET_HEADS, RET_DK, RET_DV), 1.0)
    state_gdn = nrm(ks[3], (N_GDN_LAYERS, DEC_BATCH, GDN_HEADS, GDN_DK, GDN_DV), 0.5)
    state_gdn_conv = nrm(ks[4], (N_GDN_LAYERS, DEC_BATCH, CONV_W - 1, GDN_QKV), 1.0)
    norm_g = 1.0 + nrm(ks[5], (DEPTH, 3, D_MODEL), 0.05)
    ffn_gu = nrm(ks[6], (DEPTH, 2, D_MODEL, 2 * D_FF), D_MODEL ** -0.5)
    ffn_down = nrm(ks[7], (DEPTH, 2, D_FF, D_MODEL), D_FF ** -0.5)
    ret_w_in = nrm(ks[8], (N_RET_LAYERS, D_MODEL, RET_IN), D_MODEL ** -0.5)
    ret_w_out = nrm(ks[9], (N_RET_LAYERS, RET_HEADS * RET_DV, D_MODEL), (RET_HEADS * RET_DV) ** -0.5)
    gdn_w_in = nrm(ks[10], (N_GDN_LAYERS, D_MODEL, GDN_IN), D_MODEL ** -0.5)
    gdn_conv_w = nrm(ks[11], (N_GDN_LAYERS, CONV_W, GDN_QKV), CONV_W ** -0.5)
    gdn_a_log = jnp.log(jax.random.uniform(ks[12], (N_GDN_LAYERS, GDN_HEADS), f32, 1.0, 16.0))
    dt = jnp.exp(jax.random.uniform(ks[13], (N_GDN_LAYERS, GDN_HEADS), f32, math.log(1e-3), math.log(1e-1)))
    gdn_dt_bias = dt + jnp.log(-jnp.expm1(-dt))
    gdn_norm_w = 1.0 + nrm(ks[14], (N_GDN_LAYERS, GDN_DV), 0.05)
    gdn_w_out = nrm(ks[15], (N_GDN_LAYERS, GDN_HEADS * GDN_DV, D_MODEL), (GDN_HEADS * GDN_DV) ** -0.5)
    final_g = 1.0 + nrm(ks[16], (D_MODEL,), 0.05)
    return {'x_prompt': x_prompt, 'x_sample': x_sample, 'state_ret': state_ret, 'state_gdn': state_gdn,
            'state_gdn_conv': state_gdn_conv, 'norm_g': norm_g, 'ffn_gu': ffn_gu, 'ffn_down': ffn_down,
            'ret_w_in': ret_w_in, 'ret_w_out': ret_w_out, 'gdn_w_in': gdn_w_in, 'gdn_conv_w': gdn_conv_w,
            'gdn_a_log': gdn_a_log, 'gdn_dt_bias': gdn_dt_bias, 'gdn_norm_w': gdn_norm_w,
            'gdn_w_out': gdn_w_out, 'final_g': final_g}


def reference(x_prompt, x_sample, state_ret, state_gdn, state_gdn_conv, norm_g, ffn_gu, ffn_down,
              ret_w_in, ret_w_out, gdn_w_in, gdn_conv_w, gdn_a_log, gdn_dt_bias, gdn_norm_w,
              gdn_w_out, final_g):
    bp = x_prompt.shape[0]
    zero_ret = jnp.zeros((N_RET_LAYERS, bp, RET_HEADS, RET_DK, RET_DV), state_ret.dtype)
    zero_gdn = jnp.zeros((N_GDN_LAYERS, bp, GDN_HEADS, GDN_DK, GDN_DV), state_gdn.dtype)
    zero_conv = jnp.zeros((N_GDN_LAYERS, bp, CONV_W - 1, GDN_QKV), state_gdn_conv.dtype)
    y_prompt, ret_p, gdn_p, conv_p = _trunk(
        x_prompt, 0.0, zero_ret, zero_gdn, zero_conv, norm_g, ffn_gu, ffn_down, ret_w_in, ret_w_out,
        gdn_w_in, gdn_conv_w, gdn_a_log, gdn_dt_bias, gdn_norm_w, gdn_w_out, final_g)
    y_sample, ret_s, gdn_s, conv_s = _trunk(
        x_sample, float(PAST_LEN), state_ret, state_gdn, state_gdn_conv, norm_g, ffn_gu, ffn_down,
        ret_w_in, ret_w_out, gdn_w_in, gdn_conv_w, gdn_a_log, gdn_dt_bias, gdn_norm_w, gdn_w_out, final_g)
    return (y_prompt, y_sample, ret_p, gdn_p, conv_p, ret_s, gdn_s, conv_s)
```

```python
import functools
import math

import jax
import jax.numpy as jnp
from jax import lax
from jax.experimental import pallas as pl
from jax.experimental.pallas import tpu as pltpu

F32 = jnp.float32
BF16 = jnp.bfloat16
EPS = 1e-6
ROPE_BASE = 10000.0
SAMPLE_PAST_LEN = 16384.0

LANES = 128
SUBLANES = 8
VMEM_LIMIT_BYTES = 56 * 1024 * 1024

FFN_ROWS = 512
RET_CHUNK = 256
GDN_CHUNK = 128
RET_STATE_BLOCK = 2
GDN_STATE_BLOCK = 8


def _dot(a, b):
    return jnp.dot(a, b, preferred_element_type=F32)


def _dot_nt(a, b):
    return lax.dot_general(a, b, (((1,), (1,)), ((), ())), preferred_element_type=F32)


def _rms(x, g):
    return x * lax.rsqrt(jnp.mean(x * x, axis=-1, keepdims=True) + EPS) * g


def _silu(x):
    return x * jax.nn.sigmoid(x)


def _softplus(x):
    return jnp.maximum(x, 0.0) + jnp.log1p(jnp.exp(-jnp.abs(x)))


def _resident(shape):
    nd = len(shape)
    return pl.BlockSpec(shape, lambda *_: (0,) * nd, pipeline_mode=pl.Buffered(1))


def _params(*sem):
    return pltpu.CompilerParams(dimension_semantics=sem, vmem_limit_bytes=VMEM_LIMIT_BYTES)


def _ffn_kernel(x_ref, g_ref, wg_ref, wu_ref, wd_ref, fg_ref, o_ref, *, final):
    x = x_ref[...]
    h = _rms(x, g_ref[...]).astype(BF16)
    a = _dot(h, wg_ref[...])
    b = _dot(h, wu_ref[...])
    act = (_silu(a) * b).astype(BF16)
    y = x + 0.5 * _dot(act, wd_ref[...])
    if final:
        y = _rms(y, fg_ref[...])
    o_ref[...] = y


def _ffn(x, g, w_gu, w_down, final_g, final):
    m, d = x.shape
    f = w_down.shape[0]
    tm = min(FFN_ROWS, m)
    assert m % tm == 0 and f % LANES == 0
    return pl.pallas_call(
        functools.partial(_ffn_kernel, final=final),
        name="ffn",
        grid=(m // tm,),
        in_specs=[
            pl.BlockSpec((tm, d), lambda i: (i, 0)),
            _resident((1, d)),
            pl.BlockSpec((d, f), lambda i: (0, 0), pipeline_mode=pl.Buffered(1)),
            pl.BlockSpec((d, f), lambda i: (0, 1), pipeline_mode=pl.Buffered(1)),
            _resident((f, d)),
            _resident((1, d)),
        ],
        out_specs=pl.BlockSpec((tm, d), lambda i: (i, 0)),
        out_shape=jax.ShapeDtypeStruct((m, d), F32),
        compiler_params=_params("arbitrary"),
    )(x, g, w_gu, w_gu, w_down, final_g)


def _ret_prompt_kernel(x_ref, g_ref, cos_ref, sin_ref, win_ref, wout_ref, y_ref, sout_ref, s_scr, og_scr,
                       *, heads, dk, dv, chunk):
    t = pl.program_id(1)
    half = dk // 2

    @pl.when(t == 0)
    def _():
        s_scr[...] = jnp.zeros_like(s_scr)

    x = x_ref[...]
    hn = _rms(x, g_ref[...]).astype(BF16)
    cos = cos_ref[...]
    sin = sin_ref[...]
    ri = lax.broadcasted_iota(jnp.int32, (chunk, chunk), 0)
    ci = lax.broadcasted_iota(jnp.int32, (chunk, chunk), 1)
    lag = (ri - ci).astype(F32)
    row = lax.broadcasted_iota(jnp.int32, (chunk, half), 0).astype(F32)
    k_off, v_off, g_off = heads * dk, 2 * heads * dk, 2 * heads * dk + heads * dv
    for h in range(heads):
        log_gamma = math.log(1.0 - 2.0 ** (-5.0 - h))
        q = _dot(hn, win_ref[:, h * dk:(h + 1) * dk])
        k = _dot(hn, win_ref[:, k_off + h * dk:k_off + (h + 1) * dk])
        v = _dot(hn, win_ref[:, v_off + h * dv:v_off + (h + 1) * dv]).astype(BF16)
        gate = _dot(hn, win_ref[:, g_off + h * dv:g_off + (h + 1) * dv])
        q1, q2 = q[:, :half], q[:, half:]
        k1, k2 = k[:, :half], k[:, half:]
        qr1, qr2 = q1 * cos - q2 * sin, q2 * cos + q1 * sin
        kr1, kr2 = (k1 * cos - k2 * sin) * dk ** -0.5, (k2 * cos + k1 * sin) * dk ** -0.5
        qb = jnp.concatenate([qr1, qr2], axis=1).astype(BF16)
        kb = jnp.concatenate([kr1, kr2], axis=1).astype(BF16)
        decay = jnp.where(ri >= ci, jnp.exp(lag * log_gamma), 0.0)
        p = (_dot_nt(qb, kb) * decay).astype(BF16)
        q_scale = jnp.exp((row + 1.0) * log_gamma)
        k_scale = jnp.exp((chunk - 1.0 - row) * log_gamma)
        qd = jnp.concatenate([qr1 * q_scale, qr2 * q_scale], axis=1).astype(BF16)
        kd = jnp.concatenate([kr1 * k_scale, kr2 * k_scale], axis=1)
        s = s_scr[h]
        o = _dot(p, v) + _dot(qd, s.astype(BF16))
        s_scr[h] = math.exp(chunk * log_gamma) * s + _dot(kd.T.astype(BF16), v)
        o = o * lax.rsqrt(jnp.mean(o * o, axis=-1, keepdims=True) + EPS) * _silu(gate)
        og_scr[:, h * dv:(h + 1) * dv] = o.astype(BF16)
    y_ref[...] = x + _dot(og_scr[...], wout_ref[...])

    @pl.when(t == pl.num_programs(1) - 1)
    def _():
        ro = lax.broadcasted_iota(jnp.int32, (dk, dk), 0)
        rin = lax.broadcasted_iota(jnp.int32, (dk, dk), 1)
        perm = (rin == (ro >> 1) + (ro & 1) * half).astype(BF16)
        for h in range(heads):
            sout_ref[0, h] = sum(_dot(perm, term) for term in _bf16_terms(s_scr[h]))


def _ret_prompt(x, g, cos, sin, w_in, w_out, batch, heads, dk, dv):
    m, d = x.shape
    seq = m // batch
    chunk = min(RET_CHUNK, seq)
    assert seq % chunk == 0
    nt = seq // chunk
    return pl.pallas_call(
        functools.partial(_ret_prompt_kernel, heads=heads, dk=dk, dv=dv, chunk=chunk),
        name="ret_prompt",
        grid=(batch, nt),
        in_specs=[
            pl.BlockSpec((chunk, d), lambda b, t: (b * nt + t, 0)),
            _resident((1, d)),
            pl.BlockSpec((chunk, dk // 2), lambda b, t: (t, 0)),
            pl.BlockSpec((chunk, dk // 2), lambda b, t: (t, 0)),
            _resident(w_in.shape),
            _resident(w_out.shape),
        ],
        out_specs=[
            pl.BlockSpec((chunk, d), lambda b, t: (b * nt + t, 0)),
            pl.BlockSpec((1, heads, dk, dv), lambda b, t: (b, 0, 0, 0)),
        ],
        out_shape=[
            jax.ShapeDtypeStruct((m, d), F32),
            jax.ShapeDtypeStruct((batch, heads, dk, dv), F32),
        ],
        scratch_shapes=[pltpu.VMEM((heads, dk, dv), F32), pltpu.VMEM((chunk, heads * dv), BF16)],
        compiler_params=_params("arbitrary", "arbitrary"),
    )(x, g, cos, sin, w_in, w_out)


def _ret_sample_proj_kernel(x_ref, g_ref, cos_ref, sin_ref, win_ref, wsw_ref, q_ref, kt_ref, v_ref, gate_ref,
                            *, heads, dk, dv):
    qk_w = 2 * heads * dk
    hn = _rms(x_ref[...], g_ref[...]).astype(BF16)
    qk = _dot(hn, win_ref[:, :qk_w]) * cos_ref[...] + _dot(hn, wsw_ref[...]) * sin_ref[...]
    q_ref[...] = qk[:, :heads * dk]
    kt_ref[...] = (qk[:, heads * dk:] * dk ** -0.5).T
    v_ref[...] = _dot(hn, win_ref[:, qk_w:qk_w + heads * dv])
    gate_ref[...] = _dot(hn, win_ref[:, qk_w + heads * dv:])


def _ret_sample_state_kernel(q_ref, kt_ref, v_ref, s0_ref, s_ref, o_ref, *, heads, dk, dv, block):
    n = v_ref.shape[0]
    lane = lax.broadcasted_iota(jnp.int32, (dk, n), 1)
    for i in range(block):
        b = pl.program_id(0) * block + i
        q_row = q_ref[pl.ds(b, 1), :]
        for h in range(heads):
            gamma = 1.0 - 2.0 ** (-5.0 - h)
            kt_b = jnp.where(lane == b, kt_ref[h * dk:(h + 1) * dk, :], 0.0).astype(BF16)
            v_h = v_ref[:, h * dv:(h + 1) * dv].astype(BF16)
            s_new = gamma * s0_ref[i, h] + _dot(kt_b, v_h)
            s_ref[i, h] = s_new
            q8 = jnp.broadcast_to(q_row[:, h * dk:(h + 1) * dk], (SUBLANES, dk)).astype(BF16)
            o_ref[0, i:i + 1, h * dv:(h + 1) * dv] = _dot(q8, s_new.astype(BF16))[0:1]


def _mixer_out_kernel(x_ref, o_ref, gate_ref, nw_ref, wout_ref, y_ref, *, heads, dv):
    parts = []
    for h in range(heads):
        o = o_ref[:, h * dv:(h + 1) * dv]
        o = o * lax.rsqrt(jnp.mean(o * o, axis=-1, keepdims=True) + EPS) * nw_ref[...]
        parts.append((o * _silu(gate_ref[:, h * dv:(h + 1) * dv])).astype(BF16))
    y_ref[...] = x_ref[...] + _dot(jnp.concatenate(parts, axis=1), wout_ref[...])


def _mixer_out(x, o, gate, norm_w, w_out, heads, dv):
    return pl.pallas_call(
        functools.partial(_mixer_out_kernel, heads=heads, dv=dv),
        name="mixer_out",
        out_shape=jax.ShapeDtypeStruct(x.shape, F32),
        compiler_params=_params(),
    )(x, o, gate, norm_w, w_out)


def _ret_sample(x, g, cos, sin, w_in, w_sw, w_out, s0, ones_dv):
    n, d = x.shape
    _, heads, dk, dv = s0.shape
    block = math.gcd(RET_STATE_BLOCK, n)
    q, kt, v, gate = pl.pallas_call(
        functools.partial(_ret_sample_proj_kernel, heads=heads, dk=dk, dv=dv),
        name="ret_sample_proj",
        out_shape=[
            jax.ShapeDtypeStruct((n, heads * dk), F32),
            jax.ShapeDtypeStruct((heads * dk, n), F32),
            jax.ShapeDtypeStruct((n, heads * dv), F32),
            jax.ShapeDtypeStruct((n, heads * dv), F32),
        ],
        compiler_params=_params(),
    )(x, g, cos, sin, w_in, w_sw)
    s_new, o = pl.pallas_call(
        functools.partial(_ret_sample_state_kernel, heads=heads, dk=dk, dv=dv, block=block),
        name="ret_sample_state",
        grid=(n // block,),
        in_specs=[
            _resident(q.shape),
            _resident(kt.shape),
            _resident(v.shape),
            pl.BlockSpec((block, heads, dk, dv), lambda i: (i, 0, 0, 0)),
        ],
        out_specs=[
            pl.BlockSpec((block, heads, dk, dv), lambda i: (i, 0, 0, 0)),
            pl.BlockSpec((1, block, heads * dv), lambda i: (i, 0, 0)),
        ],
        out_shape=[
            jax.ShapeDtypeStruct(s0.shape, F32),
            jax.ShapeDtypeStruct((n // block, block, heads * dv), F32),
        ],
        compiler_params=_params("arbitrary"),
    )(q, kt, v, s0)
    y = _mixer_out(x, o.reshape(n, heads * dv), gate, ones_dv, w_out, heads, dv)
    return y, s_new


def _bf16_terms(x):
    t0 = x.astype(BF16)
    r = x - t0.astype(F32)
    t1 = r.astype(BF16)
    t2 = (r - t1.astype(F32)).astype(BF16)
    return t0, t1, t2


def _unit_lower_inverse_minus_identity(lm, ri, ci, chunk):
    def level_mask(lev):
        return ((ri >> lev) == (ci >> lev)) & ((ri >> (lev - 1)) != (ci >> (lev - 1)))

    n = jnp.where(level_mask(1), -lm, 0.0)
    for lev in range(2, chunk.bit_length()):
        c = jnp.where(level_mask(lev), lm, 0.0)
        g = c + _dot(c.astype(BF16), n.astype(BF16))
        n = n - g - _dot(n.astype(BF16), g.astype(BF16))
    return n


def _gdn_prompt_kernel(x_ref, g_ref, cbuf_ref, win_ref, cw_ref, alog_ref, dtb_ref, nw_ref, wout_ref,
                       y_ref, sout_ref, cout_ref, s_scr, xp_scr, og_scr, *, heads, dk, dv, chunk, taps):
    t = pl.program_id(1)
    qkv_w = heads * (2 * dk + dv)
    hd = heads * dv
    halo = taps - 1
    base = SUBLANES

    @pl.when(t == 0)
    def _():
        s_scr[...] = jnp.zeros_like(s_scr)
        xp_scr[base - halo:base, :] = cbuf_ref[0]

    x = x_ref[...]
    hn = _rms(x, g_ref[...]).astype(BF16)
    qkv = _dot(hn, win_ref[:, :qkv_w])
    xp_scr[base:base + chunk, :] = qkv
    conv = qkv * cw_ref[halo:taps, :]
    for j in range(halo):
        conv = conv + xp_scr[base - halo + j:base - halo + j + chunk, :] * cw_ref[j:j + 1, :]
    tail = xp_scr[base + chunk - halo:base + chunk, :]
    xp_scr[base - halo:base, :] = tail

    @pl.when(t == pl.num_programs(1) - 1)
    def _():
        cout_ref[0] = tail

    act = _silu(conv)
    z = _dot(hn, win_ref[:, qkv_w:qkv_w + hd])
    beta = jax.nn.sigmoid(_dot(hn, win_ref[:, qkv_w + hd:qkv_w + 2 * hd]))
    a = -jnp.exp(alog_ref[...]) * _softplus(_dot(hn, win_ref[:, qkv_w + 2 * hd:qkv_w + 3 * hd]) + dtb_ref[...])
    ri = lax.broadcasted_iota(jnp.int32, (chunk, chunk), 0)
    ci = lax.broadcasted_iota(jnp.int32, (chunk, chunk), 1)
    tril = (ri >= ci).astype(BF16)
    bcum = sum(_dot(tril, term) for term in _bf16_terms(a))
    for h in range(heads):
        q = act[:, h * dk:(h + 1) * dk]
        k = act[:, heads * dk + h * dk:heads * dk + (h + 1) * dk]
        v = act[:, 2 * heads * dk + h * dv:2 * heads * dk + (h + 1) * dv]
        q = q * lax.rsqrt(jnp.sum(q * q, axis=-1, keepdims=True) + EPS) * dk ** -0.5
        k = k * lax.rsqrt(jnp.sum(k * k, axis=-1, keepdims=True) + EPS)
        bt = beta[:, h * dv:(h + 1) * dv]
        b = bcum[:, h * dv:(h + 1) * dv]
        diff = jnp.minimum(b - b.T, 0.0)
        e = jnp.exp(diff)
        kb = k.astype(BF16)
        qk_kk = _dot_nt(jnp.concatenate([q.astype(BF16), kb], axis=0), kb)
        p = jnp.where(ri >= ci, qk_kk[:chunk] * e, 0.0).astype(BF16)
        lm = jnp.where(ri > ci, qk_kk[chunk:] * e, 0.0) * bt
        n = _unit_lower_inverse_minus_identity(lm, ri, ci, chunk).astype(BF16)
        eb = jnp.exp(b)
        s = s_scr[h]
        kq_s = _dot(jnp.concatenate([(k * eb).astype(BF16), (q * eb).astype(BF16)], axis=0), s.astype(BF16))
        r = bt * (v - kq_s[:chunk])
        w = r + _dot(n, r.astype(BF16))
        wb = w.astype(BF16)
        o = kq_s[chunk:] + _dot(p, wb)
        b_last = b[chunk - 1:chunk, :]
        kd = k * jnp.exp(b_last - b)
        s_scr[h] = jnp.exp(b_last) * s + _dot(kd.T.astype(BF16), wb)
        o = o * lax.rsqrt(jnp.mean(o * o, axis=-1, keepdims=True) + EPS) * nw_ref[...]
        og_scr[:, h * dv:(h + 1) * dv] = (o * _silu(z[:, h * dv:(h + 1) * dv])).astype(BF16)
    y_ref[...] = x + _dot(og_scr[...], wout_ref[...])

    @pl.when(t == pl.num_programs(1) - 1)
    def _():
        sout_ref[0] = s_scr[...]


def _gdn_prompt(x, g, cbuf, w_in, conv_w, a_log, dt_bias, norm_w, w_out, batch, heads, dk, dv):
    m, d = x.shape
    seq = m // batch
    chunk = GDN_CHUNK
    taps = conv_w.shape[0]
    qkv_w = heads * (2 * dk + dv)
    assert seq % chunk == 0 and dk == chunk and dv == chunk and chunk == LANES
    nt = seq // chunk
    return pl.pallas_call(
        functools.partial(_gdn_prompt_kernel, heads=heads, dk=dk, dv=dv, chunk=chunk, taps=taps),
        name="gdn_prompt",
        grid=(batch, nt),
        in_specs=[
            pl.BlockSpec((chunk, d), lambda b, t: (b * nt + t, 0)),
            _resident((1, d)),
            pl.BlockSpec((1, taps - 1, qkv_w), lambda b, t: (b, 0, 0)),
            _resident(w_in.shape),
            _resident(conv_w.shape),
            _resident(a_log.shape),
            _resident(dt_bias.shape),
            _resident(norm_w.shape),
            _resident(w_out.shape),
        ],
        out_specs=[
            pl.BlockSpec((chunk, d), lambda b, t: (b * nt + t, 0)),
            pl.BlockSpec((1, heads, dk, dv), lambda b, t: (b, 0, 0, 0)),
            pl.BlockSpec((1, taps - 1, qkv_w), lambda b, t: (b, 0, 0)),
        ],
        out_shape=[
            jax.ShapeDtypeStruct((m, d), F32),
            jax.ShapeDtypeStruct((batch, heads, dk, dv), F32),
            jax.ShapeDtypeStruct((batch, taps - 1, qkv_w), F32),
        ],
        scratch_shapes=[
            pltpu.VMEM((heads, dk, dv), F32),
            pltpu.VMEM((SUBLANES + chunk, qkv_w), F32),
            pltpu.VMEM((chunk, heads * dv), BF16),
        ],
        compiler_params=_params("arbitrary", "arbitrary"),
    )(x, g, cbuf, w_in, conv_w, a_log, dt_bias, norm_w, w_out)


def _gdn_sample_proj_kernel(x_ref, g_ref, cbuf_ref, win_ref, cw_ref, alog_ref, dtb_ref,
                            q_ref, kt_ref, k_ref, v_ref, z_ref, beta_ref, ea_ref, cout_ref, *, heads, dk, dv, taps):
    qkv_w = heads * (2 * dk + dv)
    hd = heads * dv
    halo = taps - 1
    hn = _rms(x_ref[...], g_ref[...]).astype(BF16)
    qkv = _dot(hn, win_ref[:, :qkv_w])
    conv = qkv * cw_ref[halo:taps, :]
    for j in range(halo):
        conv = conv + cbuf_ref[j] * cw_ref[j:j + 1, :]
        if j > 0:
            cout_ref[j - 1] = cbuf_ref[j]
    cout_ref[halo - 1] = qkv
    act = _silu(conv)
    for h in range(heads):
        q = act[:, h * dk:(h + 1) * dk]
        k = act[:, heads * dk + h * dk:heads * dk + (h + 1) * dk]
        q_ref[:, h * dk:(h + 1) * dk] = q * lax.rsqrt(jnp.sum(q * q, axis=-1, keepdims=True) + EPS) * dk ** -0.5
        k_ref[:, h * dk:(h + 1) * dk] = k * lax.rsqrt(jnp.sum(k * k, axis=-1, keepdims=True) + EPS)
    kt_ref[...] = k_ref[...].T
    v_ref[...] = act[:, 2 * heads * dk:]
    z_ref[...] = _dot(hn, win_ref[:, qkv_w:qkv_w + hd])
    beta_ref[...] = jax.nn.sigmoid(_dot(hn, win_ref[:, qkv_w + hd:qkv_w + 2 * hd]))
    a = -jnp.exp(alog_ref[...]) * _softplus(_dot(hn, win_ref[:, qkv_w + 2 * hd:qkv_w + 3 * hd]) + dtb_ref[...])
    ea_ref[...] = jnp.exp(a)


def _gdn_sample_state_kernel(q_ref, kt_ref, k_ref, v_ref, beta_ref, ea_ref, s0_ref, s_ref, o_ref,
                             *, heads, dk, dv, block):
    n = v_ref.shape[0]
    lane = lax.broadcasted_iota(jnp.int32, (dk, n), 1)
    for i in range(block):
        b = pl.program_id(0) * block + i
        q_row, k_row, v_row = q_ref[pl.ds(b, 1), :], k_ref[pl.ds(b, 1), :], v_ref[pl.ds(b, 1), :]
        beta_row, ea_row = beta_ref[pl.ds(b, 1), :], ea_ref[pl.ds(b, 1), :]
        for h in range(heads):
            s0 = s0_ref[i, h]
            ea = ea_row[:, h * dv:(h + 1) * dv]
            k8 = jnp.broadcast_to(k_row[:, h * dk:(h + 1) * dk] * ea, (SUBLANES, dk)).astype(BF16)
            ks = _dot(k8, s0.astype(BF16))[0:1]
            w = beta_row[:, h * dv:(h + 1) * dv] * (v_row[:, h * dv:(h + 1) * dv] - ks)
            kt_b = jnp.where(lane == b, kt_ref[h * dk:(h + 1) * dk, :], 0.0).astype(BF16)
            s_new = ea * s0 + _dot(kt_b, jnp.broadcast_to(w, (n, dv)).astype(BF16))
            s_ref[i, h] = s_new
            q8 = jnp.broadcast_to(q_row[:, h * dk:(h + 1) * dk], (SUBLANES, dk)).astype(BF16)
            o_ref[i:i + 1, h * dv:(h + 1) * dv] = _dot(q8, s_new.astype(BF16))[0:1]


def _gdn_sample(x, g, cbuf, w_in, conv_w, a_log, dt_bias, norm_w, w_out, s0):
    n, d = x.shape
    _, heads, dk, dv = s0.shape
    taps = conv_w.shape[0]
    qkv_w = heads * (2 * dk + dv)
    hd = heads * dv
    block = math.gcd(GDN_STATE_BLOCK, n)
    cbuf_t = jnp.swapaxes(cbuf, 0, 1)
    q, kt, k, v, z, beta, ea, cout_t = pl.pallas_call(
        functools.partial(_gdn_sample_proj_kernel, heads=heads, dk=dk, dv=dv, taps=taps),
        name="gdn_sample_proj",
        out_shape=[
            jax.ShapeDtypeStruct((n, heads * dk), F32),
            jax.ShapeDtypeStruct((heads * dk, n), F32),
            jax.ShapeDtypeStruct((n, heads * dk), F32),
            jax.ShapeDtypeStruct((n, hd), F32),
            jax.ShapeDtypeStruct((n, hd), F32),
            jax.ShapeDtypeStruct((n, hd), F32),
            jax.ShapeDtypeStruct((n, hd), F32),
            jax.ShapeDtypeStruct((taps - 1, n, qkv_w), F32),
        ],
        compiler_params=_params(),
    )(x, g, cbuf_t, w_in, conv_w, a_log, dt_bias)
    s_new, o = pl.pallas_call(
        functools.partial(_gdn_sample_state_kernel, heads=heads, dk=dk, dv=dv, block=block),
        name="gdn_sample_state",
        grid=(n // block,),
        in_specs=[_resident(q.shape), _resident(kt.shape), _resident(k.shape), _resident(v.shape),
                  _resident(beta.shape), _resident(ea.shape),
                  pl.BlockSpec((block, heads, dk, dv), lambda i: (i, 0, 0, 0))],
        out_specs=[
            pl.BlockSpec((block, heads, dk, dv), lambda i: (i, 0, 0, 0)),
            pl.BlockSpec((block, hd), lambda i: (i, 0)),
        ],
        out_shape=[jax.ShapeDtypeStruct(s0.shape, F32), jax.ShapeDtypeStruct((n, hd), F32)],
        compiler_params=_params("arbitrary"),
    )(q, kt, k, v, beta, ea, s0)
    y = _mixer_out(x, o, z, norm_w, w_out, heads, dv)
    return y, s_new, jnp.swapaxes(cout_t, 0, 1)


def _rope_tables(pos, dk):
    theta = 1.0 / (ROPE_BASE ** jnp.linspace(0.0, 1.0, dk // 2, dtype=F32))
    ang = pos[:, None] * theta[None, :]
    return jnp.cos(ang), jnp.sin(ang)


def _prep_ret_weights(w_in, heads, dk):
    qk_w = 2 * heads * dk
    w_qk = w_in[:, :qk_w]
    col = jnp.arange(qk_w)
    within = col % dk
    deint = (col - within) + jnp.where(within < dk // 2, 2 * within, 2 * (within - dk // 2) + 1)
    w_deint = jnp.concatenate([w_qk[:, deint], w_in[:, qk_w:]], axis=1).astype(BF16)
    swapped = w_qk[:, col ^ 1] * jnp.where(col % 2 == 0, -1.0, 1.0)[None, :]
    return w_in.astype(BF16), w_deint, swapped.astype(BF16)


def _prep_gdn_weights(w_in, a_log, dt_bias, heads, dk, dv):
    qkv_w = heads * (2 * dk + dv)
    hd = heads * dv
    w_beta = jnp.repeat(w_in[:, qkv_w + hd:qkv_w + hd + heads], dv, axis=1)
    w_a = jnp.repeat(w_in[:, qkv_w + hd + heads:], dv, axis=1)
    w = jnp.concatenate([w_in[:, :qkv_w + hd], w_beta, w_a], axis=1).astype(BF16)
    return w, jnp.repeat(a_log, dv)[None, :], jnp.repeat(dt_bias, dv)[None, :]


def _trunk(x, batch, pos0, s_ret, s_gdn, s_conv, prompt, norm_g, ffn_gu, ffn_down, ret_w, ret_w_out, gdn_w,
           gdn_conv_w, gdn_norm_w, gdn_w_out, final_g):
    depth = norm_g.shape[0]
    n_mixers = 2
    m, d = x.shape
    seq = m // batch
    _, _, ret_heads, ret_dk, ret_dv = s_ret.shape
    _, _, gdn_heads, gdn_dk, gdn_dv = s_gdn.shape
    cos, sin = _rope_tables(jnp.arange(seq, dtype=F32) + pos0, ret_dk)
    if not prompt:
        cos = jnp.tile(jnp.repeat(cos, 2, axis=1), (1, 2 * ret_heads))
        sin = jnp.tile(jnp.repeat(sin, 2, axis=1), (1, 2 * ret_heads))
    ones_dv = jnp.ones((1, ret_dv), F32)
    fg = final_g[None, :]
    new_ret, new_gdn, new_conv = [], [], []
    for i in range(depth):
        x = _ffn(x, norm_g[i, 0][None, :], ffn_gu[i, 0], ffn_down[i, 0], fg, False)
        j = i // n_mixers
        g = norm_g[i, 1][None, :]
        if i % n_mixers == 0:
            w_orig, w_deint, w_swap = ret_w[j]
            if prompt:
                x, s = _ret_prompt(x, g, cos, sin, w_deint, ret_w_out[j], batch, ret_heads, ret_dk, ret_dv)
            else:
                x, s = _ret_sample(x, g, cos, sin, w_orig, w_swap, ret_w_out[j], s_ret[j], ones_dv)
            new_ret.append(s)
        else:
            w, a_log, dt_bias = gdn_w[j]
            nw = gdn_norm_w[j][None, :]
            if prompt:
                x, s, buf = _gdn_prompt(x, g, s_conv[j], w, gdn_conv_w[j], a_log, dt_bias, nw, gdn_w_out[j],
                                        batch, gdn_heads, gdn_dk, gdn_dv)
            else:
                x, s, buf = _gdn_sample(x, g, s_conv[j], w, gdn_conv_w[j], a_log, dt_bias, nw, gdn_w_out[j],
                                        s_gdn[j])
            new_gdn.append(s)
            new_conv.append(buf)
        x = _ffn(x, norm_g[i, 2][None, :], ffn_gu[i, 1], ffn_down[i, 1], fg, i == depth - 1)
    return x, jnp.stack(new_ret), jnp.stack(new_gdn), jnp.stack(new_conv)


def kernel(x_prompt, x_sample, state_ret, state_gdn, state_gdn_conv, norm_g, ffn_gu, ffn_down, ret_w_in,
           ret_w_out, gdn_w_in, gdn_conv_w, gdn_a_log, gdn_dt_bias, gdn_norm_w, gdn_w_out, final_g):
    bp, seq, d = x_prompt.shape
    bs, dec_seq, _ = x_sample.shape
    assert dec_seq == 1
    n_ret, _, ret_heads, ret_dk, ret_dv = state_ret.shape
    n_gdn, _, gdn_heads, gdn_dk, gdn_dv = state_gdn.shape
    past_len = SAMPLE_PAST_LEN

    ffn_gu_b = ffn_gu.astype(BF16)
    ffn_down_b = ffn_down.astype(BF16)
    ret_w = [_prep_ret_weights(ret_w_in[j], ret_heads, ret_dk) for j in range(n_ret)]
    ret_w_out_b = ret_w_out.astype(BF16)
    gdn_w = [_prep_gdn_weights(gdn_w_in[j], gdn_a_log[j], gdn_dt_bias[j], gdn_heads, gdn_dk, gdn_dv)
             for j in range(n_gdn)]
    gdn_w_out_b = gdn_w_out.astype(BF16)
    shared = (norm_g, ffn_gu_b, ffn_down_b, ret_w, ret_w_out_b, gdn_w, gdn_conv_w, gdn_norm_w, gdn_w_out_b, final_g)

    zero_ret = jnp.zeros((n_ret, bp, ret_heads, ret_dk, ret_dv), state_ret.dtype)
    zero_gdn = jnp.zeros((n_gdn, bp, gdn_heads, gdn_dk, gdn_dv), state_gdn.dtype)
    zero_conv = jnp.zeros((n_gdn, bp) + state_gdn_conv.shape[2:], state_gdn_conv.dtype)
    y_p, ret_p, gdn_p, conv_p = _trunk(x_prompt.reshape(bp * seq, d), bp, 0.0, zero_ret, zero_gdn, zero_conv,
                                       True, *shared)
    y_s, ret_s, gdn_s, conv_s = _trunk(x_sample.reshape(bs, d), bs, past_len, state_ret, state_gdn,
                                       state_gdn_conv, False, *shared)
    return (y_p.reshape(bp, seq, d), y_s.reshape(bs, dec_seq, d), ret_p, gdn_p, conv_p, ret_s, gdn_s, conv_s)
```

```python
import functools
import math

import jax
import jax.numpy as jnp
from jax import lax
from jax.experimental import pallas as pl
from jax.experimental.pallas import tpu as pltpu

F32 = jnp.float32
BF16 = jnp.bfloat16
EPS = 1e-6
ROPE_BASE = 10000.0
SAMPLE_PAST_LEN = 16384.0

LANES = 128
SUBLANES = 8
VMEM_LIMIT_BYTES = 56 * 1024 * 1024

FFN_ROWS = 512
RET_CHUNK = 256
GDN_CHUNK = 128
RET_STATE_BLOCK = 2
GDN_STATE_BLOCK = 8


def _dot(a, b):
    return jnp.dot(a, b, preferred_element_type=F32)


def _dot_nt(a, b):
    return lax.dot_general(a, b, (((1,), (1,)), ((), ())), preferred_element_type=F32)


def _rms(x, g):
    return x * lax.rsqrt(jnp.mean(x * x, axis=-1, keepdims=True) + EPS) * g


def _silu(x):
    return x * jax.nn.sigmoid(x)


def _softplus(x):
    return jnp.maximum(x, 0.0) + jnp.log1p(jnp.exp(-jnp.abs(x)))


def _bf16_terms(x):
    t0 = x.astype(BF16)
    r = x - t0.astype(F32)
    t1 = r.astype(BF16)
    t2 = (r - t1.astype(F32)).astype(BF16)
    return t0, t1, t2


def _resident(shape):
    nd = len(shape)
    return pl.BlockSpec(shape, lambda *_: (0,) * nd, pipeline_mode=pl.Buffered(1))


def _stacked_state(args, in_specs, prev):
    if prev is None:
        return args, in_specs, {}
    return args + [prev], in_specs + [pl.BlockSpec(memory_space=pl.ANY)], {len(args): 0}


def _params(*sem):
    return pltpu.CompilerParams(dimension_semantics=sem, vmem_limit_bytes=VMEM_LIMIT_BYTES)


def _ffn_kernel(x_ref, g_ref, wg_ref, wu_ref, wd_ref, fg_ref, o_ref, *, final):
    x = x_ref[...]
    h = _rms(x, g_ref[...]).astype(BF16)
    a = _dot(h, wg_ref[...])
    b = _dot(h, wu_ref[...])
    act = (_silu(a) * b).astype(BF16)
    y = x + 0.5 * _dot(act, wd_ref[...])
    if final:
        y = _rms(y, fg_ref[...])
    o_ref[...] = y


def _ffn(x, g, w_gu, w_down, final_g, final):
    m, d = x.shape
    f = w_down.shape[0]
    tm = min(FFN_ROWS, m)
    assert m % tm == 0 and f % LANES == 0
    return pl.pallas_call(
        functools.partial(_ffn_kernel, final=final),
        name="ffn",
        grid=(m // tm,),
        in_specs=[
            pl.BlockSpec((tm, d), lambda i: (i, 0)),
            _resident((1, d)),
            pl.BlockSpec((d, f), lambda i: (0, 0), pipeline_mode=pl.Buffered(1)),
            pl.BlockSpec((d, f), lambda i: (0, 1), pipeline_mode=pl.Buffered(1)),
            _resident((f, d)),
            _resident((1, d)),
        ],
        out_specs=pl.BlockSpec((tm, d), lambda i: (i, 0)),
        out_shape=jax.ShapeDtypeStruct((m, d), F32),
        compiler_params=_params("arbitrary"),
    )(x, g, w_gu, w_gu, w_down, final_g)


def _ret_prompt_kernel(*refs, heads, dk, dv, chunk):
    x_ref, g_ref, cos_ref, sin_ref, win_ref, wout_ref = refs[:6]
    sout_ref, y_ref, s_scr, og_scr = refs[-4:]
    t = pl.program_id(1)
    half = dk // 2

    @pl.when(t == 0)
    def _():
        s_scr[...] = jnp.zeros_like(s_scr)

    x = x_ref[...]
    hn = _rms(x, g_ref[...]).astype(BF16)
    cos = cos_ref[...]
    sin = sin_ref[...]
    ri = lax.broadcasted_iota(jnp.int32, (chunk, chunk), 0)
    ci = lax.broadcasted_iota(jnp.int32, (chunk, chunk), 1)
    lag = (ri - ci).astype(F32)
    row = lax.broadcasted_iota(jnp.int32, (chunk, half), 0).astype(F32)
    k_off, v_off, g_off = heads * dk, 2 * heads * dk, 2 * heads * dk + heads * dv
    for h in range(heads):
        log_gamma = math.log(1.0 - 2.0 ** (-5.0 - h))
        q = _dot(hn, win_ref[:, h * dk:(h + 1) * dk])
        k = _dot(hn, win_ref[:, k_off + h * dk:k_off + (h + 1) * dk])
        v = _dot(hn, win_ref[:, v_off + h * dv:v_off + (h + 1) * dv]).astype(BF16)
        gate = _dot(hn, win_ref[:, g_off + h * dv:g_off + (h + 1) * dv])
        q1, q2 = q[:, :half], q[:, half:]
        k1, k2 = k[:, :half], k[:, half:]
        qr1, qr2 = q1 * cos - q2 * sin, q2 * cos + q1 * sin
        kr1, kr2 = (k1 * cos - k2 * sin) * dk ** -0.5, (k2 * cos + k1 * sin) * dk ** -0.5
        qb = jnp.concatenate([qr1, qr2], axis=1).astype(BF16)
        kb = jnp.concatenate([kr1, kr2], axis=1).astype(BF16)
        decay = jnp.where(ri >= ci, jnp.exp(lag * log_gamma), 0.0)
        p = (_dot_nt(qb, kb) * decay).astype(BF16)
        q_scale = jnp.exp((row + 1.0) * log_gamma)
        k_scale = jnp.exp((chunk - 1.0 - row) * log_gamma)
        qd = jnp.concatenate([qr1 * q_scale, qr2 * q_scale], axis=1).astype(BF16)
        kd = jnp.concatenate([kr1 * k_scale, kr2 * k_scale], axis=1)
        s = s_scr[h]
        o = _dot(p, v) + _dot(qd, s.astype(BF16))
        s_scr[h] = math.exp(chunk * log_gamma) * s + _dot(kd.T.astype(BF16), v)
        o = o * lax.rsqrt(jnp.mean(o * o, axis=-1, keepdims=True) + EPS) * _silu(gate)
        og_scr[:, h * dv:(h + 1) * dv] = o.astype(BF16)
    y_ref[...] = x + _dot(og_scr[...], wout_ref[...])

    @pl.when(t == pl.num_programs(1) - 1)
    def _():
        ro = lax.broadcasted_iota(jnp.int32, (dk, dk), 0)
        rin = lax.broadcasted_iota(jnp.int32, (dk, dk), 1)
        perm = (rin == (ro >> 1) + (ro & 1) * half).astype(BF16)
        for h in range(heads):
            sout_ref[0, 0, h] = sum(_dot(perm, term) for term in _bf16_terms(s_scr[h]))


def _ret_prompt(x, g, cos, sin, w_in, w_out, batch, heads, dk, dv, layers, layer, prev):
    m, d = x.shape
    seq = m // batch
    chunk = min(RET_CHUNK, seq)
    assert seq % chunk == 0
    nt = seq // chunk
    args, in_specs, aliases = _stacked_state(
        [x, g, cos, sin, w_in, w_out],
        [pl.BlockSpec((chunk, d), lambda b, t: (b * nt + t, 0)),
         _resident((1, d)),
         pl.BlockSpec((chunk, dk // 2), lambda b, t: (t, 0)),
         pl.BlockSpec((chunk, dk // 2), lambda b, t: (t, 0)),
         _resident(w_in.shape),
         _resident(w_out.shape)],
        prev)
    return pl.pallas_call(
        functools.partial(_ret_prompt_kernel, heads=heads, dk=dk, dv=dv, chunk=chunk),
        name="ret_prompt",
        grid=(batch, nt),
        in_specs=in_specs,
        out_specs=[
            pl.BlockSpec((1, 1, heads, dk, dv), lambda b, t: (layer, b, 0, 0, 0)),
            pl.BlockSpec((chunk, d), lambda b, t: (b * nt + t, 0)),
        ],
        out_shape=[
            jax.ShapeDtypeStruct((layers, batch, heads, dk, dv), F32),
            jax.ShapeDtypeStruct((m, d), F32),
        ],
        input_output_aliases=aliases,
        scratch_shapes=[pltpu.VMEM((heads, dk, dv), F32), pltpu.VMEM((chunk, heads * dv), BF16)],
        compiler_params=_params("arbitrary", "arbitrary"),
    )(*args)


def _ret_sample_proj_kernel(x_ref, g_ref, cos_ref, sin_ref, win_ref, wsw_ref, q_ref, kt_ref, v_ref, gate_ref,
                            *, heads, dk, dv):
    qk_w = 2 * heads * dk
    hn = _rms(x_ref[...], g_ref[...]).astype(BF16)
    qk = _dot(hn, win_ref[:, :qk_w]) * cos_ref[...] + _dot(hn, wsw_ref[...]) * sin_ref[...]
    q_ref[...] = qk[:, :heads * dk]
    kt_ref[...] = (qk[:, heads * dk:] * dk ** -0.5).T
    v_ref[...] = _dot(hn, win_ref[:, qk_w:qk_w + heads * dv])
    gate_ref[...] = _dot(hn, win_ref[:, qk_w + heads * dv:])


def _ret_sample_state_kernel(*refs, heads, dk, dv, block):
    q_ref, kt_ref, v_ref, s0_ref = refs[:4]
    s_ref, o_ref = refs[-2:]
    n = v_ref.shape[0]
    lane = lax.broadcasted_iota(jnp.int32, (dk, n), 1)
    for i in range(block):
        b = pl.program_id(0) * block + i
        q_row = q_ref[pl.ds(b, 1), :]
        for h in range(heads):
            gamma = 1.0 - 2.0 ** (-5.0 - h)
            kt_b = jnp.where(lane == b, kt_ref[h * dk:(h + 1) * dk, :], 0.0).astype(BF16)
            v_h = v_ref[:, h * dv:(h + 1) * dv].astype(BF16)
            s_new = gamma * s0_ref[0, i, h] + _dot(kt_b, v_h)
            s_ref[0, i, h] = s_new
            q8 = jnp.broadcast_to(q_row[:, h * dk:(h + 1) * dk], (SUBLANES, dk)).astype(BF16)
            o_ref[0, i:i + 1, h * dv:(h + 1) * dv] = _dot(q8, s_new.astype(BF16))[0:1]


def _mixer_out_kernel(x_ref, o_ref, gate_ref, nw_ref, wout_ref, y_ref, *, heads, dv):
    parts = []
    for h in range(heads):
        o = o_ref[:, h * dv:(h + 1) * dv]
        o = o * lax.rsqrt(jnp.mean(o * o, axis=-1, keepdims=True) + EPS) * nw_ref[...]
        parts.append((o * _silu(gate_ref[:, h * dv:(h + 1) * dv])).astype(BF16))
    y_ref[...] = x_ref[...] + _dot(jnp.concatenate(parts, axis=1), wout_ref[...])


def _mixer_out(x, o, gate, norm_w, w_out, heads, dv):
    return pl.pallas_call(
        functools.partial(_mixer_out_kernel, heads=heads, dv=dv),
        name="mixer_out",
        out_shape=jax.ShapeDtypeStruct(x.shape, F32),
        compiler_params=_params(),
    )(x, o, gate, norm_w, w_out)


def _ret_sample(x, g, cos, sin, w_in, w_sw, w_out, s0, ones_dv, layer, prev):
    n, d = x.shape
    _, _, heads, dk, dv = s0.shape
    block = math.gcd(RET_STATE_BLOCK, n)
    q, kt, v, gate = pl.pallas_call(
        functools.partial(_ret_sample_proj_kernel, heads=heads, dk=dk, dv=dv),
        name="ret_sample_proj",
        out_shape=[
            jax.ShapeDtypeStruct((n, heads * dk), F32),
            jax.ShapeDtypeStruct((heads * dk, n), F32),
            jax.ShapeDtypeStruct((n, heads * dv), F32),
            jax.ShapeDtypeStruct((n, heads * dv), F32),
        ],
        compiler_params=_params(),
    )(x, g, cos, sin, w_in, w_sw)
    state_spec = pl.BlockSpec((1, block, heads, dk, dv), lambda i: (layer, i, 0, 0, 0))
    args, in_specs, aliases = _stacked_state(
        [q, kt, v, s0], [_resident(q.shape), _resident(kt.shape), _resident(v.shape), state_spec], prev)
    s_new, o = pl.pallas_call(
        functools.partial(_ret_sample_state_kernel, heads=heads, dk=dk, dv=dv, block=block),
        name="ret_sample_state",
        grid=(n // block,),
        in_specs=in_specs,
        out_specs=[state_spec, pl.BlockSpec((1, block, heads * dv), lambda i: (i, 0, 0))],
        out_shape=[
            jax.ShapeDtypeStruct(s0.shape, F32),
            jax.ShapeDtypeStruct((n // block, block, heads * dv), F32),
        ],
        input_output_aliases=aliases,
        compiler_params=_params("arbitrary"),
    )(*args)
    y = _mixer_out(x, o.reshape(n, heads * dv), gate, ones_dv, w_out, heads, dv)
    return y, s_new


def _unit_lower_inverse_minus_identity(lms, ri, ci, chunk):
    def level_mask(lev):
        return ((ri >> lev) == (ci >> lev)) & ((ri >> (lev - 1)) != (ci >> (lev - 1)))

    mask = level_mask(1)
    ns = [jnp.where(mask, -lm, 0.0) for lm in lms]
    for lev in range(2, chunk.bit_length()):
        mask = level_mask(lev)
        cs = [jnp.where(mask, lm, 0.0) for lm in lms]
        gs = [c + _dot(c.astype(BF16), n.astype(BF16)) for c, n in zip(cs, ns)]
        ns = [n - g - _dot(n.astype(BF16), g.astype(BF16)) for n, g in zip(ns, gs)]
    return ns


def _gdn_prompt_kernel(*refs, heads, dk, dv, chunk, taps):
    x_ref, g_ref, cbuf_ref, win_ref, cw_ref, alog_ref, dtb_ref, nw_ref, wout_ref = refs[:9]
    sout_ref, y_ref, cout_ref, s_scr, xp_scr, og_scr = refs[-6:]
    t = pl.program_id(1)
    qkv_w = heads * (2 * dk + dv)
    hd = heads * dv
    halo = taps - 1
    base = SUBLANES

    @pl.when(t == 0)
    def _():
        s_scr[...] = jnp.zeros_like(s_scr)
        xp_scr[base - halo:base, :] = cbuf_ref[0]

    x = x_ref[...]
    hn = _rms(x, g_ref[...]).astype(BF16)
    qkv = _dot(hn, win_ref[:, :qkv_w])
    xp_scr[base:base + chunk, :] = qkv
    conv = qkv * cw_ref[halo:taps, :]
    for j in range(halo):
        conv = conv + xp_scr[base - halo + j:base - halo + j + chunk, :] * cw_ref[j:j + 1, :]
    tail = xp_scr[base + chunk - halo:base + chunk, :]
    xp_scr[base - halo:base, :] = tail

    @pl.when(t == pl.num_programs(1) - 1)
    def _():
        cout_ref[0] = tail

    act = _silu(conv)
    z = _dot(hn, win_ref[:, qkv_w:qkv_w + hd])
    beta = jax.nn.sigmoid(_dot(hn, win_ref[:, qkv_w + hd:qkv_w + 2 * hd]))
    a = -jnp.exp(alog_ref[...]) * _softplus(_dot(hn, win_ref[:, qkv_w + 2 * hd:qkv_w + 3 * hd]) + dtb_ref[...])
    ri = lax.broadcasted_iota(jnp.int32, (chunk, chunk), 0)
    ci = lax.broadcasted_iota(jnp.int32, (chunk, chunk), 1)
    tril = (ri >= ci).astype(BF16)
    bcum = sum(_dot(tril, term) for term in _bf16_terms(a))
    hs = range(heads)
    qs = [act[:, h * dk:(h + 1) * dk] for h in hs]
    ks = [act[:, heads * dk + h * dk:heads * dk + (h + 1) * dk] for h in hs]
    vs = [act[:, 2 * heads * dk + h * dv:2 * heads * dk + (h + 1) * dv] for h in hs]
    qs = [q * lax.rsqrt(jnp.sum(q * q, axis=-1, keepdims=True) + EPS) * dk ** -0.5 for q in qs]
    ks = [k * lax.rsqrt(jnp.sum(k * k, axis=-1, keepdims=True) + EPS) for k in ks]
    bts = [beta[:, h * dv:(h + 1) * dv] for h in hs]
    bs = [bcum[:, h * dv:(h + 1) * dv] for h in hs]
    es = [jnp.exp(jnp.minimum(b - b.T, 0.0)) for b in bs]
    kbs = [k.astype(BF16) for k in ks]
    qk_kks = [_dot_nt(jnp.concatenate([q.astype(BF16), kb], axis=0), kb) for q, kb in zip(qs, kbs)]
    ps = [jnp.where(ri >= ci, qk_kk[:chunk] * e, 0.0).astype(BF16) for qk_kk, e in zip(qk_kks, es)]
    lms = [jnp.where(ri > ci, qk_kk[chunk:] * e, 0.0) * bt for qk_kk, e, bt in zip(qk_kks, es, bts)]
    ebs = [jnp.exp(b) for b in bs]
    ss = [s_scr[h] for h in hs]
    kq_ss = [_dot(jnp.concatenate([(k * eb).astype(BF16), (q * eb).astype(BF16)], axis=0), s.astype(BF16))
             for k, q, eb, s in zip(ks, qs, ebs, ss)]
    rs = [bt * (v - kq_s[:chunk]) for bt, v, kq_s in zip(bts, vs, kq_ss)]
    ns = _unit_lower_inverse_minus_identity(lms, ri, ci, chunk)
    wbs = [(r + _dot(n.astype(BF16), r.astype(BF16))).astype(BF16) for r, n in zip(rs, ns)]
    os_ = [kq_s[chunk:] + _dot(p, wb) for kq_s, p, wb in zip(kq_ss, ps, wbs)]
    b_lasts = [b[chunk - 1:chunk, :] for b in bs]
    kds = [k * jnp.exp(b_last - b) for k, b, b_last in zip(ks, bs, b_lasts)]
    s_news = [jnp.exp(b_last) * s + _dot(kd.T.astype(BF16), wb) for b_last, s, kd, wb in zip(b_lasts, ss, kds, wbs)]
    for h in hs:
        s_scr[h] = s_news[h]
        o = os_[h]
        o = o * lax.rsqrt(jnp.mean(o * o, axis=-1, keepdims=True) + EPS) * nw_ref[...]
        og_scr[:, h * dv:(h + 1) * dv] = (o * _silu(z[:, h * dv:(h + 1) * dv])).astype(BF16)
    y_ref[...] = x + _dot(og_scr[...], wout_ref[...])

    @pl.when(t == pl.num_programs(1) - 1)
    def _():
        sout_ref[0, 0] = s_scr[...]


def _gdn_prompt(x, g, cbuf, w_in, conv_w, a_log, dt_bias, norm_w, w_out, batch, heads, dk, dv, layers, layer, prev):
    m, d = x.shape
    seq = m // batch
    chunk = GDN_CHUNK
    taps = conv_w.shape[0]
    qkv_w = heads * (2 * dk + dv)
    assert seq % chunk == 0 and dk == chunk and dv == chunk and chunk == LANES
    nt = seq // chunk
    args, in_specs, aliases = _stacked_state(
        [x, g, cbuf, w_in, conv_w, a_log, dt_bias, norm_w, w_out],
        [pl.BlockSpec((chunk, d), lambda b, t: (b * nt + t, 0)),
         _resident((1, d)),
         pl.BlockSpec((1, taps - 1, qkv_w), lambda b, t: (b, 0, 0)),
         _resident(w_in.shape),
         _resident(conv_w.shape),
         _resident(a_log.shape),
         _resident(dt_bias.shape),
         _resident(norm_w.shape),
         _resident(w_out.shape)],
        prev)
    return pl.pallas_call(
        functools.partial(_gdn_prompt_kernel, heads=heads, dk=dk, dv=dv, chunk=chunk, taps=taps),
        name="gdn_prompt",
        grid=(batch, nt),
        in_specs=in_specs,
        out_specs=[
            pl.BlockSpec((1, 1, heads, dk, dv), lambda b, t: (layer, b, 0, 0, 0)),
            pl.BlockSpec((chunk, d), lambda b, t: (b * nt + t, 0)),
            pl.BlockSpec((1, taps - 1, qkv_w), lambda b, t: (b, 0, 0)),
        ],
        out_shape=[
            jax.ShapeDtypeStruct((layers, batch, heads, dk, dv), F32),
            jax.ShapeDtypeStruct((m, d), F32),
            jax.ShapeDtypeStruct((batch, taps - 1, qkv_w), F32),
        ],
        input_output_aliases=aliases,
        scratch_shapes=[
            pltpu.VMEM((heads, dk, dv), F32),
            pltpu.VMEM((SUBLANES + chunk, qkv_w), F32),
            pltpu.VMEM((chunk, heads * dv), BF16),
        ],
        compiler_params=_params("arbitrary", "arbitrary"),
    )(*args)


def _gdn_sample_proj_kernel(x_ref, g_ref, cbuf_ref, win_ref, cw_ref, alog_ref, dtb_ref,
                            q_ref, kt_ref, k_ref, v_ref, z_ref, beta_ref, ea_ref, cout_ref, *, heads, dk, dv, taps):
    qkv_w = heads * (2 * dk + dv)
    hd = heads * dv
    halo = taps - 1
    hn = _rms(x_ref[...], g_ref[...]).astype(BF16)
    qkv = _dot(hn, win_ref[:, :qkv_w])
    conv = qkv * cw_ref[halo:taps, :]
    for j in range(halo):
        conv = conv + cbuf_ref[j] * cw_ref[j:j + 1, :]
        if j > 0:
            cout_ref[j - 1] = cbuf_ref[j]
    cout_ref[halo - 1] = qkv
    act = _silu(conv)
    for h in range(heads):
        q = act[:, h * dk:(h + 1) * dk]
        k = act[:, heads * dk + h * dk:heads * dk + (h + 1) * dk]
        q_ref[:, h * dk:(h + 1) * dk] = q * lax.rsqrt(jnp.sum(q * q, axis=-1, keepdims=True) + EPS) * dk ** -0.5
        k_ref[:, h * dk:(h + 1) * dk] = k * lax.rsqrt(jnp.sum(k * k, axis=-1, keepdims=True) + EPS)
    kt_ref[...] = k_ref[...].T
    v_ref[...] = act[:, 2 * heads * dk:]
    z_ref[...] = _dot(hn, win_ref[:, qkv_w:qkv_w + hd])
    beta_ref[...] = jax.nn.sigmoid(_dot(hn, win_ref[:, qkv_w + hd:qkv_w + 2 * hd]))
    a = -jnp.exp(alog_ref[...]) * _softplus(_dot(hn, win_ref[:, qkv_w + 2 * hd:qkv_w + 3 * hd]) + dtb_ref[...])
    ea_ref[...] = jnp.exp(a)


def _gdn_sample_state_kernel(*refs, heads, dk, dv, block):
    q_ref, kt_ref, k_ref, v_ref, beta_ref, ea_ref, s0_ref = refs[:7]
    s_ref, o_ref = refs[-2:]
    n = v_ref.shape[0]
    lane = lax.broadcasted_iota(jnp.int32, (dk, n), 1)
    for i in range(block):
        b = pl.program_id(0) * block + i
        q_row, k_row, v_row = q_ref[pl.ds(b, 1), :], k_ref[pl.ds(b, 1), :], v_ref[pl.ds(b, 1), :]
        beta_row, ea_row = beta_ref[pl.ds(b, 1), :], ea_ref[pl.ds(b, 1), :]
        for h in range(heads):
            s0 = s0_ref[0, i, h]
            ea = ea_row[:, h * dv:(h + 1) * dv]
            k8 = jnp.broadcast_to(k_row[:, h * dk:(h + 1) * dk] * ea, (SUBLANES, dk)).astype(BF16)
            ks = _dot(k8, s0.astype(BF16))[0:1]
            w = beta_row[:, h * dv:(h + 1) * dv] * (v_row[:, h * dv:(h + 1) * dv] - ks)
            kt_b = jnp.where(lane == b, kt_ref[h * dk:(h + 1) * dk, :], 0.0).astype(BF16)
            s_new = ea * s0 + _dot(kt_b, jnp.broadcast_to(w, (n, dv)).astype(BF16))
            s_ref[0, i, h] = s_new
            q8 = jnp.broadcast_to(q_row[:, h * dk:(h + 1) * dk], (SUBLANES, dk)).astype(BF16)
            o_ref[i:i + 1, h * dv:(h + 1) * dv] = _dot(q8, s_new.astype(BF16))[0:1]


def _gdn_sample(x, g, cbuf, w_in, conv_w, a_log, dt_bias, norm_w, w_out, s0, layer, prev):
    n, d = x.shape
    _, _, heads, dk, dv = s0.shape
    taps = conv_w.shape[0]
    qkv_w = heads * (2 * dk + dv)
    hd = heads * dv
    block = math.gcd(GDN_STATE_BLOCK, n)
    cbuf_t = jnp.swapaxes(cbuf, 0, 1)
    q, kt, k, v, z, beta, ea, cout_t = pl.pallas_call(
        functools.partial(_gdn_sample_proj_kernel, heads=heads, dk=dk, dv=dv, taps=taps),
        name="gdn_sample_proj",
        out_shape=[
            jax.ShapeDtypeStruct((n, heads * dk), F32),
            jax.ShapeDtypeStruct((heads * dk, n), F32),
            jax.ShapeDtypeStruct((n, heads * dk), F32),
            jax.ShapeDtypeStruct((n, hd), F32),
            jax.ShapeDtypeStruct((n, hd), F32),
            jax.ShapeDtypeStruct((n, hd), F32),
            jax.ShapeDtypeStruct((n, hd), F32),
            jax.ShapeDtypeStruct((taps - 1, n, qkv_w), F32),
        ],
        compiler_params=_params(),
    )(x, g, cbuf_t, w_in, conv_w, a_log, dt_bias)
    state_spec = pl.BlockSpec((1, block, heads, dk, dv), lambda i: (layer, i, 0, 0, 0))
    args, in_specs, aliases = _stacked_state(
        [q, kt, k, v, beta, ea, s0],
        [_resident(q.shape), _resident(kt.shape), _resident(k.shape), _resident(v.shape),
         _resident(beta.shape), _resident(ea.shape), state_spec],
        prev)
    s_new, o = pl.pallas_call(
        functools.partial(_gdn_sample_state_kernel, heads=heads, dk=dk, dv=dv, block=block),
        name="gdn_sample_state",
        grid=(n // block,),
        in_specs=in_specs,
        out_specs=[state_spec, pl.BlockSpec((block, hd), lambda i: (i, 0))],
        out_shape=[jax.ShapeDtypeStruct(s0.shape, F32), jax.ShapeDtypeStruct((n, hd), F32)],
        input_output_aliases=aliases,
        compiler_params=_params("arbitrary"),
    )(*args)
    y = _mixer_out(x, o, z, norm_w, w_out, heads, dv)
    return y, s_new, jnp.swapaxes(cout_t, 0, 1)


def _rope_tables(pos, dk):
    theta = 1.0 / (ROPE_BASE ** jnp.linspace(0.0, 1.0, dk // 2, dtype=F32))
    ang = pos[:, None] * theta[None, :]
    return jnp.cos(ang), jnp.sin(ang)


def _prep_ret_weights(w_in, heads, dk):
    qk_w = 2 * heads * dk
    w_qk = w_in[:, :qk_w]
    col = jnp.arange(qk_w)
    within = col % dk
    deint = (col - within) + jnp.where(within < dk // 2, 2 * within, 2 * (within - dk // 2) + 1)
    w_deint = jnp.concatenate([w_qk[:, deint], w_in[:, qk_w:]], axis=1).astype(BF16)
    swapped = w_qk[:, col ^ 1] * jnp.where(col % 2 == 0, -1.0, 1.0)[None, :]
    return w_in.astype(BF16), w_deint, swapped.astype(BF16)


def _prep_gdn_weights(w_in, a_log, dt_bias, heads, dk, dv):
    qkv_w = heads * (2 * dk + dv)
    hd = heads * dv
    w_beta = jnp.repeat(w_in[:, qkv_w + hd:qkv_w + hd + heads], dv, axis=1)
    w_a = jnp.repeat(w_in[:, qkv_w + hd + heads:], dv, axis=1)
    w = jnp.concatenate([w_in[:, :qkv_w + hd], w_beta, w_a], axis=1).astype(BF16)
    return w, jnp.repeat(a_log, dv)[None, :], jnp.repeat(dt_bias, dv)[None, :]


def _trunk(x, batch, pos0, s_ret, s_gdn, s_conv, prompt, norm_g, ffn_gu, ffn_down, ret_w, ret_w_out, gdn_w,
           gdn_conv_w, gdn_norm_w, gdn_w_out, final_g):
    depth = norm_g.shape[0]
    n_mixers = 2
    m, d = x.shape
    seq = m // batch
    n_ret, _, ret_heads, ret_dk, ret_dv = s_ret.shape
    n_gdn, _, gdn_heads, gdn_dk, gdn_dv = s_gdn.shape
    cos, sin = _rope_tables(jnp.arange(seq, dtype=F32) + pos0, ret_dk)
    if not prompt:
        cos = jnp.tile(jnp.repeat(cos, 2, axis=1), (1, 2 * ret_heads))
        sin = jnp.tile(jnp.repeat(sin, 2, axis=1), (1, 2 * ret_heads))
    ones_dv = jnp.ones((1, ret_dv), F32)
    fg = final_g[None, :]
    new_ret, new_gdn, new_conv = None, None, []
    for i in range(depth):
        x = _ffn(x, norm_g[i, 0][None, :], ffn_gu[i, 0], ffn_down[i, 0], fg, False)
        j = i // n_mixers
        g = norm_g[i, 1][None, :]
        if i % n_mixers == 0:
            w_orig, w_deint, w_swap = ret_w[j]
            if prompt:
                new_ret, x = _ret_prompt(x, g, cos, sin, w_deint, ret_w_out[j], batch, ret_heads, ret_dk, ret_dv,
                                         n_ret, j, new_ret)
            else:
                x, new_ret = _ret_sample(x, g, cos, sin, w_orig, w_swap, ret_w_out[j], s_ret, ones_dv, j, new_ret)
        else:
            w, a_log, dt_bias = gdn_w[j]
            nw = gdn_norm_w[j][None, :]
            if prompt:
                new_gdn, x, buf = _gdn_prompt(x, g, s_conv[j], w, gdn_conv_w[j], a_log, dt_bias, nw, gdn_w_out[j],
                                              batch, gdn_heads, gdn_dk, gdn_dv, n_gdn, j, new_gdn)
            else:
                x, new_gdn, buf = _gdn_sample(x, g, s_conv[j], w, gdn_conv_w[j], a_log, dt_bias, nw, gdn_w_out[j],
                                              s_gdn, j, new_gdn)
            new_conv.append(buf)
        x = _ffn(x, norm_g[i, 2][None, :], ffn_gu[i, 1], ffn_down[i, 1], fg, i == depth - 1)
    return x, new_ret, new_gdn, jnp.stack(new_conv)


def kernel(x_prompt, x_sample, state_ret, state_gdn, state_gdn_conv, norm_g, ffn_gu, ffn_down, ret_w_in,
           ret_w_out, gdn_w_in, gdn_conv_w, gdn_a_log, gdn_dt_bias, gdn_norm_w, gdn_w_out, final_g):
    bp, seq, d = x_prompt.shape
    bs, dec_seq, _ = x_sample.shape
    assert dec_seq == 1
    n_ret, _, ret_heads, ret_dk, ret_dv = state_ret.shape
    n_gdn, _, gdn_heads, gdn_dk, gdn_dv = state_gdn.shape

    ffn_gu_b = ffn_gu.astype(BF16)
    ffn_down_b = ffn_down.astype(BF16)
    ret_w = [_prep_ret_weights(ret_w_in[j], ret_heads, ret_dk) for j in range(n_ret)]
    ret_w_out_b = ret_w_out.astype(BF16)
    gdn_w = [_prep_gdn_weights(gdn_w_in[j], gdn_a_log[j], gdn_dt_bias[j], gdn_heads, gdn_dk, gdn_dv)
             for j in range(n_gdn)]
    gdn_w_out_b = gdn_w_out.astype(BF16)
    shared = (norm_g, ffn_gu_b, ffn_down_b, ret_w, ret_w_out_b, gdn_w, gdn_conv_w, gdn_norm_w, gdn_w_out_b, final_g)

    zero_ret = jax.ShapeDtypeStruct((n_ret, bp, ret_heads, ret_dk, ret_dv), F32)
    zero_gdn = jax.ShapeDtypeStruct((n_gdn, bp, gdn_heads, gdn_dk, gdn_dv), F32)
    zero_conv = jnp.zeros((n_gdn, bp) + state_gdn_conv.shape[2:], state_gdn_conv.dtype)
    y_p, ret_p, gdn_p, conv_p = _trunk(x_prompt.reshape(bp * seq, d), bp, 0.0, zero_ret, zero_gdn, zero_conv,
                                       True, *shared)
    y_s, ret_s, gdn_s, conv_s = _trunk(x_sample.reshape(bs, d), bs, SAMPLE_PAST_LEN, state_ret, state_gdn,
                                       state_gdn_conv, False, *shared)
    return (y_p.reshape(bp, seq, d), y_s.reshape(bs, dec_seq, d), ret_p, gdn_p, conv_p, ret_s, gdn_s, conv_s)
```

```python
import functools
import math

import jax
import jax.numpy as jnp
from jax import lax
from jax.experimental import pallas as pl
from jax.experimental.pallas import tpu as pltpu

F32 = jnp.float32
BF16 = jnp.bfloat16
EPS = 1e-6
ROPE_BASE = 10000.0
SAMPLE_PAST_LEN = 16384.0

LANES = 128
SUBLANES = 8
VMEM_LIMIT_BYTES = 56 * 1024 * 1024

FFN_ROWS = 512
RET_CHUNK = 256
GDN_CHUNK = 128
GDN_ROWS = 256
RET_STATE_BLOCK = 2
GDN_STATE_BLOCK = 8


def _dot(a, b):
    return jnp.dot(a, b, preferred_element_type=F32)


def _dot_nt(a, b):
    return lax.dot_general(a, b, (((1,), (1,)), ((), ())), preferred_element_type=F32)


def _rms(x, g):
    return x * lax.rsqrt(jnp.mean(x * x, axis=-1, keepdims=True) + EPS) * g


def _silu(x):
    return x * jax.nn.sigmoid(x)


def _softplus(x):
    return jnp.maximum(x, 0.0) + jnp.log1p(jnp.exp(-jnp.abs(x)))


def _bf16_terms(x):
    t0 = x.astype(BF16)
    r = x - t0.astype(F32)
    t1 = r.astype(BF16)
    t2 = (r - t1.astype(F32)).astype(BF16)
    return t0, t1, t2


def _expand_heads(x, first, heads, width):
    c = lax.broadcasted_iota(jnp.int32, (x.shape[1], heads * width), 0)
    lane = lax.broadcasted_iota(jnp.int32, (x.shape[1], heads * width), 1)
    e = ((lane >= (c - first) * width) & (lane < (c - first + 1) * width)).astype(BF16)
    return sum(_dot(term, e) for term in _bf16_terms(x))


def _resident(shape):
    nd = len(shape)
    return pl.BlockSpec(shape, lambda *_: (0,) * nd, pipeline_mode=pl.Buffered(1))


def _stacked_state(args, in_specs, prev):
    if prev is None:
        return args, in_specs, {}
    return args + [prev], in_specs + [pl.BlockSpec(memory_space=pl.ANY)], {len(args): 0}


def _params(*sem):
    return pltpu.CompilerParams(dimension_semantics=sem, vmem_limit_bytes=VMEM_LIMIT_BYTES)


def _ffn_kernel(x_ref, g_ref, wg_ref, wu_ref, wd_ref, fg_ref, o_ref, *, final):
    x = x_ref[...]
    h = _rms(x, g_ref[...]).astype(BF16)
    a = _dot(h, wg_ref[...])
    b = _dot(h, wu_ref[...])
    act = (_silu(a) * b).astype(BF16)
    y = x + 0.5 * _dot(act, wd_ref[...])
    if final:
        y = _rms(y, fg_ref[...])
    o_ref[...] = y


def _ffn(x, norm_g, w_gu, w_down, final_g, layer, which_norm, which_ffn, final):
    m, d = x.shape
    f = w_down.shape[2]
    tm = min(FFN_ROWS, m)
    assert m % tm == 0 and f % LANES == 0
    once = pl.Buffered(1)
    return pl.pallas_call(
        functools.partial(_ffn_kernel, final=final),
        name="ffn",
        grid=(m // tm,),
        in_specs=[
            pl.BlockSpec((tm, d), lambda i: (i, 0)),
            pl.BlockSpec((None, None, 1, d), lambda i: (layer, which_norm, 0, 0), pipeline_mode=once),
            pl.BlockSpec((None, None, d, f), lambda i: (layer, which_ffn, 0, 0), pipeline_mode=once),
            pl.BlockSpec((None, None, d, f), lambda i: (layer, which_ffn, 0, 1), pipeline_mode=once),
            pl.BlockSpec((None, None, f, d), lambda i: (layer, which_ffn, 0, 0), pipeline_mode=once),
            _resident((1, d)),
        ],
        out_specs=pl.BlockSpec((tm, d), lambda i: (i, 0)),
        out_shape=jax.ShapeDtypeStruct((m, d), F32),
        compiler_params=_params("arbitrary"),
    )(x, norm_g, w_gu, w_gu, w_down, final_g)


def _ret_prompt_kernel(*refs, heads, dk, dv, chunk):
    x_ref, g_ref, cos_ref, sin_ref, win_ref, wout_ref = refs[:6]
    sout_ref, y_ref, s_scr, og_scr = refs[-4:]
    t = pl.program_id(1)
    half = dk // 2

    @pl.when(t == 0)
    def _():
        s_scr[...] = jnp.zeros_like(s_scr)

    x = x_ref[...]
    hn = _rms(x, g_ref[...]).astype(BF16)
    cos = cos_ref[...]
    sin = sin_ref[...]
    ri = lax.broadcasted_iota(jnp.int32, (chunk, chunk), 0)
    ci = lax.broadcasted_iota(jnp.int32, (chunk, chunk), 1)
    lag = (ri - ci).astype(F32)
    row = lax.broadcasted_iota(jnp.int32, (chunk, half), 0).astype(F32)
    k_off, v_off, g_off = heads * dk, 2 * heads * dk, 2 * heads * dk + heads * dv
    for h in range(heads):
        log_gamma = math.log(1.0 - 2.0 ** (-5.0 - h))
        q = _dot(hn, win_ref[:, h * dk:(h + 1) * dk])
        k = _dot(hn, win_ref[:, k_off + h * dk:k_off + (h + 1) * dk])
        v = _dot(hn, win_ref[:, v_off + h * dv:v_off + (h + 1) * dv]).astype(BF16)
        gate = _dot(hn, win_ref[:, g_off + h * dv:g_off + (h + 1) * dv])
        q1, q2 = q[:, :half], q[:, half:]
        k1, k2 = k[:, :half], k[:, half:]
        qr1, qr2 = q1 * cos - q2 * sin, q2 * cos + q1 * sin
        kr1, kr2 = (k1 * cos - k2 * sin) * dk ** -0.5, (k2 * cos + k1 * sin) * dk ** -0.5
        qb = jnp.concatenate([qr1, qr2], axis=1).astype(BF16)
        kb = jnp.concatenate([kr1, kr2], axis=1).astype(BF16)
        decay = jnp.where(ri >= ci, jnp.exp(lag * log_gamma), 0.0)
        p = (_dot_nt(qb, kb) * decay).astype(BF16)
        q_scale = jnp.exp((row + 1.0) * log_gamma)
        k_scale = jnp.exp((chunk - 1.0 - row) * log_gamma)
        qd = jnp.concatenate([qr1 * q_scale, qr2 * q_scale], axis=1).astype(BF16)
        kd = jnp.concatenate([kr1 * k_scale, kr2 * k_scale], axis=1)
        s = s_scr[h]
        o = _dot(p, v) + _dot(qd, s.astype(BF16))
        s_scr[h] = math.exp(chunk * log_gamma) * s + _dot(kd.T.astype(BF16), v)
        o = o * lax.rsqrt(jnp.mean(o * o, axis=-1, keepdims=True) + EPS) * _silu(gate)
        og_scr[:, h * dv:(h + 1) * dv] = o.astype(BF16)
    y_ref[...] = x + _dot(og_scr[...], wout_ref[...])

    @pl.when(t == pl.num_programs(1) - 1)
    def _():
        ro = lax.broadcasted_iota(jnp.int32, (dk, dk), 0)
        rin = lax.broadcasted_iota(jnp.int32, (dk, dk), 1)
        perm = (rin == (ro >> 1) + (ro & 1) * half).astype(BF16)
        for h in range(heads):
            sout_ref[0, 0, h] = sum(_dot(perm, term) for term in _bf16_terms(s_scr[h]))


def _ret_prompt(x, g, cos, sin, w_in, w_out, batch, heads, dk, dv, layers, layer, prev):
    m, d = x.shape
    seq = m // batch
    chunk = min(RET_CHUNK, seq)
    assert seq % chunk == 0
    nt = seq // chunk
    args, in_specs, aliases = _stacked_state(
        [x, g, cos, sin, w_in, w_out],
        [pl.BlockSpec((chunk, d), lambda b, t: (b * nt + t, 0)),
         _resident((1, d)),
         pl.BlockSpec((chunk, dk // 2), lambda b, t: (t, 0)),
         pl.BlockSpec((chunk, dk // 2), lambda b, t: (t, 0)),
         _resident(w_in.shape),
         _resident(w_out.shape)],
        prev)
    return pl.pallas_call(
        functools.partial(_ret_prompt_kernel, heads=heads, dk=dk, dv=dv, chunk=chunk),
        name="ret_prompt",
        grid=(batch, nt),
        in_specs=in_specs,
        out_specs=[
            pl.BlockSpec((1, 1, heads, dk, dv), lambda b, t: (layer, b, 0, 0, 0)),
            pl.BlockSpec((chunk, d), lambda b, t: (b * nt + t, 0)),
        ],
        out_shape=[
            jax.ShapeDtypeStruct((layers, batch, heads, dk, dv), F32),
            jax.ShapeDtypeStruct((m, d), F32),
        ],
        input_output_aliases=aliases,
        scratch_shapes=[pltpu.VMEM((heads, dk, dv), F32), pltpu.VMEM((chunk, heads * dv), BF16)],
        compiler_params=_params("arbitrary", "arbitrary"),
    )(*args)


def _ret_sample_proj_kernel(x_ref, g_ref, cos_ref, sin_ref, win_ref, wsw_ref, q_ref, kt_ref, v_ref, gate_ref,
                            *, heads, dk, dv):
    qk_w = 2 * heads * dk
    hn = _rms(x_ref[...], g_ref[...]).astype(BF16)
    qk = _dot(hn, win_ref[:, :qk_w]) * cos_ref[...] + _dot(hn, wsw_ref[...]) * sin_ref[...]
    q_ref[...] = qk[:, :heads * dk]
    kt_ref[...] = (qk[:, heads * dk:] * dk ** -0.5).T
    v_ref[...] = _dot(hn, win_ref[:, qk_w:qk_w + heads * dv])
    gate_ref[...] = _dot(hn, win_ref[:, qk_w + heads * dv:])


def _ret_sample_state_kernel(*refs, heads, dk, dv, block):
    q_ref, kt_ref, v_ref, s0_ref = refs[:4]
    s_ref, o_ref = refs[-2:]
    n = v_ref.shape[0]
    lane = lax.broadcasted_iota(jnp.int32, (dk, n), 1)
    hs = range(heads)
    gammas = [1.0 - 2.0 ** (-5.0 - h) for h in hs]
    v_hs = [v_ref[:, h * dv:(h + 1) * dv].astype(BF16) for h in hs]
    for i in range(block):
        b = pl.program_id(0) * block + i
        q_row = q_ref[pl.ds(b, 1), :]
        kt_bs = [jnp.where(lane == b, kt_ref[h * dk:(h + 1) * dk, :], 0.0).astype(BF16) for h in hs]
        s_news = [gamma * s0_ref[0, i, h] + _dot(kt_b, v_h) for h, gamma, kt_b, v_h in zip(hs, gammas, kt_bs, v_hs)]
        q8s = [jnp.broadcast_to(q_row[:, h * dk:(h + 1) * dk], (SUBLANES, dk)).astype(BF16) for h in hs]
        outs = [_dot(q8, s_new.astype(BF16))[0:1] for q8, s_new in zip(q8s, s_news)]
        for h in hs:
            s_ref[0, i, h] = s_news[h]
        o_ref[0, i:i + 1, :] = jnp.concatenate(outs, axis=1)


def _mixer_out_kernel(x_ref, o_ref, gate_ref, nw_ref, wout_ref, y_ref, *, heads, dv):
    parts = []
    for h in range(heads):
        o = o_ref[:, h * dv:(h + 1) * dv]
        o = o * lax.rsqrt(jnp.mean(o * o, axis=-1, keepdims=True) + EPS) * nw_ref[...]
        parts.append((o * _silu(gate_ref[:, h * dv:(h + 1) * dv])).astype(BF16))
    y_ref[...] = x_ref[...] + _dot(jnp.concatenate(parts, axis=1), wout_ref[...])


def _mixer_out(x, o, gate, norm_w, w_out, heads, dv):
    return pl.pallas_call(
        functools.partial(_mixer_out_kernel, heads=heads, dv=dv),
        name="mixer_out",
        out_shape=jax.ShapeDtypeStruct(x.shape, F32),
        compiler_params=_params(),
    )(x, o, gate, norm_w, w_out)


def _ret_sample(x, g, cos, sin, w_in, w_sw, w_out, s0, ones_dv, layer, prev):
    n, d = x.shape
    _, _, heads, dk, dv = s0.shape
    block = math.gcd(RET_STATE_BLOCK, n)
    q, kt, v, gate = pl.pallas_call(
        functools.partial(_ret_sample_proj_kernel, heads=heads, dk=dk, dv=dv),
        name="ret_sample_proj",
        out_shape=[
            jax.ShapeDtypeStruct((n, heads * dk), F32),
            jax.ShapeDtypeStruct((heads * dk, n), F32),
            jax.ShapeDtypeStruct((n, heads * dv), F32),
            jax.ShapeDtypeStruct((n, heads * dv), F32),
        ],
        compiler_params=_params(),
    )(x, g, cos, sin, w_in, w_sw)
    state_spec = pl.BlockSpec((1, block, heads, dk, dv), lambda i: (layer, i, 0, 0, 0))
    args, in_specs, aliases = _stacked_state(
        [q, kt, v, s0], [_resident(q.shape), _resident(kt.shape), _resident(v.shape), state_spec], prev)
    s_new, o = pl.pallas_call(
        functools.partial(_ret_sample_state_kernel, heads=heads, dk=dk, dv=dv, block=block),
        name="ret_sample_state",
        grid=(n // block,),
        in_specs=in_specs,
        out_specs=[state_spec, pl.BlockSpec((1, block, heads * dv), lambda i: (i, 0, 0))],
        out_shape=[
            jax.ShapeDtypeStruct(s0.shape, F32),
            jax.ShapeDtypeStruct((n // block, block, heads * dv), F32),
        ],
        input_output_aliases=aliases,
        compiler_params=_params("arbitrary"),
    )(*args)
    y = _mixer_out(x, o.reshape(n, heads * dv), gate, ones_dv, w_out, heads, dv)
    return y, s_new


def _unit_lower_inverse_minus_identity(lms, ri, ci, chunk):
    def level_mask(lev):
        return ((ri >> lev) == (ci >> lev)) & ((ri >> (lev - 1)) != (ci >> (lev - 1)))

    mask = level_mask(1)
    ns = [jnp.where(mask, -lm, 0.0) for lm in lms]
    for lev in range(2, chunk.bit_length()):
        mask = level_mask(lev)
        cs = [jnp.where(mask, lm, 0.0) for lm in lms]
        gs = [c + _dot(c.astype(BF16), n.astype(BF16)) for c, n in zip(cs, ns)]
        ns = [n - g - _dot(n.astype(BF16), g.astype(BF16)) for n, g in zip(ns, gs)]
    return ns


def _gdn_prompt_kernel(*refs, heads, dk, dv, rows, chunk, taps):
    x_ref, g_ref, cbuf_ref, win_ref, cw_ref, alog_ref, dtb_ref, nw_ref, wout_ref = refs[:9]
    sout_ref, y_ref, cout_ref, s_scr, xp_scr, og_scr = refs[-6:]
    t = pl.program_id(1)
    qkv_w = heads * (2 * dk + dv)
    hd = heads * dv
    halo = taps - 1
    base = SUBLANES

    @pl.when(t == 0)
    def _():
        s_scr[...] = jnp.zeros_like(s_scr)
        xp_scr[base - halo:base, :] = cbuf_ref[0]

    x = x_ref[...]
    hn = _rms(x, g_ref[...]).astype(BF16)
    qkv = _dot(hn, win_ref[:, :qkv_w])
    xp_scr[base:base + rows, :] = qkv
    conv = qkv * cw_ref[halo:taps, :]
    for j in range(halo):
        conv = conv + xp_scr[base - halo + j:base - halo + j + rows, :] * cw_ref[j:j + 1, :]
    tail = xp_scr[base + rows - halo:base + rows, :]
    xp_scr[base - halo:base, :] = tail

    @pl.when(t == pl.num_programs(1) - 1)
    def _():
        cout_ref[0] = tail

    act = _silu(conv)
    z = _dot(hn, win_ref[:, qkv_w:qkv_w + hd])
    ba = _dot(hn, win_ref[:, qkv_w + hd:])
    a = -jnp.exp(alog_ref[...]) * _softplus(ba + dtb_ref[...])
    ri = lax.broadcasted_iota(jnp.int32, (chunk, chunk), 0)
    ci = lax.broadcasted_iota(jnp.int32, (chunk, chunk), 1)
    tril = (ri >= ci).astype(BF16)
    subs = range(rows // chunk)
    bcum = jnp.concatenate([sum(_dot(tril, term) for term in _bf16_terms(a[c * chunk:(c + 1) * chunk]))
                            for c in subs], axis=0)
    beta = _expand_heads(jax.nn.sigmoid(ba), 0, heads, dv)
    bcum = _expand_heads(bcum, heads, heads, dv)
    probs = [(c, h) for c in subs for h in range(heads)]

    def tile(arr, c, col, width):
        return arr[c * chunk:(c + 1) * chunk, col:col + width]

    qs = [tile(act, c, h * dk, dk) for c, h in probs]
    ks = [tile(act, c, heads * dk + h * dk, dk) for c, h in probs]
    vs = [tile(act, c, 2 * heads * dk + h * dv, dv) for c, h in probs]
    qs = [q * lax.rsqrt(jnp.sum(q * q, axis=-1, keepdims=True) + EPS) * dk ** -0.5 for q in qs]
    ks = [k * lax.rsqrt(jnp.sum(k * k, axis=-1, keepdims=True) + EPS) for k in ks]
    bts = [tile(beta, c, h * dv, dv) for c, h in probs]
    bs = [tile(bcum, c, h * dv, dv) for c, h in probs]
    es = [jnp.exp(jnp.minimum(b - b.T, 0.0)) for b in bs]
    kbs = [k.astype(BF16) for k in ks]
    qk_kks = [_dot_nt(jnp.concatenate([q.astype(BF16), kb], axis=0), kb) for q, kb in zip(qs, kbs)]
    ps = [jnp.where(ri >= ci, qk_kk[:chunk] * e, 0.0).astype(BF16) for qk_kk, e in zip(qk_kks, es)]
    lms = [jnp.where(ri > ci, qk_kk[chunk:] * e, 0.0) * bt for qk_kk, e, bt in zip(qk_kks, es, bts)]
    ns = [n.astype(BF16) for n in _unit_lower_inverse_minus_identity(lms, ri, ci, chunk)]
    kq_es = [jnp.concatenate([(k * jnp.exp(b)).astype(BF16), (q * jnp.exp(b)).astype(BF16)], axis=0)
             for k, q, b in zip(ks, qs, bs)]
    b_lasts = [b[chunk - 1:chunk, :] for b in bs]
    kdts = [(k * jnp.exp(b_last - b)).T.astype(BF16) for k, b, b_last in zip(ks, bs, b_lasts)]
    ss = [s_scr[h] for h in range(heads)]
    for c in subs:
        sel = slice(c * heads, (c + 1) * heads)
        kq_ss = [_dot(kq_e, s.astype(BF16)) for kq_e, s in zip(kq_es[sel], ss)]
        rs = [bt * (v - kq_s[:chunk]) for bt, v, kq_s in zip(bts[sel], vs[sel], kq_ss)]
        wbs = [(r + _dot(n, r.astype(BF16))).astype(BF16) for r, n in zip(rs, ns[sel])]
        os_ = [kq_s[chunk:] + _dot(p, wb) for kq_s, p, wb in zip(kq_ss, ps[sel], wbs)]
        ss = [jnp.exp(b_last) * s + _dot(kdt, wb) for b_last, s, kdt, wb in zip(b_lasts[sel], ss, kdts[sel], wbs)]
        for h in range(heads):
            o = os_[h]
            o = o * lax.rsqrt(jnp.mean(o * o, axis=-1, keepdims=True) + EPS) * nw_ref[...]
            og_scr[c * chunk:(c + 1) * chunk, h * dv:(h + 1) * dv] = (o * _silu(tile(z, c, h * dv, dv))).astype(BF16)
    for h in range(heads):
        s_scr[h] = ss[h]
    y_ref[...] = x + _dot(og_scr[...], wout_ref[...])

    @pl.when(t == pl.num_programs(1) - 1)
    def _():
        sout_ref[0, 0] = s_scr[...]


def _gdn_prompt(x, g, cbuf, w_in, conv_w, a_log, dt_bias, norm_w, w_out, batch, heads, dk, dv, layers, layer, prev):
    m, d = x.shape
    seq = m // batch
    chunk = GDN_CHUNK
    taps = conv_w.shape[0]
    qkv_w = heads * (2 * dk + dv)
    rows = math.gcd(GDN_ROWS, seq)
    assert rows % chunk == 0 and dk == chunk and dv == chunk and chunk == LANES and 2 * heads <= LANES
    nt = seq // rows
    args, in_specs, aliases = _stacked_state(
        [x, g, cbuf, w_in, conv_w, a_log, dt_bias, norm_w, w_out],
        [pl.BlockSpec((rows, d), lambda b, t: (b * nt + t, 0)),
         _resident((1, d)),
         pl.BlockSpec((1, taps - 1, qkv_w), lambda b, t: (b, 0, 0)),
         _resident(w_in.shape),
         _resident(conv_w.shape),
         _resident(a_log.shape),
         _resident(dt_bias.shape),
         _resident(norm_w.shape),
         _resident(w_out.shape)],
        prev)
    return pl.pallas_call(
        functools.partial(_gdn_prompt_kernel, heads=heads, dk=dk, dv=dv, rows=rows, chunk=chunk, taps=taps),
        name="gdn_prompt",
        grid=(batch, nt),
        in_specs=in_specs,
        out_specs=[
            pl.BlockSpec((1, 1, heads, dk, dv), lambda b, t: (layer, b, 0, 0, 0)),
            pl.BlockSpec((rows, d), lambda b, t: (b * nt + t, 0)),
            pl.BlockSpec((1, taps - 1, qkv_w), lambda b, t: (b, 0, 0)),
        ],
        out_shape=[
            jax.ShapeDtypeStruct((layers, batch, heads, dk, dv), F32),
            jax.ShapeDtypeStruct((m, d), F32),
            jax.ShapeDtypeStruct((batch, taps - 1, qkv_w), F32),
        ],
        input_output_aliases=aliases,
        scratch_shapes=[
            pltpu.VMEM((heads, dk, dv), F32),
            pltpu.VMEM((SUBLANES + rows, qkv_w), F32),
            pltpu.VMEM((rows, heads * dv), BF16),
        ],
        compiler_params=_params("arbitrary", "arbitrary"),
    )(*args)


def _gdn_sample_proj_kernel(x_ref, g_ref, cbuf_ref, win_ref, cw_ref, alog_ref, dtb_ref,
                            q_ref, kt_ref, k_ref, v_ref, z_ref, beta_ref, ea_ref, cout_ref, *, heads, dk, dv, taps):
    qkv_w = heads * (2 * dk + dv)
    hd = heads * dv
    halo = taps - 1
    hn = _rms(x_ref[...], g_ref[...]).astype(BF16)
    qkv = _dot(hn, win_ref[:, :qkv_w])
    conv = qkv * cw_ref[halo:taps, :]
    for j in range(halo):
        conv = conv + cbuf_ref[j] * cw_ref[j:j + 1, :]
        if j > 0:
            cout_ref[j - 1] = cbuf_ref[j]
    cout_ref[halo - 1] = qkv
    act = _silu(conv)
    for h in range(heads):
        q = act[:, h * dk:(h + 1) * dk]
        k = act[:, heads * dk + h * dk:heads * dk + (h + 1) * dk]
        q_ref[:, h * dk:(h + 1) * dk] = q * lax.rsqrt(jnp.sum(q * q, axis=-1, keepdims=True) + EPS) * dk ** -0.5
        k_ref[:, h * dk:(h + 1) * dk] = k * lax.rsqrt(jnp.sum(k * k, axis=-1, keepdims=True) + EPS)
    kt_ref[...] = k_ref[...].T
    v_ref[...] = act[:, 2 * heads * dk:]
    z_ref[...] = _dot(hn, win_ref[:, qkv_w:qkv_w + hd])
    ba = _dot(hn, win_ref[:, qkv_w + hd:])
    beta_ref[...] = _expand_heads(jax.nn.sigmoid(ba), 0, heads, dv)
    a = -jnp.exp(alog_ref[...]) * _softplus(ba + dtb_ref[...])
    ea_ref[...] = jnp.exp(_expand_heads(a, heads, heads, dv))


def _gdn_sample_state_kernel(*refs, heads, dk, dv, block):
    q_ref, kt_ref, k_ref, v_ref, beta_ref, ea_ref, s0_ref = refs[:7]
    s_ref, o_ref = refs[-2:]
    n = v_ref.shape[0]
    lane = lax.broadcasted_iota(jnp.int32, (dk, n), 1)
    for i in range(block):
        b = pl.program_id(0) * block + i
        q_row, k_row, v_row = q_ref[pl.ds(b, 1), :], k_ref[pl.ds(b, 1), :], v_ref[pl.ds(b, 1), :]
        beta_row, ea_row = beta_ref[pl.ds(b, 1), :], ea_ref[pl.ds(b, 1), :]
        hs = range(heads)
        s0s = [s0_ref[0, i, h] for h in hs]
        eas = [ea_row[:, h * dv:(h + 1) * dv] for h in hs]
        k8s = [jnp.broadcast_to(k_row[:, h * dk:(h + 1) * dk] * ea, (SUBLANES, dk)).astype(BF16)
               for h, ea in zip(hs, eas)]
        kss = [_dot(k8, s0.astype(BF16))[0:1] for k8, s0 in zip(k8s, s0s)]
        ws = [beta_row[:, h * dv:(h + 1) * dv] * (v_row[:, h * dv:(h + 1) * dv] - ks) for h, ks in zip(hs, kss)]
        kt_bs = [jnp.where(lane == b, kt_ref[h * dk:(h + 1) * dk, :], 0.0).astype(BF16) for h in hs]
        s_news = [ea * s0 + _dot(kt_b, jnp.broadcast_to(w, (n, dv)).astype(BF16))
                  for ea, s0, kt_b, w in zip(eas, s0s, kt_bs, ws)]
        q8s = [jnp.broadcast_to(q_row[:, h * dk:(h + 1) * dk], (SUBLANES, dk)).astype(BF16) for h in hs]
        outs = [_dot(q8, s_new.astype(BF16))[0:1] for q8, s_new in zip(q8s, s_news)]
        for h in hs:
            s_ref[0, i, h] = s_news[h]
        o_ref[i:i + 1, :] = jnp.concatenate(outs, axis=1)


def _gdn_sample(x, g, cbuf, w_in, conv_w, a_log, dt_bias, norm_w, w_out, s0, layer, prev):
    n, d = x.shape
    _, _, heads, dk, dv = s0.shape
    taps = conv_w.shape[0]
    qkv_w = heads * (2 * dk + dv)
    hd = heads * dv
    block = math.gcd(GDN_STATE_BLOCK, n)
    cbuf_t = jnp.swapaxes(cbuf, 0, 1)
    q, kt, k, v, z, beta, ea, cout_t = pl.pallas_call(
        functools.partial(_gdn_sample_proj_kernel, heads=heads, dk=dk, dv=dv, taps=taps),
        name="gdn_sample_proj",
        out_shape=[
            jax.ShapeDtypeStruct((n, heads * dk), F32),
            jax.ShapeDtypeStruct((heads * dk, n), F32),
            jax.ShapeDtypeStruct((n, heads * dk), F32),
            jax.ShapeDtypeStruct((n, hd), F32),
            jax.ShapeDtypeStruct((n, hd), F32),
            jax.ShapeDtypeStruct((n, hd), F32),
            jax.ShapeDtypeStruct((n, hd), F32),
            jax.ShapeDtypeStruct((taps - 1, n, qkv_w), F32),
        ],
        compiler_params=_params(),
    )(x, g, cbuf_t, w_in, conv_w, a_log, dt_bias)
    state_spec = pl.BlockSpec((1, block, heads, dk, dv), lambda i: (layer, i, 0, 0, 0))
    args, in_specs, aliases = _stacked_state(
        [q, kt, k, v, beta, ea, s0],
        [_resident(q.shape), _resident(kt.shape), _resident(k.shape), _resident(v.shape),
         _resident(beta.shape), _resident(ea.shape), state_spec],
        prev)
    s_new, o = pl.pallas_call(
        functools.partial(_gdn_sample_state_kernel, heads=heads, dk=dk, dv=dv, block=block),
        name="gdn_sample_state",
        grid=(n // block,),
        in_specs=in_specs,
        out_specs=[state_spec, pl.BlockSpec((block, hd), lambda i: (i, 0))],
        out_shape=[jax.ShapeDtypeStruct(s0.shape, F32), jax.ShapeDtypeStruct((n, hd), F32)],
        input_output_aliases=aliases,
        compiler_params=_params("arbitrary"),
    )(*args)
    y = _mixer_out(x, o, z, norm_w, w_out, heads, dv)
    return y, s_new, jnp.swapaxes(cout_t, 0, 1)


def _rope_tables(pos, dk):
    theta = 1.0 / (ROPE_BASE ** jnp.linspace(0.0, 1.0, dk // 2, dtype=F32))
    ang = pos[:, None] * theta[None, :]
    return jnp.cos(ang), jnp.sin(ang)


def _prep_ret_weights(w_in, heads, dk):
    qk_w = 2 * heads * dk
    w_qk = w_in[:, :qk_w]
    col = jnp.arange(qk_w)
    within = col % dk
    deint = (col - within) + jnp.where(within < dk // 2, 2 * within, 2 * (within - dk // 2) + 1)
    w_deint = jnp.concatenate([w_qk[:, deint], w_in[:, qk_w:]], axis=1).astype(BF16)
    swapped = w_qk[:, col ^ 1] * jnp.where(col % 2 == 0, -1.0, 1.0)[None, :]
    return w_in.astype(BF16), w_deint, swapped.astype(BF16)


def _prep_gdn_weights(w_in, a_log, dt_bias, heads, dk, dv):
    pad = LANES - 2 * heads
    w = jnp.pad(w_in, ((0, 0), (0, pad))).astype(BF16)
    place = lambda p: jnp.pad(p, (heads, pad))[None, :]
    return w, place(a_log), place(dt_bias)


def _trunk(x, batch, pos0, s_ret, s_gdn, s_conv, prompt, norm_g, ffn_gu, ffn_down, ret_w, ret_w_out, gdn_w,
           gdn_conv_w, gdn_norm_w, gdn_w_out, final_g):
    depth = norm_g.shape[0]
    n_mixers = 2
    m, d = x.shape
    seq = m // batch
    n_ret, _, ret_heads, ret_dk, ret_dv = s_ret.shape
    n_gdn, _, gdn_heads, gdn_dk, gdn_dv = s_gdn.shape
    cos, sin = _rope_tables(jnp.arange(seq, dtype=F32) + pos0, ret_dk)
    if not prompt:
        cos = jnp.tile(jnp.repeat(cos, 2, axis=1), (1, 2 * ret_heads))
        sin = jnp.tile(jnp.repeat(sin, 2, axis=1), (1, 2 * ret_heads))
    ones_dv = jnp.ones((1, ret_dv), F32)
    fg = final_g[None, :]
    norm_g4 = norm_g[:, :, None, :]
    new_ret, new_gdn, new_conv = None, None, []
    for i in range(depth):
        x = _ffn(x, norm_g4, ffn_gu, ffn_down, fg, i, 0, 0, False)
        j = i // n_mixers
        g = norm_g[i, 1][None, :]
        if i % n_mixers == 0:
            w_orig, w_deint, w_swap = ret_w[j]
            if prompt:
                new_ret, x = _ret_prompt(x, g, cos, sin, w_deint, ret_w_out[j], batch, ret_heads, ret_dk, ret_dv,
                                         n_ret, j, new_ret)
            else:
                x, new_ret = _ret_sample(x, g, cos, sin, w_orig, w_swap, ret_w_out[j], s_ret, ones_dv, j, new_ret)
        else:
            w, a_log, dt_bias = gdn_w[j]
            nw = gdn_norm_w[j][None, :]
            if prompt:
                new_gdn, x, buf = _gdn_prompt(x, g, s_conv[j], w, gdn_conv_w[j], a_log, dt_bias, nw, gdn_w_out[j],
                                              batch, gdn_heads, gdn_dk, gdn_dv, n_gdn, j, new_gdn)
            else:
                x, new_gdn, buf = _gdn_sample(x, g, s_conv[j], w, gdn_conv_w[j], a_log, dt_bias, nw, gdn_w_out[j],
                                              s_gdn, j, new_gdn)
            new_conv.append(buf)
        x = _ffn(x, norm_g4, ffn_gu, ffn_down, fg, i, 2, 1, i == depth - 1)
    return x, new_ret, new_gdn, jnp.stack(new_conv)


def kernel(x_prompt, x_sample, state_ret, state_gdn, state_gdn_conv, norm_g, ffn_gu, ffn_down, ret_w_in,
           ret_w_out, gdn_w_in, gdn_conv_w, gdn_a_log, gdn_dt_bias, gdn_norm_w, gdn_w_out, final_g):
    bp, seq, d = x_prompt.shape
    bs, dec_seq, _ = x_sample.shape
    assert dec_seq == 1
    n_ret, _, ret_heads, ret_dk, ret_dv = state_ret.shape
    n_gdn, _, gdn_heads, gdn_dk, gdn_dv = state_gdn.shape

    ffn_gu_b = ffn_gu.astype(BF16)
    ffn_down_b = ffn_down.astype(BF16)
    ret_w = [_prep_ret_weights(ret_w_in[j], ret_heads, ret_dk) for j in range(n_ret)]
    ret_w_out_b = ret_w_out.astype(BF16)
    gdn_w = [_prep_gdn_weights(gdn_w_in[j], gdn_a_log[j], gdn_dt_bias[j], gdn_heads, gdn_dk, gdn_dv)
             for j in range(n_gdn)]
    gdn_w_out_b = gdn_w_out.astype(BF16)
    shared = (norm_g, ffn_gu_b, ffn_down_b, ret_w, ret_w_out_b, gdn_w, gdn_conv_w, gdn_norm_w, gdn_w_out_b, final_g)

    zero_ret = jax.ShapeDtypeStruct((n_ret, bp, ret_heads, ret_dk, ret_dv), F32)
    zero_gdn = jax.ShapeDtypeStruct((n_gdn, bp, gdn_heads, gdn_dk, gdn_dv), F32)
    zero_conv = jnp.zeros((n_gdn, bp) + state_gdn_conv.shape[2:], state_gdn_conv.dtype)
    y_p, ret_p, gdn_p, conv_p = _trunk(x_prompt.reshape(bp * seq, d), bp, 0.0, zero_ret, zero_gdn, zero_conv,
                                       True, *shared)
    y_s, ret_s, gdn_s, conv_s = _trunk(x_sample.reshape(bs, d), bs, SAMPLE_PAST_LEN, state_ret, state_gdn,
                                       state_gdn_conv, False, *shared)
    return (y_p.reshape(bp, seq, d), y_s.reshape(bs, dec_seq, d), ret_p, gdn_p, conv_p, ret_s, gdn_s, conv_s)
```

```python
import functools
import math

import jax
import jax.numpy as jnp
from jax import lax
from jax.experimental import pallas as pl
from jax.experimental.pallas import tpu as pltpu

F32 = jnp.float32
BF16 = jnp.bfloat16
EPS = 1e-6
ROPE_BASE = 10000.0
SAMPLE_PAST_LEN = 16384.0

LANES = 128
SUBLANES = 8
VMEM_LIMIT_BYTES = 56 * 1024 * 1024

FFN_ROWS = 512
RET_CHUNK = 256
GDN_CHUNK = 128
GDN_ROWS = 256
RET_STATE_BLOCK = 2
GDN_STATE_BLOCK = 8


def _dot(a, b):
    return jnp.dot(a, b, preferred_element_type=F32)


def _dot_nt(a, b):
    return lax.dot_general(a, b, (((1,), (1,)), ((), ())), preferred_element_type=F32)


def _rms(x, g):
    return x * lax.rsqrt(jnp.mean(x * x, axis=-1, keepdims=True) + EPS) * g


def _silu(x):
    h = 0.5 * x
    return h * jnp.tanh(h) + h


def _softplus(x):
    return jnp.maximum(x, 0.0) + jnp.log1p(jnp.exp(-jnp.abs(x)))


def _bf16_terms(x):
    t0 = x.astype(BF16)
    r = x - t0.astype(F32)
    t1 = r.astype(BF16)
    t2 = (r - t1.astype(F32)).astype(BF16)
    return t0, t1, t2


def _exact_dot_right(x, sel):
    return _dot(jnp.concatenate(_bf16_terms(x), axis=1), jnp.concatenate([sel] * 3, axis=0))


def _exact_dot_left(sel, x):
    return _dot(jnp.concatenate([sel] * 3, axis=1), jnp.concatenate(_bf16_terms(x), axis=0))


def _expand_heads(x, first, heads, width):
    c = lax.broadcasted_iota(jnp.int32, (x.shape[1], heads * width), 0)
    lane = lax.broadcasted_iota(jnp.int32, (x.shape[1], heads * width), 1)
    e = ((lane >= (c - first) * width) & (lane < (c - first + 1) * width)).astype(BF16)
    return _exact_dot_right(x, e)


def _dots(lhs, rhs, nt=False):
    dot = _dot_nt if nt else _dot
    return [dot(a, b) for a, b in zip(lhs, rhs)]


def _resident(shape):
    nd = len(shape)
    return pl.BlockSpec(shape, lambda *_: (0,) * nd, pipeline_mode=pl.Buffered(1))


def _params(*sem):
    return pltpu.CompilerParams(dimension_semantics=sem, vmem_limit_bytes=VMEM_LIMIT_BYTES)


def _ffn_kernel(x_ref, g_ref, wg_ref, wu_ref, wd_ref, fg_ref, o_ref, *, final):
    x = x_ref[...]
    h = _rms(x, g_ref[...]).astype(BF16)
    a = _dot(h, wg_ref[...])
    b = _dot(h, wu_ref[...])
    act = (_silu(a) * b).astype(BF16)
    y = x + 0.5 * _dot(act, wd_ref[...])
    if final:
        y = _rms(y, fg_ref[...])
    o_ref[...] = y


def _ffn(x, norm_g, w_gu, w_down, final_g, layer, which_norm, which_ffn, final):
    m, d = x.shape
    f = w_down.shape[2]
    tm = min(FFN_ROWS, m)
    assert m % tm == 0 and f % LANES == 0
    once = pl.Buffered(1)
    return pl.pallas_call(
        functools.partial(_ffn_kernel, final=final),
        name="ffn",
        grid=(m // tm,),
        in_specs=[
            pl.BlockSpec((tm, d), lambda i: (i, 0)),
            pl.BlockSpec((None, None, 1, d), lambda i: (layer, which_norm, 0, 0), pipeline_mode=once),
            pl.BlockSpec((None, None, d, f), lambda i: (layer, which_ffn, 0, 0), pipeline_mode=once),
            pl.BlockSpec((None, None, d, f), lambda i: (layer, which_ffn, 0, 1), pipeline_mode=once),
            pl.BlockSpec((None, None, f, d), lambda i: (layer, which_ffn, 0, 0), pipeline_mode=once),
            _resident((1, d)),
        ],
        out_specs=pl.BlockSpec((tm, d), lambda i: (i, 0)),
        out_shape=jax.ShapeDtypeStruct((m, d), F32),
        compiler_params=_params("arbitrary"),
    )(x, norm_g, w_gu, w_gu, w_down, final_g)


def _ret_prompt_kernel(x_ref, g_ref, cos_ref, sin_ref, win_ref, wout_ref, y_ref, sout_ref, s_scr, og_scr,
                       *, heads, dk, dv, chunk):
    t = pl.program_id(1)
    half = dk // 2

    @pl.when(t == 0)
    def _():
        s_scr[...] = jnp.zeros_like(s_scr)

    x = x_ref[...]
    hn = _rms(x, g_ref[...]).astype(BF16)
    cos = cos_ref[...]
    sin = sin_ref[...]
    ri = lax.broadcasted_iota(jnp.int32, (chunk, chunk), 0)
    ci = lax.broadcasted_iota(jnp.int32, (chunk, chunk), 1)
    lag = (ri - ci).astype(F32)
    row = lax.broadcasted_iota(jnp.int32, (chunk, half), 0).astype(F32)
    k_off, v_off, g_off = heads * dk, 2 * heads * dk, 2 * heads * dk + heads * dv
    for h in range(heads):
        log_gamma = math.log(1.0 - 2.0 ** (-5.0 - h))
        q = _dot(hn, win_ref[:, h * dk:(h + 1) * dk])
        k = _dot(hn, win_ref[:, k_off + h * dk:k_off + (h + 1) * dk])
        v = _dot(hn, win_ref[:, v_off + h * dv:v_off + (h + 1) * dv]).astype(BF16)
        gate = _dot(hn, win_ref[:, g_off + h * dv:g_off + (h + 1) * dv])
        q1, q2 = q[:, :half], q[:, half:]
        k1, k2 = k[:, :half], k[:, half:]
        qr1, qr2 = q1 * cos - q2 * sin, q2 * cos + q1 * sin
        kr1, kr2 = (k1 * cos - k2 * sin) * dk ** -0.5, (k2 * cos + k1 * sin) * dk ** -0.5
        qb = jnp.concatenate([qr1, qr2], axis=1).astype(BF16)
        kb = jnp.concatenate([kr1, kr2], axis=1).astype(BF16)
        decay = jnp.where(ri >= ci, jnp.exp(lag * log_gamma), 0.0)
        p = (_dot_nt(qb, kb) * decay).astype(BF16)
        q_scale = jnp.exp((row + 1.0) * log_gamma)
        k_scale = jnp.exp((chunk - 1.0 - row) * log_gamma)
        qd = jnp.concatenate([qr1 * q_scale, qr2 * q_scale], axis=1).astype(BF16)
        kd = jnp.concatenate([kr1 * k_scale, kr2 * k_scale], axis=1)
        s = s_scr[h]
        o = _dot(p, v) + _dot(qd, s.astype(BF16))
        s_scr[h] = math.exp(chunk * log_gamma) * s + _dot(kd.T.astype(BF16), v)
        o = o * lax.rsqrt(jnp.mean(o * o, axis=-1, keepdims=True) + EPS) * _silu(gate)
        og_scr[:, h * dv:(h + 1) * dv] = o.astype(BF16)
    y_ref[...] = x + _dot(og_scr[...], wout_ref[...])

    @pl.when(t == pl.num_programs(1) - 1)
    def _():
        ro = lax.broadcasted_iota(jnp.int32, (dk, dk), 0)
        rin = lax.broadcasted_iota(jnp.int32, (dk, dk), 1)
        perm = (rin == (ro >> 1) + (ro & 1) * half).astype(BF16)
        for h in range(heads):
            sout_ref[0, h] = _exact_dot_left(perm, s_scr[h])


def _ret_prompt(x, g, cos, sin, w_in, w_out, batch, heads, dk, dv):
    m, d = x.shape
    seq = m // batch
    chunk = min(RET_CHUNK, seq)
    assert seq % chunk == 0
    nt = seq // chunk
    return pl.pallas_call(
        functools.partial(_ret_prompt_kernel, heads=heads, dk=dk, dv=dv, chunk=chunk),
        name="ret_prompt",
        grid=(batch, nt),
        in_specs=[
            pl.BlockSpec((chunk, d), lambda b, t: (b * nt + t, 0)),
            _resident((1, d)),
            pl.BlockSpec((chunk, dk // 2), lambda b, t: (t, 0)),
            pl.BlockSpec((chunk, dk // 2), lambda b, t: (t, 0)),
            _resident(w_in.shape),
            _resident(w_out.shape),
        ],
        out_specs=[
            pl.BlockSpec((chunk, d), lambda b, t: (b * nt + t, 0)),
            pl.BlockSpec((1, heads, dk, dv), lambda b, t: (b, 0, 0, 0)),
        ],
        out_shape=[
            jax.ShapeDtypeStruct((m, d), F32),
            jax.ShapeDtypeStruct((batch, heads, dk, dv), F32),
        ],
        scratch_shapes=[pltpu.VMEM((heads, dk, dv), F32), pltpu.VMEM((chunk, heads * dv), BF16)],
        compiler_params=_params("arbitrary", "arbitrary"),
    )(x, g, cos, sin, w_in, w_out)


def _ret_sample_proj_kernel(x_ref, g_ref, cos_ref, sin_ref, win_ref, wsw_ref, q_ref, kt_ref, v_ref, gate_ref,
                            *, heads, dk, dv):
    qk_w = 2 * heads * dk
    hn = _rms(x_ref[...], g_ref[...]).astype(BF16)
    qk = _dot(hn, win_ref[:, :qk_w]) * cos_ref[...] + _dot(hn, wsw_ref[...]) * sin_ref[...]
    q_ref[...] = qk[:, :heads * dk]
    kt_ref[...] = (qk[:, heads * dk:] * dk ** -0.5).T
    v_ref[...] = _dot(hn, win_ref[:, qk_w:qk_w + heads * dv])
    gate_ref[...] = _dot(hn, win_ref[:, qk_w + heads * dv:])


def _ret_sample_state_kernel(q_ref, *refs, heads, dk, dv, block, n_layers, write_state):
    ins, outs = refs[:3 * n_layers], refs[3 * n_layers:]
    o_ref = outs[-1]
    n = q_ref.shape[0]
    lane = lax.broadcasted_iota(jnp.int32, (dk, n), 1)
    hs = range(heads)
    gammas = [1.0 - 2.0 ** (-5.0 - h) for h in hs]
    for i in range(block):
        b = pl.program_id(0) * block + i
        for layer in range(n_layers):
            kt_ref, v_ref, s0_ref = ins[3 * layer:3 * layer + 3]
            kt_bs = [jnp.where(lane == b, kt_ref[h * dk:(h + 1) * dk, :], 0.0).astype(BF16) for h in hs]
            s_news = [gamma * s0_ref[0, i, h] + _dot(kt_b, v_ref[:, h * dv:(h + 1) * dv].astype(BF16))
                      for h, gamma, kt_b in zip(hs, gammas, kt_bs)]
            if write_state:
                for h in hs:
                    outs[0][layer, i, h] = s_news[h]
        q_row = q_ref[pl.ds(b, 1), :]
        q8s = [jnp.broadcast_to(q_row[:, h * dk:(h + 1) * dk], (SUBLANES, dk)).astype(BF16) for h in hs]
        o_ref[0, i:i + 1, :] = jnp.concatenate(
            [_dot(q8, s_new.astype(BF16))[0:1] for q8, s_new in zip(q8s, s_news)], axis=1)


def _mixer_out_kernel(x_ref, o_ref, gate_ref, nw_ref, wout_ref, y_ref, *, heads, dv):
    parts = []
    for h in range(heads):
        o = o_ref[:, h * dv:(h + 1) * dv]
        o = o * lax.rsqrt(jnp.mean(o * o, axis=-1, keepdims=True) + EPS) * nw_ref[...]
        parts.append((o * _silu(gate_ref[:, h * dv:(h + 1) * dv])).astype(BF16))
    y_ref[...] = x_ref[...] + _dot(jnp.concatenate(parts, axis=1), wout_ref[...])


def _mixer_out(x, o, gate, norm_w, w_out, heads, dv):
    return pl.pallas_call(
        functools.partial(_mixer_out_kernel, heads=heads, dv=dv),
        name="mixer_out",
        out_shape=jax.ShapeDtypeStruct(x.shape, F32),
        compiler_params=_params(),
    )(x, o, gate, norm_w, w_out)


def _ret_sample(x, g, cos, sin, w_in, w_sw, w_out, s0, ones_dv, layer, earlier):
    n, d = x.shape
    layers, _, heads, dk, dv = s0.shape
    block = math.gcd(RET_STATE_BLOCK, n)
    q, kt, v, gate = pl.pallas_call(
        functools.partial(_ret_sample_proj_kernel, heads=heads, dk=dk, dv=dv),
        name="ret_sample_proj",
        out_shape=[
            jax.ShapeDtypeStruct((n, heads * dk), F32),
            jax.ShapeDtypeStruct((heads * dk, n), F32),
            jax.ShapeDtypeStruct((n, heads * dv), F32),
            jax.ShapeDtypeStruct((n, heads * dv), F32),
        ],
        compiler_params=_params(),
    )(x, g, cos, sin, w_in, w_sw)
    last = layer == layers - 1
    handled = list(enumerate(earlier)) + [(layer, (kt, v))] if last else [(layer, (kt, v))]
    args, in_specs = [q], [_resident(q.shape)]
    for idx, (kt_l, v_l) in handled:
        args += [kt_l, v_l, s0]
        in_specs += [_resident(kt_l.shape), _resident(v_l.shape),
                     pl.BlockSpec((1, block, heads, dk, dv), lambda i, idx=idx: (idx, i, 0, 0, 0))]
    o_spec = pl.BlockSpec((1, block, heads * dv), lambda i: (i, 0, 0))
    o_shape = jax.ShapeDtypeStruct((n // block, block, heads * dv), F32)
    outs = pl.pallas_call(
        functools.partial(_ret_sample_state_kernel, heads=heads, dk=dk, dv=dv, block=block,
                          n_layers=len(handled), write_state=last),
        name="ret_sample_state",
        grid=(n // block,),
        in_specs=in_specs,
        out_specs=[pl.BlockSpec((layers, block, heads, dk, dv), lambda i: (0, i, 0, 0, 0)), o_spec] if last
        else [o_spec],
        out_shape=[jax.ShapeDtypeStruct(s0.shape, F32), o_shape] if last else [o_shape],
        compiler_params=_params("arbitrary"),
    )(*args)
    y = _mixer_out(x, outs[-1].reshape(n, heads * dv), gate, ones_dv, w_out, heads, dv)
    return y, (outs[0] if last else None), earlier + [(kt, v)]


def _unit_lower_inverse_minus_identity(lms, ri, ci, chunk):
    def level_mask(lev):
        return ((ri >> lev) == (ci >> lev)) & ((ri >> (lev - 1)) != (ci >> (lev - 1)))

    mask = level_mask(1)
    ns = [jnp.where(mask, -lm, 0.0) for lm in lms]
    for lev in range(2, chunk.bit_length()):
        mask = level_mask(lev)
        cs = [jnp.where(mask, lm, 0.0) for lm in lms]
        nbs = [n.astype(BF16) for n in ns]
        gs = [c + cn for c, cn in zip(cs, _dots([c.astype(BF16) for c in cs], nbs))]
        ns = [n - g - ng for n, g, ng in zip(ns, gs, _dots(nbs, [g.astype(BF16) for g in gs]))]
    return ns


def _gdn_prompt_kernel(x_ref, g_ref, cbuf_ref, win_ref, cw_ref, alog_ref, dtb_ref, nw_ref, wout_ref,
                       y_ref, sout_ref, cout_ref, s_scr, xh_scr, uh_scr, og_scr, *, heads, dk, dv, rows, chunk):
    t = pl.program_id(1)
    qkv_w = heads * (2 * dk + dv)
    hd = heads * dv
    cw = [cw_ref[j:j + 1, :] for j in range(4)]

    @pl.when(t == 0)
    def _():
        s_scr[...] = jnp.zeros_like(s_scr)
        xh_scr[...] = jnp.zeros_like(xh_scr)
        uh_scr[...] = jnp.zeros_like(uh_scr)
        xh_scr[0:1, :] = cbuf_ref[0, 2:3, :]
        uh_scr[0:1, :] = cw[1] * cbuf_ref[0, 1:2, :] + cw[0] * cbuf_ref[0, 0:1, :]
        uh_scr[1:2, :] = cw[1] * cbuf_ref[0, 2:3, :] + cw[0] * cbuf_ref[0, 1:2, :]

    x = x_ref[...]
    hn = _rms(x, g_ref[...]).astype(BF16)
    qkv = _dot(hn, win_ref[:, :qkv_w])
    sub = lax.broadcasted_iota(jnp.int32, (SUBLANES, qkv_w), 0)

    def shifted(cur, by, halo_tile):
        rolled = pltpu.roll(cur, by, 0)
        first = jnp.where(sub < by, halo_tile, rolled[:SUBLANES])
        return jnp.concatenate([first, rolled[SUBLANES:]], axis=0)

    x_prev = shifted(qkv, 1, xh_scr[...])
    u = cw[1] * qkv + cw[0] * x_prev
    conv = cw[3] * qkv + cw[2] * x_prev + shifted(u, 2, uh_scr[...])
    xh_scr[0:1, :] = qkv[rows - 1:rows, :]
    uh_scr[0:2, :] = u[rows - 2:rows, :]

    @pl.when(t == pl.num_programs(1) - 1)
    def _():
        cout_ref[0] = qkv[rows - 3:rows, :]

    act = _silu(conv)
    z = _dot(hn, win_ref[:, qkv_w:qkv_w + hd])
    ba = _dot(hn, win_ref[:, qkv_w + hd:])
    a = -jnp.exp(alog_ref[...]) * _softplus(ba + dtb_ref[...])
    ri = lax.broadcasted_iota(jnp.int32, (chunk, chunk), 0)
    ci = lax.broadcasted_iota(jnp.int32, (chunk, chunk), 1)
    tril = (ri >= ci).astype(BF16)
    subs = range(rows // chunk)
    bcum = jnp.concatenate([_exact_dot_left(tril, a[c * chunk:(c + 1) * chunk]) for c in subs], axis=0)
    beta = _expand_heads(jax.nn.sigmoid(ba), 0, heads, dv)
    bcum = _expand_heads(bcum, heads, heads, dv)
    probs = [(c, h) for c in subs for h in range(heads)]

    def tile(arr, c, col, width):
        return arr[c * chunk:(c + 1) * chunk, col:col + width]

    qs = [tile(act, c, h * dk, dk) for c, h in probs]
    ks = [tile(act, c, heads * dk + h * dk, dk) for c, h in probs]
    vs = [tile(act, c, 2 * heads * dk + h * dv, dv) for c, h in probs]
    qs = [q * lax.rsqrt(jnp.sum(q * q, axis=-1, keepdims=True) + EPS) * dk ** -0.5 for q in qs]
    ks = [k * lax.rsqrt(jnp.sum(k * k, axis=-1, keepdims=True) + EPS) for k in ks]
    bts = [tile(beta, c, h * dv, dv) for c, h in probs]
    bs = [tile(bcum, c, h * dv, dv) for c, h in probs]
    es = [jnp.exp(jnp.minimum(b - b.T, 0.0)) for b in bs]
    kbs = [k.astype(BF16) for k in ks]
    qk_kks = _dots([jnp.concatenate([q.astype(BF16), kb], axis=0) for q, kb in zip(qs, kbs)], kbs, nt=True)
    ps = [jnp.where(ri >= ci, qk_kk[:chunk] * e, 0.0).astype(BF16) for qk_kk, e in zip(qk_kks, es)]
    lms = [jnp.where(ri > ci, qk_kk[chunk:] * e, 0.0) * bt for qk_kk, e, bt in zip(qk_kks, es, bts)]
    ns = [n.astype(BF16) for n in _unit_lower_inverse_minus_identity(lms, ri, ci, chunk)]
    kq_es = [jnp.concatenate([(k * jnp.exp(b)).astype(BF16), (q * jnp.exp(b)).astype(BF16)], axis=0)
             for k, q, b in zip(ks, qs, bs)]
    b_lasts = [b[chunk - 1:chunk, :] for b in bs]
    kdts = [(k * jnp.exp(b_last - b)).T.astype(BF16) for k, b, b_last in zip(ks, bs, b_lasts)]
    ss = [s_scr[h] for h in range(heads)]
    for c in subs:
        sel = slice(c * heads, (c + 1) * heads)
        kq_ss = _dots(kq_es[sel], [s.astype(BF16) for s in ss])
        rs = [bt * (v - kq_s[:chunk]) for bt, v, kq_s in zip(bts[sel], vs[sel], kq_ss)]
        wbs = [(r + nr).astype(BF16) for r, nr in zip(rs, _dots(ns[sel], [r.astype(BF16) for r in rs]))]
        os_ = [kq_s[chunk:] + pw for kq_s, pw in zip(kq_ss, _dots(ps[sel], wbs))]
        ss = [jnp.exp(b_last) * s + kw for b_last, s, kw in zip(b_lasts[sel], ss, _dots(kdts[sel], wbs))]
        for h in range(heads):
            o = os_[h]
            o = o * lax.rsqrt(jnp.mean(o * o, axis=-1, keepdims=True) + EPS) * nw_ref[...]
            og_scr[c * chunk:(c + 1) * chunk, h * dv:(h + 1) * dv] = (o * _silu(tile(z, c, h * dv, dv))).astype(BF16)
    for h in range(heads):
        s_scr[h] = ss[h]
    y_ref[...] = x + _dot(og_scr[...], wout_ref[...])

    @pl.when(t == pl.num_programs(1) - 1)
    def _():
        sout_ref[0] = s_scr[...]


def _gdn_prompt(x, g, cbuf, w_in, conv_w, a_log, dt_bias, norm_w, w_out, batch, heads, dk, dv):
    m, d = x.shape
    seq = m // batch
    chunk = GDN_CHUNK
    taps = conv_w.shape[0]
    qkv_w = heads * (2 * dk + dv)
    rows = math.gcd(GDN_ROWS, seq)
    assert rows % chunk == 0 and dk == chunk and dv == chunk and chunk == LANES and 2 * heads <= LANES and taps == 4
    nt = seq // rows
    return pl.pallas_call(
        functools.partial(_gdn_prompt_kernel, heads=heads, dk=dk, dv=dv, rows=rows, chunk=chunk),
        name="gdn_prompt",
        grid=(batch, nt),
        in_specs=[
            pl.BlockSpec((rows, d), lambda b, t: (b * nt + t, 0)),
            _resident((1, d)),
            pl.BlockSpec((1, taps - 1, qkv_w), lambda b, t: (b, 0, 0)),
            _resident(w_in.shape),
            _resident(conv_w.shape),
            _resident(a_log.shape),
            _resident(dt_bias.shape),
            _resident(norm_w.shape),
            _resident(w_out.shape),
        ],
        out_specs=[
            pl.BlockSpec((rows, d), lambda b, t: (b * nt + t, 0)),
            pl.BlockSpec((1, heads, dk, dv), lambda b, t: (b, 0, 0, 0)),
            pl.BlockSpec((1, taps - 1, qkv_w), lambda b, t: (b, 0, 0)),
        ],
        out_shape=[
            jax.ShapeDtypeStruct((m, d), F32),
            jax.ShapeDtypeStruct((batch, heads, dk, dv), F32),
            jax.ShapeDtypeStruct((batch, taps - 1, qkv_w), F32),
        ],
        scratch_shapes=[
            pltpu.VMEM((heads, dk, dv), F32),
            pltpu.VMEM((SUBLANES, qkv_w), F32),
            pltpu.VMEM((SUBLANES, qkv_w), F32),
            pltpu.VMEM((rows, heads * dv), BF16),
        ],
        compiler_params=_params("arbitrary", "arbitrary"),
    )(x, g, cbuf, w_in, conv_w, a_log, dt_bias, norm_w, w_out)


def _gdn_sample_proj_kernel(x_ref, g_ref, cbuf_ref, win_ref, cw_ref, alog_ref, dtb_ref,
                            q_ref, kt_ref, k_ref, v_ref, z_ref, beta_ref, ea_ref, cout_ref, *, heads, dk, dv, taps):
    qkv_w = heads * (2 * dk + dv)
    hd = heads * dv
    halo = taps - 1
    hn = _rms(x_ref[...], g_ref[...]).astype(BF16)
    qkv = _dot(hn, win_ref[:, :qkv_w])
    conv = qkv * cw_ref[halo:taps, :]
    for j in range(halo):
        conv = conv + cbuf_ref[j] * cw_ref[j:j + 1, :]
        if j > 0:
            cout_ref[j - 1] = cbuf_ref[j]
    cout_ref[halo - 1] = qkv
    act = _silu(conv)
    for h in range(heads):
        q = act[:, h * dk:(h + 1) * dk]
        k = act[:, heads * dk + h * dk:heads * dk + (h + 1) * dk]
        q_ref[:, h * dk:(h + 1) * dk] = q * lax.rsqrt(jnp.sum(q * q, axis=-1, keepdims=True) + EPS) * dk ** -0.5
        k_ref[:, h * dk:(h + 1) * dk] = k * lax.rsqrt(jnp.sum(k * k, axis=-1, keepdims=True) + EPS)
    kt_ref[...] = k_ref[...].T
    v_ref[...] = act[:, 2 * heads * dk:]
    z_ref[...] = _dot(hn, win_ref[:, qkv_w:qkv_w + hd])
    ba = _dot(hn, win_ref[:, qkv_w + hd:])
    beta_ref[...] = _expand_heads(jax.nn.sigmoid(ba), 0, heads, dv)
    a = -jnp.exp(alog_ref[...]) * _softplus(ba + dtb_ref[...])
    ea_ref[...] = jnp.exp(_expand_heads(a, heads, heads, dv))


def _gdn_sample_state_kernel(q_ref, *refs, heads, dk, dv, block, n_layers, write_state):
    ins, outs = refs[:6 * n_layers], refs[6 * n_layers:]
    o_ref = outs[-1]
    n = q_ref.shape[0]
    lane = lax.broadcasted_iota(jnp.int32, (dk, n), 1)
    hs = range(heads)
    for i in range(block):
        b = pl.program_id(0) * block + i
        for layer in range(n_layers):
            kt_ref, k_ref, v_ref, beta_ref, ea_ref, s0_ref = ins[6 * layer:6 * layer + 6]
            k_row, v_row = k_ref[pl.ds(b, 1), :], v_ref[pl.ds(b, 1), :]
            beta_row, ea_row = beta_ref[pl.ds(b, 1), :], ea_ref[pl.ds(b, 1), :]
            s0s = [s0_ref[0, i, h] for h in hs]
            eas = [ea_row[:, h * dv:(h + 1) * dv] for h in hs]
            k8s = [jnp.broadcast_to(k_row[:, h * dk:(h + 1) * dk] * ea, (SUBLANES, dk)).astype(BF16)
                   for h, ea in zip(hs, eas)]
            kss = [_dot(k8, s0.astype(BF16))[0:1] for k8, s0 in zip(k8s, s0s)]
            ws = [beta_row[:, h * dv:(h + 1) * dv] * (v_row[:, h * dv:(h + 1) * dv] - ks) for h, ks in zip(hs, kss)]
            kt_bs = [jnp.where(lane == b, kt_ref[h * dk:(h + 1) * dk, :], 0.0).astype(BF16) for h in hs]
            s_news = [ea * s0 + _dot(kt_b, jnp.broadcast_to(w, (n, dv)).astype(BF16))
                      for ea, s0, kt_b, w in zip(eas, s0s, kt_bs, ws)]
            if write_state:
                for h in hs:
                    outs[0][layer, i, h] = s_news[h]
        q_row = q_ref[pl.ds(b, 1), :]
        q8s = [jnp.broadcast_to(q_row[:, h * dk:(h + 1) * dk], (SUBLANES, dk)).astype(BF16) for h in hs]
        o_ref[i:i + 1, :] = jnp.concatenate(
            [_dot(q8, s_new.astype(BF16))[0:1] for q8, s_new in zip(q8s, s_news)], axis=1)


def _gdn_sample(x, g, cbuf, w_in, conv_w, a_log, dt_bias, norm_w, w_out, s0, layer, earlier):
    n, d = x.shape
    layers, _, heads, dk, dv = s0.shape
    taps = conv_w.shape[0]
    qkv_w = heads * (2 * dk + dv)
    hd = heads * dv
    block = math.gcd(GDN_STATE_BLOCK, n)
    cbuf_t = jnp.swapaxes(cbuf, 0, 1)
    q, kt, k, v, z, beta, ea, cout_t = pl.pallas_call(
        functools.partial(_gdn_sample_proj_kernel, heads=heads, dk=dk, dv=dv, taps=taps),
        name="gdn_sample_proj",
        out_shape=[
            jax.ShapeDtypeStruct((n, heads * dk), F32),
            jax.ShapeDtypeStruct((heads * dk, n), F32),
            jax.ShapeDtypeStruct((n, heads * dk), F32),
            jax.ShapeDtypeStruct((n, hd), F32),
            jax.ShapeDtypeStruct((n, hd), F32),
            jax.ShapeDtypeStruct((n, hd), F32),
            jax.ShapeDtypeStruct((n, hd), F32),
            jax.ShapeDtypeStruct((taps - 1, n, qkv_w), F32),
        ],
        compiler_params=_params(),
    )(x, g, cbuf_t, w_in, conv_w, a_log, dt_bias)
    last = layer == layers - 1
    own = (kt, k, v, beta, ea)
    handled = list(enumerate(earlier)) + [(layer, own)] if last else [(layer, own)]
    args, in_specs = [q], [_resident(q.shape)]
    for idx, small in handled:
        args += list(small) + [s0]
        in_specs += [_resident(t.shape) for t in small]
        in_specs.append(pl.BlockSpec((1, block, heads, dk, dv), lambda i, idx=idx: (idx, i, 0, 0, 0)))
    o_spec = pl.BlockSpec((block, hd), lambda i: (i, 0))
    o_shape = jax.ShapeDtypeStruct((n, hd), F32)
    outs = pl.pallas_call(
        functools.partial(_gdn_sample_state_kernel, heads=heads, dk=dk, dv=dv, block=block,
                          n_layers=len(handled), write_state=last),
        name="gdn_sample_state",
        grid=(n // block,),
        in_specs=in_specs,
        out_specs=[pl.BlockSpec((layers, block, heads, dk, dv), lambda i: (0, i, 0, 0, 0)), o_spec] if last
        else [o_spec],
        out_shape=[jax.ShapeDtypeStruct(s0.shape, F32), o_shape] if last else [o_shape],
        compiler_params=_params("arbitrary"),
    )(*args)
    y = _mixer_out(x, outs[-1], z, norm_w, w_out, heads, dv)
    return y, (outs[0] if last else None), jnp.swapaxes(cout_t, 0, 1), earlier + [own]


def _rope_tables(pos, dk):
    theta = 1.0 / (ROPE_BASE ** jnp.linspace(0.0, 1.0, dk // 2, dtype=F32))
    ang = pos[:, None] * theta[None, :]
    return jnp.cos(ang), jnp.sin(ang)


def _prep_ret_weights(w_in, heads, dk):
    qk_w = 2 * heads * dk
    w_qk = w_in[:, :qk_w]
    col = jnp.arange(qk_w)
    within = col % dk
    deint = (col - within) + jnp.where(within < dk // 2, 2 * within, 2 * (within - dk // 2) + 1)
    w_deint = jnp.concatenate([w_qk[:, deint], w_in[:, qk_w:]], axis=1).astype(BF16)
    swapped = w_qk[:, col ^ 1] * jnp.where(col % 2 == 0, -1.0, 1.0)[None, :]
    return w_in.astype(BF16), w_deint, swapped.astype(BF16)


def _prep_gdn_weights(w_in, a_log, dt_bias, heads, dk, dv):
    pad = LANES - 2 * heads
    w = jnp.pad(w_in, ((0, 0), (0, pad))).astype(BF16)
    place = lambda p: jnp.pad(p, (heads, pad))[None, :]
    return w, place(a_log), place(dt_bias)


def _trunk(x, batch, pos0, s_ret, s_gdn, s_conv, prompt, norm_g, ffn_gu, ffn_down, ret_w, ret_w_out, gdn_w,
           gdn_conv_w, gdn_norm_w, gdn_w_out, final_g):
    depth = norm_g.shape[0]
    n_mixers = 2
    m, d = x.shape
    seq = m // batch
    n_ret, _, ret_heads, ret_dk, ret_dv = s_ret.shape
    n_gdn, _, gdn_heads, gdn_dk, gdn_dv = s_gdn.shape
    cos, sin = _rope_tables(jnp.arange(seq, dtype=F32) + pos0, ret_dk)
    if not prompt:
        cos = jnp.tile(jnp.repeat(cos, 2, axis=1), (1, 2 * ret_heads))
        sin = jnp.tile(jnp.repeat(sin, 2, axis=1), (1, 2 * ret_heads))
    ones_dv = jnp.ones((1, ret_dv), F32)
    fg = final_g[None, :]
    norm_g4 = norm_g[:, :, None, :]
    new_ret, new_gdn, new_conv = [], [], []
    ret_hist, gdn_hist = [], []
    for i in range(depth):
        x = _ffn(x, norm_g4, ffn_gu, ffn_down, fg, i, 0, 0, False)
        j = i // n_mixers
        g = norm_g[i, 1][None, :]
        if i % n_mixers == 0:
            w_orig, w_deint, w_swap = ret_w[j]
            if prompt:
                x, s = _ret_prompt(x, g, cos, sin, w_deint, ret_w_out[j], batch, ret_heads, ret_dk, ret_dv)
                new_ret.append(s)
            else:
                x, new_ret, ret_hist = _ret_sample(x, g, cos, sin, w_orig, w_swap, ret_w_out[j], s_ret, ones_dv,
                                                   j, ret_hist)
        else:
            w, a_log, dt_bias = gdn_w[j]
            nw = gdn_norm_w[j][None, :]
            if prompt:
                x, s, buf = _gdn_prompt(x, g, s_conv[j], w, gdn_conv_w[j], a_log, dt_bias, nw, gdn_w_out[j],
                                        batch, gdn_heads, gdn_dk, gdn_dv)
                new_gdn.append(s)
            else:
                x, new_gdn, buf, gdn_hist = _gdn_sample(x, g, s_conv[j], w, gdn_conv_w[j], a_log, dt_bias, nw,
                                                        gdn_w_out[j], s_gdn, j, gdn_hist)
            new_conv.append(buf)
        x = _ffn(x, norm_g4, ffn_gu, ffn_down, fg, i, 2, 1, i == depth - 1)
    if prompt:
        new_ret, new_gdn = jnp.stack(new_ret), jnp.stack(new_gdn)
    return x, new_ret, new_gdn, jnp.stack(new_conv)


def kernel(x_prompt, x_sample, state_ret, state_gdn, state_gdn_conv, norm_g, ffn_gu, ffn_down, ret_w_in,
           ret_w_out, gdn_w_in, gdn_conv_w, gdn_a_log, gdn_dt_bias, gdn_norm_w, gdn_w_out, final_g):
    bp, seq, d = x_prompt.shape
    bs, dec_seq, _ = x_sample.shape
    assert dec_seq == 1
    n_ret, _, ret_heads, ret_dk, ret_dv = state_ret.shape
    n_gdn, _, gdn_heads, gdn_dk, gdn_dv = state_gdn.shape

    ffn_gu_b = ffn_gu.astype(BF16)
    ffn_down_b = ffn_down.astype(BF16)
    ret_w = [_prep_ret_weights(ret_w_in[j], ret_heads, ret_dk) for j in range(n_ret)]
    ret_w_out_b = ret_w_out.astype(BF16)
    gdn_w = [_prep_gdn_weights(gdn_w_in[j], gdn_a_log[j], gdn_dt_bias[j], gdn_heads, gdn_dk, gdn_dv)
             for j in range(n_gdn)]
    gdn_w_out_b = gdn_w_out.astype(BF16)
    shared = (norm_g, ffn_gu_b, ffn_down_b, ret_w, ret_w_out_b, gdn_w, gdn_conv_w, gdn_norm_w, gdn_w_out_b, final_g)

    zero_ret = jax.ShapeDtypeStruct((n_ret, bp, ret_heads, ret_dk, ret_dv), F32)
    zero_gdn = jax.ShapeDtypeStruct((n_gdn, bp, gdn_heads, gdn_dk, gdn_dv), F32)
    zero_conv = jnp.zeros((n_gdn, bp) + state_gdn_conv.shape[2:], state_gdn_conv.dtype)
    y_p, ret_p, gdn_p, conv_p = _trunk(x_prompt.reshape(bp * seq, d), bp, 0.0, zero_ret, zero_gdn, zero_conv,
                                       True, *shared)
    y_s, ret_s, gdn_s, conv_s = _trunk(x_sample.reshape(bs, d), bs, SAMPLE_PAST_LEN, state_ret, state_gdn,
                                       state_gdn_conv, False, *shared)
    return (y_p.reshape(bp, seq, d), y_s.reshape(bs, dec_seq, d), ret_p, gdn_p, conv_p, ret_s, gdn_s, conv_s)
```

```python
import functools
import math

import jax
import jax.numpy as jnp
from jax import lax
from jax.experimental import pallas as pl
from jax.experimental.pallas import tpu as pltpu

F32 = jnp.float32
BF16 = jnp.bfloat16
EPS = 1e-6
ROPE_BASE = 10000.0
SAMPLE_PAST_LEN = 16384.0

LANES = 128
SUBLANES = 8
VMEM_LIMIT_BYTES = 56 * 1024 * 1024

FFN_ROWS = 512
RET_CHUNK = 256
GDN_CHUNK = 128
GDN_ROWS = 256
RET_STATE_BLOCK = 2
GDN_STATE_BLOCK = 8


def _dot(a, b):
    return jnp.dot(a, b, preferred_element_type=F32)


def _dot_nt(a, b):
    return lax.dot_general(a, b, (((1,), (1,)), ((), ())), preferred_element_type=F32)


def _rms(x, g):
    return x * lax.rsqrt(jnp.mean(x * x, axis=-1, keepdims=True) + EPS) * g


def _silu(x):
    h = 0.5 * x
    return h * jnp.tanh(h) + h


def _softplus(x):
    return jnp.maximum(x, 0.0) + jnp.log1p(jnp.exp(-jnp.abs(x)))


def _bf16_terms(x):
    t0 = x.astype(BF16)
    r = x - t0.astype(F32)
    t1 = r.astype(BF16)
    t2 = (r - t1.astype(F32)).astype(BF16)
    return t0, t1, t2


def _exact_dot_right(x, sel):
    return _dot(jnp.concatenate(_bf16_terms(x), axis=1), jnp.concatenate([sel] * 3, axis=0))


def _exact_dot_left(sel, x):
    return _dot(jnp.concatenate([sel] * 3, axis=1), jnp.concatenate(_bf16_terms(x), axis=0))


def _expand_heads(x, first, heads, width):
    c = lax.broadcasted_iota(jnp.int32, (x.shape[1], heads * width), 0)
    lane = lax.broadcasted_iota(jnp.int32, (x.shape[1], heads * width), 1)
    e = ((lane >= (c - first) * width) & (lane < (c - first + 1) * width)).astype(BF16)
    return _exact_dot_right(x, e)


def _dots(lhs, rhs, nt=False):
    dot = _dot_nt if nt else _dot
    return [dot(a, b) for a, b in zip(lhs, rhs)]


def _resident(shape):
    nd = len(shape)
    return pl.BlockSpec(shape, lambda *_: (0,) * nd, pipeline_mode=pl.Buffered(1))


def _params(*sem):
    return pltpu.CompilerParams(dimension_semantics=sem, vmem_limit_bytes=VMEM_LIMIT_BYTES)


def _ffn_kernel(x_ref, g_ref, wg_ref, wu_ref, wd_ref, fg_ref, *rest, final, cast_next):
    o_ref = rest[2] if cast_next else rest[0]
    x = x_ref[...]
    h = _rms(x, g_ref[...]).astype(BF16)
    a = _dot(h, wg_ref[...])
    b = _dot(h, wu_ref[...])
    act = (_silu(a) * b).astype(BF16)
    y = x + 0.5 * _dot(act, wd_ref[...])
    if final:
        y = _rms(y, fg_ref[...])
    o_ref[...] = y
    if cast_next:
        next_gu_ref, next_down_ref, _, gu_out_ref, down_out_ref = rest
        gu_out_ref[...] = next_gu_ref[...].astype(BF16)
        down_out_ref[...] = next_down_ref[...].astype(BF16)


def _slab_count(rows, steps):
    return max(n for n in range(1, steps + 1) if steps % n == 0 and rows % (16 * n) == 0)


def _ffn(x, norm_g, w_gu, w_down, final_g, layer, which_norm, final, cast_next=None):
    m, d = x.shape
    f = w_down.shape[0]
    tm = min(FFN_ROWS, m)
    steps = m // tm
    assert m % tm == 0 and f % LANES == 0
    in_specs = [
        pl.BlockSpec((tm, d), lambda i: (i, 0)),
        pl.BlockSpec((None, None, 1, d), lambda i: (layer, which_norm, 0, 0), pipeline_mode=pl.Buffered(1)),
        pl.BlockSpec((d, f), lambda i: (0, 0), pipeline_mode=pl.Buffered(1)),
        pl.BlockSpec((d, f), lambda i: (0, 1), pipeline_mode=pl.Buffered(1)),
        _resident((f, d)),
        _resident((1, d)),
    ]
    args = [x, norm_g, w_gu, w_gu, w_down, final_g]
    out_specs = [pl.BlockSpec((tm, d), lambda i: (i, 0))]
    out_shape = [jax.ShapeDtypeStruct((m, d), F32)]
    if cast_next is not None:
        next_gu, next_down, nl, nw = cast_next
        for w, rows, cols in ((next_gu, d, 2 * f), (next_down, f, d)):
            n = _slab_count(rows, steps)
            in_specs.append(pl.BlockSpec((None, None, rows // n, cols),
                                         lambda i, n=n: (nl, nw, i * n // steps, 0)))
            out_specs.append(pl.BlockSpec((rows // n, cols), lambda i, n=n: (i * n // steps, 0)))
            out_shape.append(jax.ShapeDtypeStruct((rows, cols), BF16))
            args.append(w)
    outs = pl.pallas_call(
        functools.partial(_ffn_kernel, final=final, cast_next=cast_next is not None),
        name="ffn",
        grid=(steps,),
        in_specs=in_specs,
        out_specs=out_specs,
        out_shape=out_shape,
        compiler_params=_params("arbitrary"),
    )(*args)
    return outs[0], tuple(outs[1:])


def _ret_prompt_kernel(x_ref, g_ref, cos_ref, sin_ref, win_ref, wout_ref, y_ref, sout_ref, s_scr, og_scr,
                       *, heads, dk, dv, chunk):
    t = pl.program_id(1)
    half = dk // 2

    @pl.when(t == 0)
    def _():
        s_scr[...] = jnp.zeros_like(s_scr)

    x = x_ref[...]
    hn = _rms(x, g_ref[...]).astype(BF16)
    cos = cos_ref[...]
    sin = sin_ref[...]
    ri = lax.broadcasted_iota(jnp.int32, (chunk, chunk), 0)
    ci = lax.broadcasted_iota(jnp.int32, (chunk, chunk), 1)
    lag = (ri - ci).astype(F32)
    row = lax.broadcasted_iota(jnp.int32, (chunk, half), 0).astype(F32)
    k_off, v_off, g_off = heads * dk, 2 * heads * dk, 2 * heads * dk + heads * dv
    for h in range(heads):
        log_gamma = math.log(1.0 - 2.0 ** (-5.0 - h))
        q = _dot(hn, win_ref[:, h * dk:(h + 1) * dk])
        k = _dot(hn, win_ref[:, k_off + h * dk:k_off + (h + 1) * dk])
        v = _dot(hn, win_ref[:, v_off + h * dv:v_off + (h + 1) * dv]).astype(BF16)
        gate = _dot(hn, win_ref[:, g_off + h * dv:g_off + (h + 1) * dv])
        q1, q2 = q[:, :half], q[:, half:]
        k1, k2 = k[:, :half], k[:, half:]
        qr1, qr2 = q1 * cos - q2 * sin, q2 * cos + q1 * sin
        kr1, kr2 = (k1 * cos - k2 * sin) * dk ** -0.5, (k2 * cos + k1 * sin) * dk ** -0.5
        qb = jnp.concatenate([qr1, qr2], axis=1).astype(BF16)
        kb = jnp.concatenate([kr1, kr2], axis=1).astype(BF16)
        decay = jnp.where(ri >= ci, jnp.exp(lag * log_gamma), 0.0)
        p = (_dot_nt(qb, kb) * decay).astype(BF16)
        q_scale = jnp.exp((row + 1.0) * log_gamma)
        k_scale = jnp.exp((chunk - 1.0 - row) * log_gamma)
        qd = jnp.concatenate([qr1 * q_scale, qr2 * q_scale], axis=1).astype(BF16)
        kd = jnp.concatenate([kr1 * k_scale, kr2 * k_scale], axis=1)
        s = s_scr[h]
        o = _dot(p, v) + _dot(qd, s.astype(BF16))
        s_scr[h] = math.exp(chunk * log_gamma) * s + _dot(kd.T.astype(BF16), v)
        o = o * lax.rsqrt(jnp.mean(o * o, axis=-1, keepdims=True) + EPS) * _silu(gate)
        og_scr[:, h * dv:(h + 1) * dv] = o.astype(BF16)
    y_ref[...] = x + _dot(og_scr[...], wout_ref[...])

    @pl.when(t == pl.num_programs(1) - 1)
    def _():
        ro = lax.broadcasted_iota(jnp.int32, (dk, dk), 0)
        rin = lax.broadcasted_iota(jnp.int32, (dk, dk), 1)
        perm = (rin == (ro >> 1) + (ro & 1) * half).astype(BF16)
        for h in range(heads):
            sout_ref[0, h] = _exact_dot_left(perm, s_scr[h])


def _ret_prompt(x, g, cos, sin, w_in, w_out, batch, heads, dk, dv):
    m, d = x.shape
    seq = m // batch
    chunk = min(RET_CHUNK, seq)
    assert seq % chunk == 0
    nt = seq // chunk
    return pl.pallas_call(
        functools.partial(_ret_prompt_kernel, heads=heads, dk=dk, dv=dv, chunk=chunk),
        name="ret_prompt",
        grid=(batch, nt),
        in_specs=[
            pl.BlockSpec((chunk, d), lambda b, t: (b * nt + t, 0)),
            _resident((1, d)),
            pl.BlockSpec((chunk, dk // 2), lambda b, t: (t, 0)),
            pl.BlockSpec((chunk, dk // 2), lambda b, t: (t, 0)),
            _resident(w_in.shape),
            _resident(w_out.shape),
        ],
        out_specs=[
            pl.BlockSpec((chunk, d), lambda b, t: (b * nt + t, 0)),
            pl.BlockSpec((1, heads, dk, dv), lambda b, t: (b, 0, 0, 0)),
        ],
        out_shape=[
            jax.ShapeDtypeStruct((m, d), F32),
            jax.ShapeDtypeStruct((batch, heads, dk, dv), F32),
        ],
        scratch_shapes=[pltpu.VMEM((heads, dk, dv), F32), pltpu.VMEM((chunk, heads * dv), BF16)],
        compiler_params=_params("arbitrary", "arbitrary"),
    )(x, g, cos, sin, w_in, w_out)


def _ret_sample_proj_kernel(x_ref, g_ref, cos_ref, sin_ref, win_ref, wsw_ref, q_ref, kt_ref, v_ref, gate_ref,
                            *, heads, dk, dv):
    qk_w = 2 * heads * dk
    hn = _rms(x_ref[...], g_ref[...]).astype(BF16)
    qk = _dot(hn, win_ref[:, :qk_w]) * cos_ref[...] + _dot(hn, wsw_ref[...]) * sin_ref[...]
    q_ref[...] = qk[:, :heads * dk]
    kt_ref[...] = (qk[:, heads * dk:] * dk ** -0.5).T
    v_ref[...] = _dot(hn, win_ref[:, qk_w:qk_w + heads * dv])
    gate_ref[...] = _dot(hn, win_ref[:, qk_w + heads * dv:])


def _ret_sample_state_kernel(q_ref, *refs, heads, dk, dv, block, n_layers, write_state):
    ins, outs = refs[:3 * n_layers], refs[3 * n_layers:]
    o_ref = outs[-1]
    n = q_ref.shape[0]
    lane = lax.broadcasted_iota(jnp.int32, (dk, n), 1)
    hs = range(heads)
    gammas = [1.0 - 2.0 ** (-5.0 - h) for h in hs]
    for i in range(block):
        b = pl.program_id(0) * block + i
        for layer in range(n_layers):
            kt_ref, v_ref, s0_ref = ins[3 * layer:3 * layer + 3]
            kt_bs = [jnp.where(lane == b, kt_ref[h * dk:(h + 1) * dk, :], 0.0).astype(BF16) for h in hs]
            s_news = [gamma * s0_ref[0, i, h] + _dot(kt_b, v_ref[:, h * dv:(h + 1) * dv].astype(BF16))
                      for h, gamma, kt_b in zip(hs, gammas, kt_bs)]
            if write_state:
                for h in hs:
                    outs[0][layer, i, h] = s_news[h]
        q_row = q_ref[pl.ds(b, 1), :]
        q8s = [jnp.broadcast_to(q_row[:, h * dk:(h + 1) * dk], (SUBLANES, dk)).astype(BF16) for h in hs]
        o_ref[0, i:i + 1, :] = jnp.concatenate(
            [_dot(q8, s_new.astype(BF16))[0:1] for q8, s_new in zip(q8s, s_news)], axis=1)


def _mixer_out_kernel(x_ref, o_ref, gate_ref, nw_ref, wout_ref, y_ref, *, heads, dv):
    parts = []
    for h in range(heads):
        o = o_ref[:, h * dv:(h + 1) * dv]
        o = o * lax.rsqrt(jnp.mean(o * o, axis=-1, keepdims=True) + EPS) * nw_ref[...]
        parts.append((o * _silu(gate_ref[:, h * dv:(h + 1) * dv])).astype(BF16))
    y_ref[...] = x_ref[...] + _dot(jnp.concatenate(parts, axis=1), wout_ref[...])


def _mixer_out(x, o, gate, norm_w, w_out, heads, dv):
    return pl.pallas_call(
        functools.partial(_mixer_out_kernel, heads=heads, dv=dv),
        name="mixer_out",
        out_shape=jax.ShapeDtypeStruct(x.shape, F32),
        compiler_params=_params(),
    )(x, o, gate, norm_w, w_out)


def _ret_sample(x, g, cos, sin, w_in, w_sw, w_out, s0, ones_dv, layer, earlier):
    n, d = x.shape
    layers, _, heads, dk, dv = s0.shape
    last = layer == layers - 1
    block = math.gcd(RET_STATE_BLOCK if last else 2 * RET_STATE_BLOCK, n)
    q, kt, v, gate = pl.pallas_call(
        functools.partial(_ret_sample_proj_kernel, heads=heads, dk=dk, dv=dv),
        name="ret_sample_proj",
        out_shape=[
            jax.ShapeDtypeStruct((n, heads * dk), F32),
            jax.ShapeDtypeStruct((heads * dk, n), F32),
            jax.ShapeDtypeStruct((n, heads * dv), F32),
            jax.ShapeDtypeStruct((n, heads * dv), F32),
        ],
        compiler_params=_params(),
    )(x, g, cos, sin, w_in, w_sw)
    handled = list(enumerate(earlier)) + [(layer, (kt, v))] if last else [(layer, (kt, v))]
    args, in_specs = [q], [_resident(q.shape)]
    for idx, (kt_l, v_l) in handled:
        args += [kt_l, v_l, s0]
        in_specs += [_resident(kt_l.shape), _resident(v_l.shape),
                     pl.BlockSpec((1, block, heads, dk, dv), lambda i, idx=idx: (idx, i, 0, 0, 0))]
    o_spec = pl.BlockSpec((1, block, heads * dv), lambda i: (i, 0, 0))
    o_shape = jax.ShapeDtypeStruct((n // block, block, heads * dv), F32)
    outs = pl.pallas_call(
        functools.partial(_ret_sample_state_kernel, heads=heads, dk=dk, dv=dv, block=block,
                          n_layers=len(handled), write_state=last),
        name="ret_sample_state",
        grid=(n // block,),
        in_specs=in_specs,
        out_specs=[pl.BlockSpec((layers, block, heads, dk, dv), lambda i: (0, i, 0, 0, 0)), o_spec] if last
        else [o_spec],
        out_shape=[jax.ShapeDtypeStruct(s0.shape, F32), o_shape] if last else [o_shape],
        compiler_params=_params("arbitrary"),
    )(*args)
    y = _mixer_out(x, outs[-1].reshape(n, heads * dv), gate, ones_dv, w_out, heads, dv)
    return y, (outs[0] if last else None), earlier + [(kt, v)]


def _unit_lower_inverse_minus_identity(lms, ri, ci, chunk):
    def level_mask(lev):
        return ((ri >> lev) == (ci >> lev)) & ((ri >> (lev - 1)) != (ci >> (lev - 1)))

    mask = level_mask(1)
    ns = [jnp.where(mask, -lm, 0.0) for lm in lms]
    for lev in range(2, chunk.bit_length()):
        mask = level_mask(lev)
        cs = [jnp.where(mask, lm, 0.0) for lm in lms]
        nbs = [n.astype(BF16) for n in ns]
        gs = [c + cn for c, cn in zip(cs, _dots([c.astype(BF16) for c in cs], nbs))]
        ns = [n - g - ng for n, g, ng in zip(ns, gs, _dots(nbs, [g.astype(BF16) for g in gs]))]
    return ns


def _gdn_prompt_kernel(x_ref, g_ref, cbuf_ref, win_ref, cw_ref, alog_ref, dtb_ref, nw_ref, wout_ref,
                       y_ref, sout_ref, cout_ref, s_scr, xh_scr, uh_scr, og_scr, *, heads, dk, dv, rows, chunk):
    t = pl.program_id(1)
    qkv_w = heads * (2 * dk + dv)
    hd = heads * dv
    cw = [cw_ref[j:j + 1, :] for j in range(4)]

    @pl.when(t == 0)
    def _():
        s_scr[...] = jnp.zeros_like(s_scr)
        xh_scr[...] = jnp.zeros_like(xh_scr)
        uh_scr[...] = jnp.zeros_like(uh_scr)
        xh_scr[0:1, :] = cbuf_ref[0, 2:3, :]
        uh_scr[0:1, :] = cw[1] * cbuf_ref[0, 1:2, :] + cw[0] * cbuf_ref[0, 0:1, :]
        uh_scr[1:2, :] = cw[1] * cbuf_ref[0, 2:3, :] + cw[0] * cbuf_ref[0, 1:2, :]

    x = x_ref[...]
    hn = _rms(x, g_ref[...]).astype(BF16)
    qkv = _dot(hn, win_ref[:, :qkv_w])
    sub = lax.broadcasted_iota(jnp.int32, (SUBLANES, qkv_w), 0)

    def shifted(cur, by, halo_tile):
        rolled = pltpu.roll(cur, by, 0)
        first = jnp.where(sub < by, halo_tile, rolled[:SUBLANES])
        return jnp.concatenate([first, rolled[SUBLANES:]], axis=0)

    x_prev = shifted(qkv, 1, xh_scr[...])
    u = cw[1] * qkv + cw[0] * x_prev
    conv = cw[3] * qkv + cw[2] * x_prev + shifted(u, 2, uh_scr[...])
    xh_scr[0:1, :] = qkv[rows - 1:rows, :]
    uh_scr[0:2, :] = u[rows - 2:rows, :]

    @pl.when(t == pl.num_programs(1) - 1)
    def _():
        cout_ref[0] = qkv[rows - 3:rows, :]

    act = _silu(conv)
    z = _dot(hn, win_ref[:, qkv_w:qkv_w + hd])
    ba = _dot(hn, win_ref[:, qkv_w + hd:])
    a = -jnp.exp(alog_ref[...]) * _softplus(ba + dtb_ref[...])
    ri = lax.broadcasted_iota(jnp.int32, (chunk, chunk), 0)
    ci = lax.broadcasted_iota(jnp.int32, (chunk, chunk), 1)
    tril = (ri >= ci).astype(BF16)
    subs = range(rows // chunk)
    bcum = jnp.concatenate([_exact_dot_left(tril, a[c * chunk:(c + 1) * chunk]) for c in subs], axis=0)
    beta = _expand_heads(jax.nn.sigmoid(ba), 0, heads, dv)
    bcum = _expand_heads(bcum, heads, heads, dv)
    probs = [(c, h) for c in subs for h in range(heads)]

    def tile(arr, c, col, width):
        return arr[c * chunk:(c + 1) * chunk, col:col + width]

    qs = [tile(act, c, h * dk, dk) for c, h in probs]
    ks = [tile(act, c, heads * dk + h * dk, dk) for c, h in probs]
    vs = [tile(act, c, 2 * heads * dk + h * dv, dv) for c, h in probs]
    qs = [q * lax.rsqrt(jnp.sum(q * q, axis=-1, keepdims=True) + EPS) * dk ** -0.5 for q in qs]
    ks = [k * lax.rsqrt(jnp.sum(k * k, axis=-1, keepdims=True) + EPS) for k in ks]
    bts = [tile(beta, c, h * dv, dv) for c, h in probs]
    bs = [tile(bcum, c, h * dv, dv) for c, h in probs]
    es = [jnp.exp(jnp.minimum(b - b.T, 0.0)) for b in bs]
    kbs = [k.astype(BF16) for k in ks]
    qk_kks = _dots([jnp.concatenate([q.astype(BF16), kb], axis=0) for q, kb in zip(qs, kbs)], kbs, nt=True)
    ps = [jnp.where(ri >= ci, qk_kk[:chunk] * e, 0.0).astype(BF16) for qk_kk, e in zip(qk_kks, es)]
    lms = [jnp.where(ri > ci, qk_kk[chunk:] * e, 0.0) * bt for qk_kk, e, bt in zip(qk_kks, es, bts)]
    ns = [n.astype(BF16) for n in _unit_lower_inverse_minus_identity(lms, ri, ci, chunk)]
    kq_es = [jnp.concatenate([(k * jnp.exp(b)).astype(BF16), (q * jnp.exp(b)).astype(BF16)], axis=0)
             for k, q, b in zip(ks, qs, bs)]
    b_lasts = [b[chunk - 1:chunk, :] for b in bs]
    kdts = [(k * jnp.exp(b_last - b)).T.astype(BF16) for k, b, b_last in zip(ks, bs, b_lasts)]
    ss = [s_scr[h] for h in range(heads)]
    for c in subs:
        sel = slice(c * heads, (c + 1) * heads)
        kq_ss = _dots(kq_es[sel], [s.astype(BF16) for s in ss])
        rs = [bt * (v - kq_s[:chunk]) for bt, v, kq_s in zip(bts[sel], vs[sel], kq_ss)]
        wbs = [(r + nr).astype(BF16) for r, nr in zip(rs, _dots(ns[sel], [r.astype(BF16) for r in rs]))]
        os_ = [kq_s[chunk:] + pw for kq_s, pw in zip(kq_ss, _dots(ps[sel], wbs))]
        ss = [jnp.exp(b_last) * s + kw for b_last, s, kw in zip(b_lasts[sel], ss, _dots(kdts[sel], wbs))]
        for h in range(heads):
            o = os_[h]
            o = o * lax.rsqrt(jnp.mean(o * o, axis=-1, keepdims=True) + EPS) * nw_ref[...]
            og_scr[c * chunk:(c + 1) * chunk, h * dv:(h + 1) * dv] = (o * _silu(tile(z, c, h * dv, dv))).astype(BF16)
    for h in range(heads):
        s_scr[h] = ss[h]
    y_ref[...] = x + _dot(og_scr[...], wout_ref[...])

    @pl.when(t == pl.num_programs(1) - 1)
    def _():
        sout_ref[0] = s_scr[...]


def _gdn_prompt(x, g, cbuf, w_in, conv_w, a_log, dt_bias, norm_w, w_out, batch, heads, dk, dv):
    m, d = x.shape
    seq = m // batch
    chunk = GDN_CHUNK
    taps = conv_w.shape[0]
    qkv_w = heads * (2 * dk + dv)
    rows = math.gcd(GDN_ROWS, seq)
    assert rows % chunk == 0 and dk == chunk and dv == chunk and chunk == LANES and 2 * heads <= LANES and taps == 4
    nt = seq // rows
    return pl.pallas_call(
        functools.partial(_gdn_prompt_kernel, heads=heads, dk=dk, dv=dv, rows=rows, chunk=chunk),
        name="gdn_prompt",
        grid=(batch, nt),
        in_specs=[
            pl.BlockSpec((rows, d), lambda b, t: (b * nt + t, 0)),
            _resident((1, d)),
            pl.BlockSpec((1, taps - 1, qkv_w), lambda b, t: (b, 0, 0)),
            _resident(w_in.shape),
            _resident(conv_w.shape),
            _resident(a_log.shape),
            _resident(dt_bias.shape),
            _resident(norm_w.shape),
            _resident(w_out.shape),
        ],
        out_specs=[
            pl.BlockSpec((rows, d), lambda b, t: (b * nt + t, 0)),
            pl.BlockSpec((1, heads, dk, dv), lambda b, t: (b, 0, 0, 0)),
            pl.BlockSpec((1, taps - 1, qkv_w), lambda b, t: (b, 0, 0)),
        ],
        out_shape=[
            jax.ShapeDtypeStruct((m, d), F32),
            jax.ShapeDtypeStruct((batch, heads, dk, dv), F32),
            jax.ShapeDtypeStruct((batch, taps - 1, qkv_w), F32),
        ],
        scratch_shapes=[
            pltpu.VMEM((heads, dk, dv), F32),
            pltpu.VMEM((SUBLANES, qkv_w), F32),
            pltpu.VMEM((SUBLANES, qkv_w), F32),
            pltpu.VMEM((rows, heads * dv), BF16),
        ],
        compiler_params=_params("arbitrary", "arbitrary"),
    )(x, g, cbuf, w_in, conv_w, a_log, dt_bias, norm_w, w_out)


def _gdn_sample_proj_kernel(x_ref, g_ref, cbuf_ref, win_ref, cw_ref, alog_ref, dtb_ref,
                            q_ref, kt_ref, k_ref, v_ref, z_ref, beta_ref, ea_ref, cout_ref, *, heads, dk, dv, taps):
    qkv_w = heads * (2 * dk + dv)
    hd = heads * dv
    halo = taps - 1
    hn = _rms(x_ref[...], g_ref[...]).astype(BF16)
    qkv = _dot(hn, win_ref[:, :qkv_w])
    conv = qkv * cw_ref[halo:taps, :]
    for j in range(halo):
        conv = conv + cbuf_ref[j] * cw_ref[j:j + 1, :]
        if j > 0:
            cout_ref[j - 1] = cbuf_ref[j]
    cout_ref[halo - 1] = qkv
    act = _silu(conv)
    for h in range(heads):
        q = act[:, h * dk:(h + 1) * dk]
        k = act[:, heads * dk + h * dk:heads * dk + (h + 1) * dk]
        q_ref[:, h * dk:(h + 1) * dk] = q * lax.rsqrt(jnp.sum(q * q, axis=-1, keepdims=True) + EPS) * dk ** -0.5
        k_ref[:, h * dk:(h + 1) * dk] = k * lax.rsqrt(jnp.sum(k * k, axis=-1, keepdims=True) + EPS)
    kt_ref[...] = k_ref[...].T
    v_ref[...] = act[:, 2 * heads * dk:]
    z_ref[...] = _dot(hn, win_ref[:, qkv_w:qkv_w + hd])
    ba = _dot(hn, win_ref[:, qkv_w + hd:])
    beta_ref[...] = _expand_heads(jax.nn.sigmoid(ba), 0, heads, dv)
    a = -jnp.exp(alog_ref[...]) * _softplus(ba + dtb_ref[...])
    ea_ref[...] = jnp.exp(_expand_heads(a, heads, heads, dv))


def _gdn_sample_state_kernel(q_ref, *refs, heads, dk, dv, block, n_layers, write_state):
    ins, outs = refs[:6 * n_layers], refs[6 * n_layers:]
    o_ref = outs[-1]
    n = q_ref.shape[0]
    lane = lax.broadcasted_iota(jnp.int32, (dk, n), 1)
    hs = range(heads)
    for i in range(block):
        b = pl.program_id(0) * block + i
        for layer in range(n_layers):
            kt_ref, k_ref, v_ref, beta_ref, ea_ref, s0_ref = ins[6 * layer:6 * layer + 6]
            k_row, v_row = k_ref[pl.ds(b, 1), :], v_ref[pl.ds(b, 1), :]
            beta_row, ea_row = beta_ref[pl.ds(b, 1), :], ea_ref[pl.ds(b, 1), :]
            s0s = [s0_ref[0, i, h] for h in hs]
            eas = [ea_row[:, h * dv:(h + 1) * dv] for h in hs]
            k8s = [jnp.broadcast_to(k_row[:, h * dk:(h + 1) * dk] * ea, (SUBLANES, dk)).astype(BF16)
                   for h, ea in zip(hs, eas)]
            kss = [_dot(k8, s0.astype(BF16))[0:1] for k8, s0 in zip(k8s, s0s)]
            ws = [beta_row[:, h * dv:(h + 1) * dv] * (v_row[:, h * dv:(h + 1) * dv] - ks) for h, ks in zip(hs, kss)]
            kt_bs = [jnp.where(lane == b, kt_ref[h * dk:(h + 1) * dk, :], 0.0).astype(BF16) for h in hs]
            s_news = [ea * s0 + _dot(kt_b, jnp.broadcast_to(w, (n, dv)).astype(BF16))
                      for ea, s0, kt_b, w in zip(eas, s0s, kt_bs, ws)]
            if write_state:
                for h in hs:
                    outs[0][layer, i, h] = s_news[h]
        q_row = q_ref[pl.ds(b, 1), :]
        q8s = [jnp.broadcast_to(q_row[:, h * dk:(h + 1) * dk], (SUBLANES, dk)).astype(BF16) for h in hs]
        o_ref[i:i + 1, :] = jnp.concatenate(
            [_dot(q8, s_new.astype(BF16))[0:1] for q8, s_new in zip(q8s, s_news)], axis=1)


def _gdn_sample(x, g, cbuf, w_in, conv_w, a_log, dt_bias, norm_w, w_out, s0, layer, earlier):
    n, d = x.shape
    layers, _, heads, dk, dv = s0.shape
    taps = conv_w.shape[0]
    qkv_w = heads * (2 * dk + dv)
    hd = heads * dv
    block = math.gcd(GDN_STATE_BLOCK, n)
    cbuf_t = jnp.swapaxes(cbuf, 0, 1)
    q, kt, k, v, z, beta, ea, cout_t = pl.pallas_call(
        functools.partial(_gdn_sample_proj_kernel, heads=heads, dk=dk, dv=dv, taps=taps),
        name="gdn_sample_proj",
        out_shape=[
            jax.ShapeDtypeStruct((n, heads * dk), F32),
            jax.ShapeDtypeStruct((heads * dk, n), F32),
            jax.ShapeDtypeStruct((n, heads * dk), F32),
            jax.ShapeDtypeStruct((n, hd), F32),
            jax.ShapeDtypeStruct((n, hd), F32),
            jax.ShapeDtypeStruct((n, hd), F32),
            jax.ShapeDtypeStruct((n, hd), F32),
            jax.ShapeDtypeStruct((taps - 1, n, qkv_w), F32),
        ],
        compiler_params=_params(),
    )(x, g, cbuf_t, w_in, conv_w, a_log, dt_bias)
    last = layer == layers - 1
    own = (kt, k, v, beta, ea)
    handled = list(enumerate(earlier)) + [(layer, own)] if last else [(layer, own)]
    args, in_specs = [q], [_resident(q.shape)]
    for idx, small in handled:
        args += list(small) + [s0]
        in_specs += [_resident(t.shape) for t in small]
        in_specs.append(pl.BlockSpec((1, block, heads, dk, dv), lambda i, idx=idx: (idx, i, 0, 0, 0)))
    o_spec = pl.BlockSpec((block, hd), lambda i: (i, 0))
    o_shape = jax.ShapeDtypeStruct((n, hd), F32)
    outs = pl.pallas_call(
        functools.partial(_gdn_sample_state_kernel, heads=heads, dk=dk, dv=dv, block=block,
                          n_layers=len(handled), write_state=last),
        name="gdn_sample_state",
        grid=(n // block,),
        in_specs=in_specs,
        out_specs=[pl.BlockSpec((layers, block, heads, dk, dv), lambda i: (0, i, 0, 0, 0)), o_spec] if last
        else [o_spec],
        out_shape=[jax.ShapeDtypeStruct(s0.shape, F32), o_shape] if last else [o_shape],
        compiler_params=_params("arbitrary"),
    )(*args)
    y = _mixer_out(x, outs[-1], z, norm_w, w_out, heads, dv)
    return y, (outs[0] if last else None), jnp.swapaxes(cout_t, 0, 1), earlier + [own]


def _rope_tables(pos, dk):
    theta = 1.0 / (ROPE_BASE ** jnp.linspace(0.0, 1.0, dk // 2, dtype=F32))
    ang = pos[:, None] * theta[None, :]
    return jnp.cos(ang), jnp.sin(ang)


def _prep_ret_weights(w_in, heads, dk):
    qk_w = 2 * heads * dk
    d = w_in.shape[0]
    w = w_in.astype(BF16)
    pairs = w[:, :qk_w].reshape(d, 2 * heads, dk // 2, 2)
    w_deint = jnp.concatenate([jnp.swapaxes(pairs, 2, 3).reshape(d, qk_w), w[:, qk_w:]], axis=1)
    swapped = (pairs[..., ::-1] * jnp.array([-1.0, 1.0], BF16)).reshape(d, qk_w)
    return w, w_deint, swapped


def _prep_gdn_weights(w_in, a_log, dt_bias, heads, dk, dv):
    pad = LANES - 2 * heads
    w = jnp.pad(w_in, ((0, 0), (0, pad))).astype(BF16)
    place = lambda p: jnp.pad(p, (heads, pad))[None, :]
    return w, place(a_log), place(dt_bias)


def _trunk(x, batch, pos0, s_ret, s_gdn, s_conv, prompt, norm_g, ffn_gu, ffn_down, ffn_w, ret_w, ret_w_out, gdn_w,
           gdn_conv_w, gdn_norm_w, gdn_w_out, final_g):
    depth = norm_g.shape[0]
    n_mixers = 2
    m, d = x.shape
    seq = m // batch
    n_ret, _, ret_heads, ret_dk, ret_dv = s_ret.shape
    n_gdn, _, gdn_heads, gdn_dk, gdn_dv = s_gdn.shape
    cos, sin = _rope_tables(jnp.arange(seq, dtype=F32) + pos0, ret_dk)
    if not prompt:
        cos = jnp.tile(jnp.repeat(cos, 2, axis=1), (1, 2 * ret_heads))
        sin = jnp.tile(jnp.repeat(sin, 2, axis=1), (1, 2 * ret_heads))
    ones_dv = jnp.ones((1, ret_dv), F32)
    fg = final_g[None, :]
    norm_g4 = norm_g[:, :, None, :]
    new_ret, new_gdn, new_conv = [], [], []
    ret_hist, gdn_hist = [], []

    def ffn(x, layer, which, which_norm, final):
        nxt = (layer, 1) if which == 0 else (layer + 1, 0)
        cast_next = (ffn_gu, ffn_down) + nxt if prompt and nxt[0] < depth else None
        x, made = _ffn(x, norm_g4, *ffn_w[(layer, which)], fg, layer, which_norm, final, cast_next)
        if made:
            ffn_w[nxt] = made
        return x

    for i in range(depth):
        x = ffn(x, i, 0, 0, False)
        j = i // n_mixers
        g = norm_g[i, 1][None, :]
        if i % n_mixers == 0:
            w_orig, w_deint, w_swap = ret_w[j]
            if prompt:
                x, s = _ret_prompt(x, g, cos, sin, w_deint, ret_w_out[j], batch, ret_heads, ret_dk, ret_dv)
                new_ret.append(s)
            else:
                x, new_ret, ret_hist = _ret_sample(x, g, cos, sin, w_orig, w_swap, ret_w_out[j], s_ret, ones_dv,
                                                   j, ret_hist)
        else:
            w, a_log, dt_bias = gdn_w[j]
            nw = gdn_norm_w[j][None, :]
            if prompt:
                x, s, buf = _gdn_prompt(x, g, s_conv[j], w, gdn_conv_w[j], a_log, dt_bias, nw, gdn_w_out[j],
                                        batch, gdn_heads, gdn_dk, gdn_dv)
                new_gdn.append(s)
            else:
                x, new_gdn, buf, gdn_hist = _gdn_sample(x, g, s_conv[j], w, gdn_conv_w[j], a_log, dt_bias, nw,
                                                        gdn_w_out[j], s_gdn, j, gdn_hist)
            new_conv.append(buf)
        x = ffn(x, i, 1, 2, i == depth - 1)
    if prompt:
        new_ret, new_gdn = jnp.stack(new_ret), jnp.stack(new_gdn)
    return x, new_ret, new_gdn, jnp.stack(new_conv)


def kernel(x_prompt, x_sample, state_ret, state_gdn, state_gdn_conv, norm_g, ffn_gu, ffn_down, ret_w_in,
           ret_w_out, gdn_w_in, gdn_conv_w, gdn_a_log, gdn_dt_bias, gdn_norm_w, gdn_w_out, final_g):
    bp, seq, d = x_prompt.shape
    bs, dec_seq, _ = x_sample.shape
    assert dec_seq == 1
    n_ret, _, ret_heads, ret_dk, ret_dv = state_ret.shape
    n_gdn, _, gdn_heads, gdn_dk, gdn_dv = state_gdn.shape

    ffn_w = {(0, 0): (ffn_gu[0, 0].astype(BF16), ffn_down[0, 0].astype(BF16))}
    ret_w = [_prep_ret_weights(ret_w_in[j], ret_heads, ret_dk) for j in range(n_ret)]
    ret_w_out_b = ret_w_out.astype(BF16)
    gdn_w = [_prep_gdn_weights(gdn_w_in[j], gdn_a_log[j], gdn_dt_bias[j], gdn_heads, gdn_dk, gdn_dv)
             for j in range(n_gdn)]
    gdn_w_out_b = gdn_w_out.astype(BF16)
    shared = (norm_g, ffn_gu, ffn_down, ffn_w, ret_w, ret_w_out_b, gdn_w, gdn_conv_w, gdn_norm_w, gdn_w_out_b, final_g)

    zero_ret = jax.ShapeDtypeStruct((n_ret, bp, ret_heads, ret_dk, ret_dv), F32)
    zero_gdn = jax.ShapeDtypeStruct((n_gdn, bp, gdn_heads, gdn_dk, gdn_dv), F32)
    zero_conv = jnp.zeros((n_gdn, bp) + state_gdn_conv.shape[2:], state_gdn_conv.dtype)
    y_p, ret_p, gdn_p, conv_p = _trunk(x_prompt.reshape(bp * seq, d), bp, 0.0, zero_ret, zero_gdn, zero_conv,
                                       True, *shared)
    y_s, ret_s, gdn_s, conv_s = _trunk(x_sample.reshape(bs, d), bs, SAMPLE_PAST_LEN, state_ret, state_gdn,
                                       state_gdn_conv, False, *shared)
    return (y_p.reshape(bp, seq, d), y_s.reshape(bs, dec_seq, d), ret_p, gdn_p, conv_p, ret_s, gdn_s, conv_s)
```

```python
import functools
import math

import jax
import jax.numpy as jnp
from jax import lax
from jax.experimental import pallas as pl
from jax.experimental.pallas import tpu as pltpu

F32 = jnp.float32
BF16 = jnp.bfloat16
EPS = 1e-6
ROPE_BASE = 10000.0
SAMPLE_PAST_LEN = 16384.0

LANES = 128
SUBLANES = 8
VMEM_LIMIT_BYTES = 56 * 1024 * 1024

FFN_ROWS = 512
RET_CHUNK = 256
GDN_CHUNK = 128
GDN_ROWS = 256
RET_STATE_BLOCK = 2
GDN_STATE_BLOCK = 8


def _dot(a, b):
    return jnp.dot(a, b, preferred_element_type=F32)


def _dot_nt(a, b):
    return lax.dot_general(a, b, (((1,), (1,)), ((), ())), preferred_element_type=F32)


def _rms(x, g):
    return x * lax.rsqrt(jnp.mean(x * x, axis=-1, keepdims=True) + EPS) * g


def _silu(x):
    h = 0.5 * x
    return h * jnp.tanh(h) + h


def _softplus(x):
    return jnp.maximum(x, 0.0) + jnp.log1p(jnp.exp(-jnp.abs(x)))


def _bf16_terms(x):
    t0 = x.astype(BF16)
    r = x - t0.astype(F32)
    t1 = r.astype(BF16)
    t2 = (r - t1.astype(F32)).astype(BF16)
    return t0, t1, t2


def _exact_dot_right(x, sel):
    return _dot(jnp.concatenate(_bf16_terms(x), axis=1), jnp.concatenate([sel] * 3, axis=0))


def _exact_dot_left(sel, x):
    return _dot(jnp.concatenate([sel] * 3, axis=1), jnp.concatenate(_bf16_terms(x), axis=0))


def _expand_heads(x, first, heads, width):
    c = lax.broadcasted_iota(jnp.int32, (x.shape[1], heads * width), 0)
    lane = lax.broadcasted_iota(jnp.int32, (x.shape[1], heads * width), 1)
    e = ((lane >= (c - first) * width) & (lane < (c - first + 1) * width)).astype(BF16)
    return _exact_dot_right(x, e)


def _dots(lhs, rhs, nt=False):
    dot = _dot_nt if nt else _dot
    return [dot(a, b) for a, b in zip(lhs, rhs)]


def _resident(shape):
    nd = len(shape)
    return pl.BlockSpec(shape, lambda *_: (0,) * nd, pipeline_mode=pl.Buffered(1))


def _params(*sem):
    return pltpu.CompilerParams(dimension_semantics=sem, vmem_limit_bytes=VMEM_LIMIT_BYTES)


def _ffn_kernel(x_ref, small_ref, g_ref, wg_ref, wu_ref, wd_ref, fg_ref, *rest, final, cast_next):
    def ffn(x):
        h = _rms(x, g_ref[...]).astype(BF16)
        a = _dot(h, wg_ref[...])
        b = _dot(h, wu_ref[...])
        act = (_silu(a) * b).astype(BF16)
        y = x + 0.5 * _dot(act, wd_ref[...])
        return _rms(y, fg_ref[...]) if final else y

    o_ref, small_out_ref = rest[-4:-2] if cast_next else rest
    o_ref[...] = ffn(x_ref[...])

    @pl.when(pl.program_id(0) == pl.num_programs(0) - 1)
    def _():
        small_out_ref[...] = ffn(small_ref[...])

    if cast_next:
        next_gu_ref, next_down_ref = rest[:2]
        gu_out_ref, down_out_ref = rest[-2:]
        gu_out_ref[...] = next_gu_ref[...].astype(BF16)
        down_out_ref[...] = next_down_ref[...].astype(BF16)


def _slab_count(rows, steps):
    return max(n for n in range(1, steps + 1) if steps % n == 0 and rows % (16 * n) == 0)


def _ffn(x, small, norm_g, w_gu, w_down, final_g, layer, which_norm, final, cast_next=None):
    m, d = x.shape
    f = w_down.shape[0]
    tm = min(FFN_ROWS, m)
    steps = m // tm
    assert m % tm == 0 and f % LANES == 0
    in_specs = [
        pl.BlockSpec((tm, d), lambda i: (i, 0)),
        _resident(small.shape),
        pl.BlockSpec((None, None, 1, d), lambda i: (layer, which_norm, 0, 0), pipeline_mode=pl.Buffered(1)),
        pl.BlockSpec((d, f), lambda i: (0, 0), pipeline_mode=pl.Buffered(1)),
        pl.BlockSpec((d, f), lambda i: (0, 1), pipeline_mode=pl.Buffered(1)),
        _resident((f, d)),
        _resident((1, d)),
    ]
    args = [x, small, norm_g, w_gu, w_gu, w_down, final_g]
    out_specs = [pl.BlockSpec((tm, d), lambda i: (i, 0)), pl.BlockSpec(small.shape, lambda i: (0, 0))]
    out_shape = [jax.ShapeDtypeStruct((m, d), F32), jax.ShapeDtypeStruct(small.shape, F32)]
    if cast_next is not None:
        next_gu, next_down, nl, nw = cast_next
        for w, rows, cols in ((next_gu, d, 2 * f), (next_down, f, d)):
            n = _slab_count(rows, steps)
            in_specs.append(pl.BlockSpec((None, None, rows // n, cols),
                                         lambda i, n=n: (nl, nw, i * n // steps, 0)))
            out_specs.append(pl.BlockSpec((rows // n, cols), lambda i, n=n: (i * n // steps, 0)))
            out_shape.append(jax.ShapeDtypeStruct((rows, cols), BF16))
            args.append(w)
    outs = pl.pallas_call(
        functools.partial(_ffn_kernel, final=final, cast_next=cast_next is not None),
        name="ffn",
        grid=(steps,),
        in_specs=in_specs,
        out_specs=out_specs,
        out_shape=out_shape,
        compiler_params=_params("arbitrary"),
    )(*args)
    return outs[0], outs[1], tuple(outs[2:])


def _ret_prompt_kernel(x_ref, g_ref, cos_ref, sin_ref, win_ref, wout_ref, y_ref, sout_ref, s_scr, og_scr,
                       *, heads, dk, dv, chunk):
    t = pl.program_id(1)
    half = dk // 2

    @pl.when(t == 0)
    def _():
        s_scr[...] = jnp.zeros_like(s_scr)

    x = x_ref[...]
    hn = _rms(x, g_ref[...]).astype(BF16)
    cos = cos_ref[...]
    sin = sin_ref[...]
    ri = lax.broadcasted_iota(jnp.int32, (chunk, chunk), 0)
    ci = lax.broadcasted_iota(jnp.int32, (chunk, chunk), 1)
    lag = (ri - ci).astype(F32)
    row = lax.broadcasted_iota(jnp.int32, (chunk, half), 0).astype(F32)
    k_off, v_off, g_off = heads * dk, 2 * heads * dk, 2 * heads * dk + heads * dv
    for h in range(heads):
        log_gamma = math.log(1.0 - 2.0 ** (-5.0 - h))
        q = _dot(hn, win_ref[:, h * dk:(h + 1) * dk])
        k = _dot(hn, win_ref[:, k_off + h * dk:k_off + (h + 1) * dk])
        v = _dot(hn, win_ref[:, v_off + h * dv:v_off + (h + 1) * dv]).astype(BF16)
        gate = _dot(hn, win_ref[:, g_off + h * dv:g_off + (h + 1) * dv])
        q1, q2 = q[:, :half], q[:, half:]
        k1, k2 = k[:, :half], k[:, half:]
        qr1, qr2 = q1 * cos - q2 * sin, q2 * cos + q1 * sin
        kr1, kr2 = (k1 * cos - k2 * sin) * dk ** -0.5, (k2 * cos + k1 * sin) * dk ** -0.5
        qb = jnp.concatenate([qr1, qr2], axis=1).astype(BF16)
        kb = jnp.concatenate([kr1, kr2], axis=1).astype(BF16)
        decay = jnp.where(ri >= ci, jnp.exp(lag * log_gamma), 0.0)
        p = (_dot_nt(qb, kb) * decay).astype(BF16)
        q_scale = jnp.exp((row + 1.0) * log_gamma)
        k_scale = jnp.exp((chunk - 1.0 - row) * log_gamma)
        qd = jnp.concatenate([qr1 * q_scale, qr2 * q_scale], axis=1).astype(BF16)
        kd = jnp.concatenate([kr1 * k_scale, kr2 * k_scale], axis=1)
        s = s_scr[h]
        o = _dot(p, v) + _dot(qd, s.astype(BF16))
        s_scr[h] = math.exp(chunk * log_gamma) * s + _dot(kd.T.astype(BF16), v)
        o = o * lax.rsqrt(jnp.mean(o * o, axis=-1, keepdims=True) + EPS) * _silu(gate)
        og_scr[:, h * dv:(h + 1) * dv] = o.astype(BF16)
    y_ref[...] = x + _dot(og_scr[...], wout_ref[...])

    @pl.when(t == pl.num_programs(1) - 1)
    def _():
        ro = lax.broadcasted_iota(jnp.int32, (dk, dk), 0)
        rin = lax.broadcasted_iota(jnp.int32, (dk, dk), 1)
        perm = (rin == (ro >> 1) + (ro & 1) * half).astype(BF16)
        for h in range(heads):
            sout_ref[0, h] = _exact_dot_left(perm, s_scr[h])


def _ret_prompt(x, g, cos, sin, w_in, w_out, batch, heads, dk, dv):
    m, d = x.shape
    seq = m // batch
    chunk = min(RET_CHUNK, seq)
    assert seq % chunk == 0
    nt = seq // chunk
    return pl.pallas_call(
        functools.partial(_ret_prompt_kernel, heads=heads, dk=dk, dv=dv, chunk=chunk),
        name="ret_prompt",
        grid=(batch, nt),
        in_specs=[
            pl.BlockSpec((chunk, d), lambda b, t: (b * nt + t, 0)),
            _resident((1, d)),
            pl.BlockSpec((chunk, dk // 2), lambda b, t: (t, 0)),
            pl.BlockSpec((chunk, dk // 2), lambda b, t: (t, 0)),
            _resident(w_in.shape),
            _resident(w_out.shape),
        ],
        out_specs=[
            pl.BlockSpec((chunk, d), lambda b, t: (b * nt + t, 0)),
            pl.BlockSpec((1, heads, dk, dv), lambda b, t: (b, 0, 0, 0)),
        ],
        out_shape=[
            jax.ShapeDtypeStruct((m, d), F32),
            jax.ShapeDtypeStruct((batch, heads, dk, dv), F32),
        ],
        scratch_shapes=[pltpu.VMEM((heads, dk, dv), F32), pltpu.VMEM((chunk, heads * dv), BF16)],
        compiler_params=_params("arbitrary", "arbitrary"),
    )(x, g, cos, sin, w_in, w_out)


def _ret_sample_proj_kernel(x_ref, g_ref, cos_ref, sin_ref, win_ref, wsw_ref, q_ref, kt_ref, v_ref, gate_ref,
                            *, heads, dk, dv):
    qk_w = 2 * heads * dk
    hn = _rms(x_ref[...], g_ref[...]).astype(BF16)
    qk = _dot(hn, win_ref[:, :qk_w]) * cos_ref[...] + _dot(hn, wsw_ref[...]) * sin_ref[...]
    q_ref[...] = qk[:, :heads * dk]
    kt_ref[...] = (qk[:, heads * dk:] * dk ** -0.5).T
    v_ref[...] = _dot(hn, win_ref[:, qk_w:qk_w + heads * dv])
    gate_ref[...] = _dot(hn, win_ref[:, qk_w + heads * dv:])


def _ret_sample_state_kernel(q_ref, *refs, heads, dk, dv, block, n_layers, write_state):
    ins, outs = refs[:3 * n_layers], refs[3 * n_layers:]
    o_ref = outs[-1]
    n = q_ref.shape[0]
    lane = lax.broadcasted_iota(jnp.int32, (dk, n), 1)
    hs = range(heads)
    gammas = [1.0 - 2.0 ** (-5.0 - h) for h in hs]
    for i in range(block):
        b = pl.program_id(0) * block + i
        for layer in range(n_layers):
            kt_ref, v_ref, s0_ref = ins[3 * layer:3 * layer + 3]
            kt_bs = [jnp.where(lane == b, kt_ref[h * dk:(h + 1) * dk, :], 0.0).astype(BF16) for h in hs]
            s_news = [gamma * s0_ref[0, i, h] + _dot(kt_b, v_ref[:, h * dv:(h + 1) * dv].astype(BF16))
                      for h, gamma, kt_b in zip(hs, gammas, kt_bs)]
            if write_state:
                for h in hs:
                    outs[0][layer, i, h] = s_news[h]
        q_row = q_ref[pl.ds(b, 1), :]
        q8s = [jnp.broadcast_to(q_row[:, h * dk:(h + 1) * dk], (SUBLANES, dk)).astype(BF16) for h in hs]
        o_ref[0, i:i + 1, :] = jnp.concatenate(
            [_dot(q8, s_new.astype(BF16))[0:1] for q8, s_new in zip(q8s, s_news)], axis=1)


def _mixer_out_kernel(x_ref, o_ref, gate_ref, nw_ref, wout_ref, y_ref, *, heads, dv):
    parts = []
    for h in range(heads):
        o = o_ref[:, h * dv:(h + 1) * dv]
        o = o * lax.rsqrt(jnp.mean(o * o, axis=-1, keepdims=True) + EPS) * nw_ref[...]
        parts.append((o * _silu(gate_ref[:, h * dv:(h + 1) * dv])).astype(BF16))
    y_ref[...] = x_ref[...] + _dot(jnp.concatenate(parts, axis=1), wout_ref[...])


def _mixer_out(x, o, gate, norm_w, w_out, heads, dv):
    return pl.pallas_call(
        functools.partial(_mixer_out_kernel, heads=heads, dv=dv),
        name="mixer_out",
        out_shape=jax.ShapeDtypeStruct(x.shape, F32),
        compiler_params=_params(),
    )(x, o, gate, norm_w, w_out)


def _ret_sample(x, g, cos, sin, w_in, w_sw, w_out, s0, ones_dv, layer, earlier):
    n, d = x.shape
    layers, _, heads, dk, dv = s0.shape
    last = layer == layers - 1
    block = math.gcd(RET_STATE_BLOCK if last else 2 * RET_STATE_BLOCK, n)
    q, kt, v, gate = pl.pallas_call(
        functools.partial(_ret_sample_proj_kernel, heads=heads, dk=dk, dv=dv),
        name="ret_sample_proj",
        out_shape=[
            jax.ShapeDtypeStruct((n, heads * dk), F32),
            jax.ShapeDtypeStruct((heads * dk, n), F32),
            jax.ShapeDtypeStruct((n, heads * dv), F32),
            jax.ShapeDtypeStruct((n, heads * dv), F32),
        ],
        compiler_params=_params(),
    )(x, g, cos, sin, w_in, w_sw)
    handled = list(enumerate(earlier)) + [(layer, (kt, v))] if last else [(layer, (kt, v))]
    args, in_specs = [q], [_resident(q.shape)]
    for idx, (kt_l, v_l) in handled:
        args += [kt_l, v_l, s0]
        in_specs += [_resident(kt_l.shape), _resident(v_l.shape),
                     pl.BlockSpec((1, block, heads, dk, dv), lambda i, idx=idx: (idx, i, 0, 0, 0))]
    o_spec = pl.BlockSpec((1, block, heads * dv), lambda i: (i, 0, 0))
    o_shape = jax.ShapeDtypeStruct((n // block, block, heads * dv), F32)
    outs = pl.pallas_call(
        functools.partial(_ret_sample_state_kernel, heads=heads, dk=dk, dv=dv, block=block,
                          n_layers=len(handled), write_state=last),
        name="ret_sample_state",
        grid=(n // block,),
        in_specs=in_specs,
        out_specs=[pl.BlockSpec((layers, block, heads, dk, dv), lambda i: (0, i, 0, 0, 0)), o_spec] if last
        else [o_spec],
        out_shape=[jax.ShapeDtypeStruct(s0.shape, F32), o_shape] if last else [o_shape],
        compiler_params=_params("arbitrary"),
    )(*args)
    y = _mixer_out(x, outs[-1].reshape(n, heads * dv), gate, ones_dv, w_out, heads, dv)
    return y, (outs[0] if last else None), earlier + [(kt, v)]


def _unit_lower_inverse_minus_identity(lms, ri, ci, chunk):
    def level_mask(lev):
        return ((ri >> lev) == (ci >> lev)) & ((ri >> (lev - 1)) != (ci >> (lev - 1)))

    mask = level_mask(1)
    ns = [jnp.where(mask, -lm, 0.0) for lm in lms]
    for lev in range(2, chunk.bit_length()):
        mask = level_mask(lev)
        cs = [jnp.where(mask, lm, 0.0) for lm in lms]
        nbs = [n.astype(BF16) for n in ns]
        gs = [c + cn for c, cn in zip(cs, _dots([c.astype(BF16) for c in cs], nbs))]
        ns = [n - g - ng for n, g, ng in zip(ns, gs, _dots(nbs, [g.astype(BF16) for g in gs]))]
    return ns


def _gdn_prompt_kernel(x_ref, g_ref, cbuf_ref, win_ref, cw_ref, alog_ref, dtb_ref, nw_ref, wout_ref,
                       y_ref, sout_ref, cout_ref, s_scr, xh_scr, uh_scr, og_scr, *, heads, dk, dv, rows, chunk):
    t = pl.program_id(1)
    qkv_w = heads * (2 * dk + dv)
    hd = heads * dv
    cw = [cw_ref[j:j + 1, :] for j in range(4)]

    @pl.when(t == 0)
    def _():
        s_scr[...] = jnp.zeros_like(s_scr)
        xh_scr[...] = jnp.zeros_like(xh_scr)
        uh_scr[...] = jnp.zeros_like(uh_scr)
        xh_scr[0:1, :] = cbuf_ref[0, 2:3, :]
        uh_scr[0:1, :] = cw[1] * cbuf_ref[0, 1:2, :] + cw[0] * cbuf_ref[0, 0:1, :]
        uh_scr[1:2, :] = cw[1] * cbuf_ref[0, 2:3, :] + cw[0] * cbuf_ref[0, 1:2, :]

    x = x_ref[...]
    hn = _rms(x, g_ref[...]).astype(BF16)
    qkv = _dot(hn, win_ref[:, :qkv_w])
    sub = lax.broadcasted_iota(jnp.int32, (SUBLANES, qkv_w), 0)

    def shifted(cur, by, halo_tile):
        rolled = pltpu.roll(cur, by, 0)
        first = jnp.where(sub < by, halo_tile, rolled[:SUBLANES])
        return jnp.concatenate([first, rolled[SUBLANES:]], axis=0)

    x_prev = shifted(qkv, 1, xh_scr[...])
    u = cw[1] * qkv + cw[0] * x_prev
    conv = cw[3] * qkv + cw[2] * x_prev + shifted(u, 2, uh_scr[...])
    xh_scr[0:1, :] = qkv[rows - 1:rows, :]
    uh_scr[0:2, :] = u[rows - 2:rows, :]

    @pl.when(t == pl.num_programs(1) - 1)
    def _():
        cout_ref[0] = qkv[rows - 3:rows, :]

    act = _silu(conv)
    z = _dot(hn, win_ref[:, qkv_w:qkv_w + hd])
    ba = _dot(hn, win_ref[:, qkv_w + hd:])
    a = -jnp.exp(alog_ref[...]) * _softplus(ba + dtb_ref[...])
    ri = lax.broadcasted_iota(jnp.int32, (chunk, chunk), 0)
    ci = lax.broadcasted_iota(jnp.int32, (chunk, chunk), 1)
    tril = (ri >= ci).astype(BF16)
    subs = range(rows // chunk)
    bcum = jnp.concatenate([_exact_dot_left(tril, a[c * chunk:(c + 1) * chunk]) for c in subs], axis=0)
    beta = _expand_heads(jax.nn.sigmoid(ba), 0, heads, dv)
    bcum = _expand_heads(bcum, heads, heads, dv)
    probs = [(c, h) for c in subs for h in range(heads)]

    def tile(arr, c, col, width):
        return arr[c * chunk:(c + 1) * chunk, col:col + width]

    qs = [tile(act, c, h * dk, dk) for c, h in probs]
    ks = [tile(act, c, heads * dk + h * dk, dk) for c, h in probs]
    vs = [tile(act, c, 2 * heads * dk + h * dv, dv) for c, h in probs]
    qs = [q * lax.rsqrt(jnp.sum(q * q, axis=-1, keepdims=True) + EPS) * dk ** -0.5 for q in qs]
    ks = [k * lax.rsqrt(jnp.sum(k * k, axis=-1, keepdims=True) + EPS) for k in ks]
    bts = [tile(beta, c, h * dv, dv) for c, h in probs]
    bs = [tile(bcum, c, h * dv, dv) for c, h in probs]
    es = [jnp.exp(jnp.minimum(b - b.T, 0.0)) for b in bs]
    kbs = [k.astype(BF16) for k in ks]
    qk_kks = _dots([jnp.concatenate([q.astype(BF16), kb], axis=0) for q, kb in zip(qs, kbs)], kbs, nt=True)
    ps = [jnp.where(ri >= ci, qk_kk[:chunk] * e, 0.0).astype(BF16) for qk_kk, e in zip(qk_kks, es)]
    lms = [jnp.where(ri > ci, qk_kk[chunk:] * e, 0.0) * bt for qk_kk, e, bt in zip(qk_kks, es, bts)]
    ns = [n.astype(BF16) for n in _unit_lower_inverse_minus_identity(lms, ri, ci, chunk)]
    kq_es = [jnp.concatenate([(k * jnp.exp(b)).astype(BF16), (q * jnp.exp(b)).astype(BF16)], axis=0)
             for k, q, b in zip(ks, qs, bs)]
    b_lasts = [b[chunk - 1:chunk, :] for b in bs]
    kdts = [(k * jnp.exp(b_last - b)).T.astype(BF16) for k, b, b_last in zip(ks, bs, b_lasts)]
    ss = [s_scr[h] for h in range(heads)]
    for c in subs:
        sel = slice(c * heads, (c + 1) * heads)
        kq_ss = _dots(kq_es[sel], [s.astype(BF16) for s in ss])
        rs = [bt * (v - kq_s[:chunk]) for bt, v, kq_s in zip(bts[sel], vs[sel], kq_ss)]
        wbs = [(r + nr).astype(BF16) for r, nr in zip(rs, _dots(ns[sel], [r.astype(BF16) for r in rs]))]
        os_ = [kq_s[chunk:] + pw for kq_s, pw in zip(kq_ss, _dots(ps[sel], wbs))]
        ss = [jnp.exp(b_last) * s + kw for b_last, s, kw in zip(b_lasts[sel], ss, _dots(kdts[sel], wbs))]
        for h in range(heads):
            o = os_[h]
            o = o * lax.rsqrt(jnp.mean(o * o, axis=-1, keepdims=True) + EPS) * nw_ref[...]
            og_scr[c * chunk:(c + 1) * chunk, h * dv:(h + 1) * dv] = (o * _silu(tile(z, c, h * dv, dv))).astype(BF16)
    for h in range(heads):
        s_scr[h] = ss[h]
    y_ref[...] = x + _dot(og_scr[...], wout_ref[...])

    @pl.when(t == pl.num_programs(1) - 1)
    def _():
        sout_ref[0] = s_scr[...]


def _gdn_prompt(x, g, cbuf, w_in, conv_w, a_log, dt_bias, norm_w, w_out, batch, heads, dk, dv):
    m, d = x.shape
    seq = m // batch
    chunk = GDN_CHUNK
    taps = conv_w.shape[0]
    qkv_w = heads * (2 * dk + dv)
    rows = math.gcd(GDN_ROWS, seq)
    assert rows % chunk == 0 and dk == chunk and dv == chunk and chunk == LANES and 2 * heads <= LANES and taps == 4
    nt = seq // rows
    return pl.pallas_call(
        functools.partial(_gdn_prompt_kernel, heads=heads, dk=dk, dv=dv, rows=rows, chunk=chunk),
        name="gdn_prompt",
        grid=(batch, nt),
        in_specs=[
            pl.BlockSpec((rows, d), lambda b, t: (b * nt + t, 0)),
            _resident((1, d)),
            pl.BlockSpec((1, taps - 1, qkv_w), lambda b, t: (b, 0, 0)),
            _resident(w_in.shape),
            _resident(conv_w.shape),
            _resident(a_log.shape),
            _resident(dt_bias.shape),
            _resident(norm_w.shape),
            _resident(w_out.shape),
        ],
        out_specs=[
            pl.BlockSpec((rows, d), lambda b, t: (b * nt + t, 0)),
            pl.BlockSpec((1, heads, dk, dv), lambda b, t: (b, 0, 0, 0)),
            pl.BlockSpec((1, taps - 1, qkv_w), lambda b, t: (b, 0, 0)),
        ],
        out_shape=[
            jax.ShapeDtypeStruct((m, d), F32),
            jax.ShapeDtypeStruct((batch, heads, dk, dv), F32),
            jax.ShapeDtypeStruct((batch, taps - 1, qkv_w), F32),
        ],
        scratch_shapes=[
            pltpu.VMEM((heads, dk, dv), F32),
            pltpu.VMEM((SUBLANES, qkv_w), F32),
            pltpu.VMEM((SUBLANES, qkv_w), F32),
            pltpu.VMEM((rows, heads * dv), BF16),
        ],
        compiler_params=_params("arbitrary", "arbitrary"),
    )(x, g, cbuf, w_in, conv_w, a_log, dt_bias, norm_w, w_out)


def _gdn_sample_proj_kernel(x_ref, g_ref, cbuf_ref, win_ref, cw_ref, alog_ref, dtb_ref,
                            q_ref, kt_ref, k_ref, v_ref, z_ref, beta_ref, ea_ref, cout_ref, *, heads, dk, dv, taps):
    qkv_w = heads * (2 * dk + dv)
    hd = heads * dv
    halo = taps - 1
    hn = _rms(x_ref[...], g_ref[...]).astype(BF16)
    qkv = _dot(hn, win_ref[:, :qkv_w])
    conv = qkv * cw_ref[halo:taps, :]
    for j in range(halo):
        conv = conv + cbuf_ref[j] * cw_ref[j:j + 1, :]
        if j > 0:
            cout_ref[j - 1] = cbuf_ref[j]
    cout_ref[halo - 1] = qkv
    act = _silu(conv)
    for h in range(heads):
        q = act[:, h * dk:(h + 1) * dk]
        k = act[:, heads * dk + h * dk:heads * dk + (h + 1) * dk]
        q_ref[:, h * dk:(h + 1) * dk] = q * lax.rsqrt(jnp.sum(q * q, axis=-1, keepdims=True) + EPS) * dk ** -0.5
        k_ref[:, h * dk:(h + 1) * dk] = k * lax.rsqrt(jnp.sum(k * k, axis=-1, keepdims=True) + EPS)
    kt_ref[...] = k_ref[...].T
    v_ref[...] = act[:, 2 * heads * dk:]
    z_ref[...] = _dot(hn, win_ref[:, qkv_w:qkv_w + hd])
    ba = _dot(hn, win_ref[:, qkv_w + hd:])
    beta_ref[...] = _expand_heads(jax.nn.sigmoid(ba), 0, heads, dv)
    a = -jnp.exp(alog_ref[...]) * _softplus(ba + dtb_ref[...])
    ea_ref[...] = jnp.exp(_expand_heads(a, heads, heads, dv))


def _gdn_sample_state_kernel(q_ref, *refs, heads, dk, dv, block, n_layers, write_state):
    ins, outs = refs[:6 * n_layers], refs[6 * n_layers:]
    o_ref = outs[-1]
    n = q_ref.shape[0]
    lane = lax.broadcasted_iota(jnp.int32, (dk, n), 1)
    hs = range(heads)
    for i in range(block):
        b = pl.program_id(0) * block + i
        for layer in range(n_layers):
            kt_ref, k_ref, v_ref, beta_ref, ea_ref, s0_ref = ins[6 * layer:6 * layer + 6]
            k_row, v_row = k_ref[pl.ds(b, 1), :], v_ref[pl.ds(b, 1), :]
            beta_row, ea_row = beta_ref[pl.ds(b, 1), :], ea_ref[pl.ds(b, 1), :]
            s0s = [s0_ref[0, i, h] for h in hs]
            eas = [ea_row[:, h * dv:(h + 1) * dv] for h in hs]
            k8s = [jnp.broadcast_to(k_row[:, h * dk:(h + 1) * dk] * ea, (SUBLANES, dk)).astype(BF16)
                   for h, ea in zip(hs, eas)]
            kss = [_dot(k8, s0.astype(BF16))[0:1] for k8, s0 in zip(k8s, s0s)]
            ws = [beta_row[:, h * dv:(h + 1) * dv] * (v_row[:, h * dv:(h + 1) * dv] - ks) for h, ks in zip(hs, kss)]
            kt_bs = [jnp.where(lane == b, kt_ref[h * dk:(h + 1) * dk, :], 0.0).astype(BF16) for h in hs]
            s_news = [ea * s0 + _dot(kt_b, jnp.broadcast_to(w, (n, dv)).astype(BF16))
                      for ea, s0, kt_b, w in zip(eas, s0s, kt_bs, ws)]
            if write_state:
                for h in hs:
                    outs[0][layer, i, h] = s_news[h]
        q_row = q_ref[pl.ds(b, 1), :]
        q8s = [jnp.broadcast_to(q_row[:, h * dk:(h + 1) * dk], (SUBLANES, dk)).astype(BF16) for h in hs]
        o_ref[i:i + 1, :] = jnp.concatenate(
            [_dot(q8, s_new.astype(BF16))[0:1] for q8, s_new in zip(q8s, s_news)], axis=1)


def _gdn_sample(x, g, cbuf, w_in, conv_w, a_log, dt_bias, norm_w, w_out, s0, layer, earlier):
    n, d = x.shape
    layers, _, heads, dk, dv = s0.shape
    taps = conv_w.shape[0]
    qkv_w = heads * (2 * dk + dv)
    hd = heads * dv
    block = math.gcd(GDN_STATE_BLOCK, n)
    cbuf_t = jnp.swapaxes(cbuf, 0, 1)
    q, kt, k, v, z, beta, ea, cout_t = pl.pallas_call(
        functools.partial(_gdn_sample_proj_kernel, heads=heads, dk=dk, dv=dv, taps=taps),
        name="gdn_sample_proj",
        out_shape=[
            jax.ShapeDtypeStruct((n, heads * dk), F32),
            jax.ShapeDtypeStruct((heads * dk, n), F32),
            jax.ShapeDtypeStruct((n, heads * dk), F32),
            jax.ShapeDtypeStruct((n, hd), F32),
            jax.ShapeDtypeStruct((n, hd), F32),
            jax.ShapeDtypeStruct((n, hd), F32),
            jax.ShapeDtypeStruct((n, hd), F32),
            jax.ShapeDtypeStruct((taps - 1, n, qkv_w), F32),
        ],
        compiler_params=_params(),
    )(x, g, cbuf_t, w_in, conv_w, a_log, dt_bias)
    last = layer == layers - 1
    own = (kt, k, v, beta, ea)
    handled = list(enumerate(earlier)) + [(layer, own)] if last else [(layer, own)]
    args, in_specs = [q], [_resident(q.shape)]
    for idx, small in handled:
        args += list(small) + [s0]
        in_specs += [_resident(t.shape) for t in small]
        in_specs.append(pl.BlockSpec((1, block, heads, dk, dv), lambda i, idx=idx: (idx, i, 0, 0, 0)))
    o_spec = pl.BlockSpec((block, hd), lambda i: (i, 0))
    o_shape = jax.ShapeDtypeStruct((n, hd), F32)
    outs = pl.pallas_call(
        functools.partial(_gdn_sample_state_kernel, heads=heads, dk=dk, dv=dv, block=block,
                          n_layers=len(handled), write_state=last),
        name="gdn_sample_state",
        grid=(n // block,),
        in_specs=in_specs,
        out_specs=[pl.BlockSpec((layers, block, heads, dk, dv), lambda i: (0, i, 0, 0, 0)), o_spec] if last
        else [o_spec],
        out_shape=[jax.ShapeDtypeStruct(s0.shape, F32), o_shape] if last else [o_shape],
        compiler_params=_params("arbitrary"),
    )(*args)
    y = _mixer_out(x, outs[-1], z, norm_w, w_out, heads, dv)
    return y, (outs[0] if last else None), jnp.swapaxes(cout_t, 0, 1), earlier + [own]


def _rope_tables(pos, dk):
    theta = 1.0 / (ROPE_BASE ** jnp.linspace(0.0, 1.0, dk // 2, dtype=F32))
    ang = pos[:, None] * theta[None, :]
    return jnp.cos(ang), jnp.sin(ang)


def _ret_weight_prep_kernel(w_ref, w_out_ref, deint_ref, swap_ref, *, heads, dk):
    qk_w = 2 * heads * dk
    half = dk // 2
    w = w_ref[...].astype(BF16)
    w_out_ref[...] = w
    deint_ref[:, qk_w:] = w[:, qk_w:]
    src = lax.broadcasted_iota(jnp.int32, (dk, dk), 0)
    dst = lax.broadcasted_iota(jnp.int32, (dk, dk), 1)
    to_halves = (src == jnp.where(dst < half, 2 * dst, 2 * (dst - half) + 1)).astype(BF16)
    swap_signed = jnp.where(src == (dst ^ 1), jnp.where((dst & 1) == 0, -1.0, 1.0), 0.0).astype(BF16)
    for h in range(2 * heads):
        block = w[:, h * dk:(h + 1) * dk]
        deint_ref[:, h * dk:(h + 1) * dk] = _dot(block, to_halves).astype(BF16)
        swap_ref[:, h * dk:(h + 1) * dk] = _dot(block, swap_signed).astype(BF16)


def _prep_ret_weights(w_in_all, layer, heads, dk):
    _, d, cols = w_in_all.shape
    qk_w = 2 * heads * dk
    rows = math.gcd(256, d)
    return pl.pallas_call(
        functools.partial(_ret_weight_prep_kernel, heads=heads, dk=dk),
        name="ret_weight_prep",
        grid=(d // rows,),
        in_specs=[pl.BlockSpec((None, rows, cols), lambda i: (layer, i, 0))],
        out_specs=[pl.BlockSpec((rows, cols), lambda i: (i, 0)), pl.BlockSpec((rows, cols), lambda i: (i, 0)),
                   pl.BlockSpec((rows, qk_w), lambda i: (i, 0))],
        out_shape=[jax.ShapeDtypeStruct((d, cols), BF16), jax.ShapeDtypeStruct((d, cols), BF16),
                   jax.ShapeDtypeStruct((d, qk_w), BF16)],
        compiler_params=_params("arbitrary"),
    )(w_in_all)


def _prep_gdn_weights(w_in, a_log, dt_bias, heads, dk, dv):
    pad = LANES - 2 * heads
    w = jnp.pad(w_in, ((0, 0), (0, pad))).astype(BF16)
    place = lambda p: jnp.pad(p, (heads, pad))[None, :]
    return w, place(a_log), place(dt_bias)


def _trunk(x_p, x_s, batch_p, s_ret, s_gdn, s_conv, norm_g, ffn_gu, ffn_down, ret_w, ret_w_out, gdn_w, gdn_conv_w,
           gdn_norm_w, gdn_w_out, final_g):
    depth = norm_g.shape[0]
    n_mixers = 2
    seq = x_p.shape[0] // batch_p
    _, _, ret_heads, ret_dk, ret_dv = s_ret.shape
    _, _, gdn_heads, gdn_dk, gdn_dv = s_gdn.shape
    cos_p, sin_p = _rope_tables(jnp.arange(seq, dtype=F32), ret_dk)
    cos_s, sin_s = (jnp.tile(jnp.repeat(t, 2, axis=1), (1, 2 * ret_heads))
                    for t in _rope_tables(jnp.full((1,), SAMPLE_PAST_LEN, F32), ret_dk))
    ones_dv = jnp.ones((1, ret_dv), F32)
    fg = final_g[None, :]
    norm_g4 = norm_g[:, :, None, :]
    zero_conv = jnp.zeros((batch_p,) + s_conv.shape[2:], s_conv.dtype)
    ffn_w = (ffn_gu[0, 0].astype(BF16), ffn_down[0, 0].astype(BF16))
    ret_p, gdn_p, conv_p, conv_s = [], [], [], []
    ret_s = gdn_s = None
    ret_hist, gdn_hist = [], []
    for i in range(depth):
        j = i // n_mixers
        for which, which_norm in ((0, 0), (1, 2)):
            if which == 1:
                g = norm_g[i, 1][None, :]
                if i % n_mixers == 0:
                    w_orig, w_deint, w_swap = ret_w[j]
                    x_p, s = _ret_prompt(x_p, g, cos_p, sin_p, w_deint, ret_w_out[j], batch_p, ret_heads, ret_dk,
                                         ret_dv)
                    ret_p.append(s)
                    x_s, ret_s, ret_hist = _ret_sample(x_s, g, cos_s, sin_s, w_orig, w_swap, ret_w_out[j], s_ret,
                                                       ones_dv, j, ret_hist)
                else:
                    w, a_log, dt_bias = gdn_w[j]
                    nw = gdn_norm_w[j][None, :]
                    x_p, s, buf = _gdn_prompt(x_p, g, zero_conv, w, gdn_conv_w[j], a_log, dt_bias, nw,
                                              gdn_w_out[j], batch_p, gdn_heads, gdn_dk, gdn_dv)
                    gdn_p.append(s)
                    conv_p.append(buf)
                    x_s, gdn_s, buf, gdn_hist = _gdn_sample(x_s, g, s_conv[j], w, gdn_conv_w[j], a_log, dt_bias,
                                                            nw, gdn_w_out[j], s_gdn, j, gdn_hist)
                    conv_s.append(buf)
            nxt = (i, 1) if which == 0 else (i + 1, 0)
            cast_next = (ffn_gu, ffn_down) + nxt if nxt[0] < depth else None
            x_p, x_s, made = _ffn(x_p, x_s, norm_g4, *ffn_w, fg, i, which_norm, which == 1 and i == depth - 1,
                                  cast_next)
            ffn_w = made
    return (x_p, x_s, jnp.stack(ret_p), jnp.stack(gdn_p), jnp.stack(conv_p), ret_s, gdn_s, jnp.stack(conv_s))


def kernel(x_prompt, x_sample, state_ret, state_gdn, state_gdn_conv, norm_g, ffn_gu, ffn_down, ret_w_in,
           ret_w_out, gdn_w_in, gdn_conv_w, gdn_a_log, gdn_dt_bias, gdn_norm_w, gdn_w_out, final_g):
    bp, seq, d = x_prompt.shape
    bs, dec_seq, _ = x_sample.shape
    assert dec_seq == 1
    n_ret, _, ret_heads, ret_dk, _ = state_ret.shape
    n_gdn, _, gdn_heads, gdn_dk, gdn_dv = state_gdn.shape
    ret_w = [_prep_ret_weights(ret_w_in, j, ret_heads, ret_dk) for j in range(n_ret)]
    gdn_w = [_prep_gdn_weights(gdn_w_in[j], gdn_a_log[j], gdn_dt_bias[j], gdn_heads, gdn_dk, gdn_dv)
             for j in range(n_gdn)]
    y_p, y_s, ret_p, gdn_p, conv_p, ret_s, gdn_s, conv_s = _trunk(
        x_prompt.reshape(bp * seq, d), x_sample.reshape(bs, d), bp, state_ret, state_gdn, state_gdn_conv, norm_g,
        ffn_gu, ffn_down, ret_w, ret_w_out.astype(BF16), gdn_w, gdn_conv_w, gdn_norm_w, gdn_w_out.astype(BF16),
        final_g)
    return (y_p.reshape(bp, seq, d), y_s.reshape(bs, dec_seq, d), ret_p, gdn_p, conv_p, ret_s, gdn_s, conv_s)
```

```python
import functools
import math

import jax
import jax.numpy as jnp
from jax import lax
from jax.experimental import pallas as pl
from jax.experimental.pallas import tpu as pltpu

F32 = jnp.float32
BF16 = jnp.bfloat16
EPS = 1e-6
ROPE_BASE = 10000.0
SAMPLE_PAST_LEN = 16384.0

LANES = 128
SUBLANES = 8
VMEM_LIMIT_BYTES = 56 * 1024 * 1024

FFN_ROWS = 512
RET_CHUNK = 256
GDN_CHUNK = 128
GDN_ROWS = 256
RET_STATE_BLOCK = 2
GDN_STATE_BLOCK = 8


def _dot(a, b):
    return jnp.dot(a, b, preferred_element_type=F32)


def _dot_nt(a, b):
    return lax.dot_general(a, b, (((1,), (1,)), ((), ())), preferred_element_type=F32)


def _rms(x, g):
    return x * lax.rsqrt(jnp.mean(x * x, axis=-1, keepdims=True) + EPS) * g


def _silu(x):
    h = 0.5 * x
    return h * jnp.tanh(h) + h


def _softplus(x):
    return jnp.maximum(x, 0.0) + jnp.log1p(jnp.exp(-jnp.abs(x)))


def _bf16_terms(x):
    t0 = x.astype(BF16)
    r = x - t0.astype(F32)
    t1 = r.astype(BF16)
    t2 = (r - t1.astype(F32)).astype(BF16)
    return t0, t1, t2


def _exact_dot_right(x, sel):
    return _dot(jnp.concatenate(_bf16_terms(x), axis=1), jnp.concatenate([sel] * 3, axis=0))


def _exact_dot_left(sel, x):
    return _dot(jnp.concatenate([sel] * 3, axis=1), jnp.concatenate(_bf16_terms(x), axis=0))


def _expand_heads(x, first, heads, width):
    c = lax.broadcasted_iota(jnp.int32, (x.shape[1], heads * width), 0)
    lane = lax.broadcasted_iota(jnp.int32, (x.shape[1], heads * width), 1)
    e = ((lane >= (c - first) * width) & (lane < (c - first + 1) * width)).astype(BF16)
    return _exact_dot_right(x, e)


def _dots(lhs, rhs, nt=False):
    dot = _dot_nt if nt else _dot
    return [dot(a, b) for a, b in zip(lhs, rhs)]


def _resident(shape):
    nd = len(shape)
    return pl.BlockSpec(shape, lambda *_: (0,) * nd, pipeline_mode=pl.Buffered(1))


def _params(*sem):
    return pltpu.CompilerParams(dimension_semantics=sem, vmem_limit_bytes=VMEM_LIMIT_BYTES)


def _ffn_kernel(x_ref, small_ref, g_ref, wg_ref, wu_ref, wd_ref, fg_ref, *rest, final, cast_next):
    def ffn(x):
        h = _rms(x, g_ref[...]).astype(BF16)
        a = _dot(h, wg_ref[...])
        b = _dot(h, wu_ref[...])
        act = (_silu(a) * b).astype(BF16)
        y = x + 0.5 * _dot(act, wd_ref[...])
        return _rms(y, fg_ref[...]) if final else y

    o_ref, small_out_ref = rest[-4:-2] if cast_next else rest
    o_ref[...] = ffn(x_ref[...])

    @pl.when(pl.program_id(0) == pl.num_programs(0) - 1)
    def _():
        small_out_ref[...] = ffn(small_ref[...])

    if cast_next:
        next_gu_ref, next_down_ref = rest[:2]
        gu_out_ref, down_out_ref = rest[-2:]
        gu_out_ref[...] = next_gu_ref[...].astype(BF16)
        down_out_ref[...] = next_down_ref[...].astype(BF16)


def _slab_count(rows, steps):
    return max(n for n in range(1, steps + 1) if steps % n == 0 and rows % (16 * n) == 0)


def _ffn(x, small, norm_g, w_gu, w_down, final_g, layer, which_norm, final, cast_next=None):
    m, d = x.shape
    f = w_down.shape[0]
    tm = min(FFN_ROWS, m)
    steps = m // tm
    assert m % tm == 0 and f % LANES == 0
    in_specs = [
        pl.BlockSpec((tm, d), lambda i: (i, 0)),
        _resident(small.shape),
        pl.BlockSpec((None, None, 1, d), lambda i: (layer, which_norm, 0, 0), pipeline_mode=pl.Buffered(1)),
        pl.BlockSpec((d, f), lambda i: (0, 0), pipeline_mode=pl.Buffered(1)),
        pl.BlockSpec((d, f), lambda i: (0, 1), pipeline_mode=pl.Buffered(1)),
        _resident((f, d)),
        _resident((1, d)),
    ]
    args = [x, small, norm_g, w_gu, w_gu, w_down, final_g]
    out_specs = [pl.BlockSpec((tm, d), lambda i: (i, 0)), pl.BlockSpec(small.shape, lambda i: (0, 0))]
    out_shape = [jax.ShapeDtypeStruct((m, d), F32), jax.ShapeDtypeStruct(small.shape, F32)]
    if cast_next is not None:
        next_gu, next_down, nl, nw = cast_next
        for w, rows, cols in ((next_gu, d, 2 * f), (next_down, f, d)):
            n = _slab_count(rows, steps)
            in_specs.append(pl.BlockSpec((None, None, rows // n, cols),
                                         lambda i, n=n: (nl, nw, i * n // steps, 0)))
            out_specs.append(pl.BlockSpec((rows // n, cols), lambda i, n=n: (i * n // steps, 0)))
            out_shape.append(jax.ShapeDtypeStruct((rows, cols), BF16))
            args.append(w)
    outs = pl.pallas_call(
        functools.partial(_ffn_kernel, final=final, cast_next=cast_next is not None),
        name="ffn",
        grid=(steps,),
        in_specs=in_specs,
        out_specs=out_specs,
        out_shape=out_shape,
        compiler_params=_params("arbitrary"),
    )(*args)
    return outs[0], outs[1], tuple(outs[2:])


def _ret_readout(step, block, q_ref, k_ref, v_ref, s0_ref, o_ref, *, heads, dk, dv):
    hs = range(heads)
    for i in range(block):
        b = step * block + i
        q_row, k_row, v_row = q_ref[pl.ds(b, 1), :], k_ref[pl.ds(b, 1), :], v_ref[pl.ds(b, 1), :]
        q_hs = [q_row[:, h * dk:(h + 1) * dk] for h in hs]
        q_s = [_dot(jnp.broadcast_to(q_h, (SUBLANES, dk)).astype(BF16), s0_ref[0, i, h].astype(BF16))[0:1]
               for h, q_h in zip(hs, q_hs)]
        q_k = [jnp.sum(q_h * k_row[:, h * dk:(h + 1) * dk], axis=-1, keepdims=True) for h, q_h in zip(hs, q_hs)]
        o_ref[0, i:i + 1, :] = jnp.concatenate(
            [(1.0 - 2.0 ** (-5.0 - h)) * qs + qk * v_row[:, h * dv:(h + 1) * dv] for h, qs, qk in zip(hs, q_s, q_k)],
            axis=1)


def _gdn_readout(step, block, q_ref, k_ref, v_ref, beta_ref, ea_ref, s0_ref, o_ref, *, heads, dk, dv):
    hs = range(heads)
    row = lax.broadcasted_iota(jnp.int32, (SUBLANES, dk), 0)
    for i in range(block):
        b = step * block + i
        q_row, k_row, v_row = q_ref[pl.ds(b, 1), :], k_ref[pl.ds(b, 1), :], v_ref[pl.ds(b, 1), :]
        beta_row, ea_row = beta_ref[pl.ds(b, 1), :], ea_ref[pl.ds(b, 1), :]
        q_hs = [q_row[:, h * dk:(h + 1) * dk] for h in hs]
        k_hs = [k_row[:, h * dk:(h + 1) * dk] for h in hs]
        eas = [ea_row[:, h * dv:(h + 1) * dv] for h in hs]
        lhs = [jnp.where(row == 0, q_h, jnp.where(row == 1, k_h * ea, 0.0)).astype(BF16)
               for q_h, k_h, ea in zip(q_hs, k_hs, eas)]
        prods = [_dot(l, s0_ref[0, i, h].astype(BF16)) for h, l in zip(hs, lhs)]
        ws = [beta_row[:, h * dv:(h + 1) * dv] * (v_row[:, h * dv:(h + 1) * dv] - p[1:2]) for h, p in zip(hs, prods)]
        q_k = [jnp.sum(q_h * k_h, axis=-1, keepdims=True) for q_h, k_h in zip(q_hs, k_hs)]
        o_ref[0, i:i + 1, :] = jnp.concatenate(
            [ea * p[0:1] + qk * w for ea, p, qk, w in zip(eas, prods, q_k, ws)], axis=1)


def _ret_prompt_kernel(x_ref, g_ref, cos_ref, sin_ref, win_ref, wout_ref, *rest, heads, dk, dv, chunk, side_block):
    if side_block:
        side_in, (y_ref, sout_ref, side_o_ref, s_scr, og_scr) = rest[:4], rest[4:]
        _ret_readout(pl.program_id(0) * pl.num_programs(1) + pl.program_id(1), side_block, *side_in, side_o_ref,
                     heads=heads, dk=dk, dv=dv)
    else:
        y_ref, sout_ref, s_scr, og_scr = rest
    t = pl.program_id(1)
    half = dk // 2

    @pl.when(t == 0)
    def _():
        s_scr[...] = jnp.zeros_like(s_scr)

    x = x_ref[...]
    hn = _rms(x, g_ref[...]).astype(BF16)
    cos = cos_ref[...]
    sin = sin_ref[...]
    ri = lax.broadcasted_iota(jnp.int32, (chunk, chunk), 0)
    ci = lax.broadcasted_iota(jnp.int32, (chunk, chunk), 1)
    lag = (ri - ci).astype(F32)
    row = lax.broadcasted_iota(jnp.int32, (chunk, half), 0).astype(F32)
    k_off, v_off, g_off = heads * dk, 2 * heads * dk, 2 * heads * dk + heads * dv
    for h in range(heads):
        log_gamma = math.log(1.0 - 2.0 ** (-5.0 - h))
        q = _dot(hn, win_ref[:, h * dk:(h + 1) * dk])
        k = _dot(hn, win_ref[:, k_off + h * dk:k_off + (h + 1) * dk])
        v = _dot(hn, win_ref[:, v_off + h * dv:v_off + (h + 1) * dv]).astype(BF16)
        gate = _dot(hn, win_ref[:, g_off + h * dv:g_off + (h + 1) * dv])
        q1, q2 = q[:, :half], q[:, half:]
        k1, k2 = k[:, :half], k[:, half:]
        qr1, qr2 = q1 * cos - q2 * sin, q2 * cos + q1 * sin
        kr1, kr2 = (k1 * cos - k2 * sin) * dk ** -0.5, (k2 * cos + k1 * sin) * dk ** -0.5
        qb = jnp.concatenate([qr1, qr2], axis=1).astype(BF16)
        kb = jnp.concatenate([kr1, kr2], axis=1).astype(BF16)
        decay = jnp.where(ri >= ci, jnp.exp(lag * log_gamma), 0.0)
        p = (_dot_nt(qb, kb) * decay).astype(BF16)
        q_scale = jnp.exp((row + 1.0) * log_gamma)
        k_scale = jnp.exp((chunk - 1.0 - row) * log_gamma)
        qd = jnp.concatenate([qr1 * q_scale, qr2 * q_scale], axis=1).astype(BF16)
        kd = jnp.concatenate([kr1 * k_scale, kr2 * k_scale], axis=1)
        s = s_scr[h]
        o = _dot(p, v) + _dot(qd, s.astype(BF16))
        s_scr[h] = math.exp(chunk * log_gamma) * s + _dot(kd.T.astype(BF16), v)
        o = o * lax.rsqrt(jnp.mean(o * o, axis=-1, keepdims=True) + EPS) * _silu(gate)
        og_scr[:, h * dv:(h + 1) * dv] = o.astype(BF16)
    y_ref[...] = x + _dot(og_scr[...], wout_ref[...])

    @pl.when(t == pl.num_programs(1) - 1)
    def _():
        ro = lax.broadcasted_iota(jnp.int32, (dk, dk), 0)
        rin = lax.broadcasted_iota(jnp.int32, (dk, dk), 1)
        perm = (rin == (ro >> 1) + (ro & 1) * half).astype(BF16)
        for h in range(heads):
            sout_ref[0, h] = _exact_dot_left(perm, s_scr[h])


def _side_readout_specs(small, s0, layer, steps, step_of):
    _, n, heads, dk, dv = s0.shape
    assert n % steps == 0
    block = n // steps
    in_specs = [_resident(t.shape) for t in small]
    in_specs.append(pl.BlockSpec((1, block, heads, dk, dv), lambda *ids: (layer, step_of(*ids), 0, 0, 0)))
    o_spec = pl.BlockSpec((1, block, heads * dv), lambda *ids: (step_of(*ids), 0, 0))
    return block, in_specs, o_spec, jax.ShapeDtypeStruct((steps, block, heads * dv), F32)


def _ret_prompt(x, g, cos, sin, w_in, w_out, batch, heads, dk, dv, side=None):
    m, d = x.shape
    seq = m // batch
    chunk = min(RET_CHUNK, seq)
    assert seq % chunk == 0
    nt = seq // chunk
    args = [x, g, cos, sin, w_in, w_out]
    in_specs = [
        pl.BlockSpec((chunk, d), lambda b, t: (b * nt + t, 0)),
        _resident((1, d)),
        pl.BlockSpec((chunk, dk // 2), lambda b, t: (t, 0)),
        pl.BlockSpec((chunk, dk // 2), lambda b, t: (t, 0)),
        _resident(w_in.shape),
        _resident(w_out.shape),
    ]
    out_specs = [
        pl.BlockSpec((chunk, d), lambda b, t: (b * nt + t, 0)),
        pl.BlockSpec((1, heads, dk, dv), lambda b, t: (b, 0, 0, 0)),
    ]
    out_shape = [jax.ShapeDtypeStruct((m, d), F32), jax.ShapeDtypeStruct((batch, heads, dk, dv), F32)]
    side_block = 0
    if side is not None:
        small, s0, layer = side
        side_block, side_specs, o_spec, o_shape = _side_readout_specs(small, s0, layer, batch * nt,
                                                                      lambda b, t: b * nt + t)
        args += list(small) + [s0]
        in_specs += side_specs
        out_specs.append(o_spec)
        out_shape.append(o_shape)
    outs = pl.pallas_call(
        functools.partial(_ret_prompt_kernel, heads=heads, dk=dk, dv=dv, chunk=chunk, side_block=side_block),
        name="ret_prompt",
        grid=(batch, nt),
        in_specs=in_specs,
        out_specs=out_specs,
        out_shape=out_shape,
        scratch_shapes=[pltpu.VMEM((heads, dk, dv), F32), pltpu.VMEM((chunk, heads * dv), BF16)],
        compiler_params=_params("arbitrary", "arbitrary"),
    )(*args)
    return outs if side is None else (outs[0], outs[1], outs[2].reshape(-1, outs[2].shape[-1]))


def _ret_sample_proj_kernel(x_ref, g_ref, cos_ref, sin_ref, win_ref, wsw_ref, q_ref, k_ref, kt_ref, v_ref, gate_ref,
                            *, heads, dk, dv):
    qk_w = 2 * heads * dk
    hn = _rms(x_ref[...], g_ref[...]).astype(BF16)
    qk = _dot(hn, win_ref[:, :qk_w]) * cos_ref[...] + _dot(hn, wsw_ref[...]) * sin_ref[...]
    q_ref[...] = qk[:, :heads * dk]
    k_ref[...] = qk[:, heads * dk:] * dk ** -0.5
    kt_ref[...] = k_ref[...].T
    v_ref[...] = _dot(hn, win_ref[:, qk_w:qk_w + heads * dv])
    gate_ref[...] = _dot(hn, win_ref[:, qk_w + heads * dv:])


def _ret_sample_state_kernel(q_ref, *refs, heads, dk, dv, block):
    n_layers = (len(refs) - 2) // 3
    ins, (s_ref, o_ref) = refs[:3 * n_layers], refs[3 * n_layers:]
    n = q_ref.shape[0]
    lane = lax.broadcasted_iota(jnp.int32, (dk, n), 1)
    hs = range(heads)
    gammas = [1.0 - 2.0 ** (-5.0 - h) for h in hs]
    for i in range(block):
        b = pl.program_id(0) * block + i
        for layer in range(n_layers):
            kt_ref, v_ref, s0_ref = ins[3 * layer:3 * layer + 3]
            kt_bs = [jnp.where(lane == b, kt_ref[h * dk:(h + 1) * dk, :], 0.0).astype(BF16) for h in hs]
            s_news = [gamma * s0_ref[0, i, h] + _dot(kt_b, v_ref[:, h * dv:(h + 1) * dv].astype(BF16))
                      for h, gamma, kt_b in zip(hs, gammas, kt_bs)]
            for h in hs:
                s_ref[layer, i, h] = s_news[h]
        q_row = q_ref[pl.ds(b, 1), :]
        q8s = [jnp.broadcast_to(q_row[:, h * dk:(h + 1) * dk], (SUBLANES, dk)).astype(BF16) for h in hs]
        o_ref[0, i:i + 1, :] = jnp.concatenate(
            [_dot(q8, s_new.astype(BF16))[0:1] for q8, s_new in zip(q8s, s_news)], axis=1)


def _mixer_out_kernel(x_ref, o_ref, gate_ref, nw_ref, wout_ref, y_ref, *, heads, dv):
    parts = []
    for h in range(heads):
        o = o_ref[:, h * dv:(h + 1) * dv]
        o = o * lax.rsqrt(jnp.mean(o * o, axis=-1, keepdims=True) + EPS) * nw_ref[...]
        parts.append((o * _silu(gate_ref[:, h * dv:(h + 1) * dv])).astype(BF16))
    y_ref[...] = x_ref[...] + _dot(jnp.concatenate(parts, axis=1), wout_ref[...])


def _mixer_out(x, o, gate, norm_w, w_out, heads, dv):
    return pl.pallas_call(
        functools.partial(_mixer_out_kernel, heads=heads, dv=dv),
        name="mixer_out",
        out_shape=jax.ShapeDtypeStruct(x.shape, F32),
        compiler_params=_params(),
    )(x, o, gate, norm_w, w_out)


def _ret_sample_proj(x, g, cos, sin, w_in, w_sw, heads, dk, dv):
    n = x.shape[0]
    return pl.pallas_call(
        functools.partial(_ret_sample_proj_kernel, heads=heads, dk=dk, dv=dv),
        name="ret_sample_proj",
        out_shape=[
            jax.ShapeDtypeStruct((n, heads * dk), F32),
            jax.ShapeDtypeStruct((n, heads * dk), F32),
            jax.ShapeDtypeStruct((heads * dk, n), F32),
            jax.ShapeDtypeStruct((n, heads * dv), F32),
            jax.ShapeDtypeStruct((n, heads * dv), F32),
        ],
        compiler_params=_params(),
    )(x, g, cos, sin, w_in, w_sw)


def _ret_sample_final(q, history, s0):
    n = q.shape[0]
    layers, _, heads, dk, dv = s0.shape
    assert len(history) == layers
    block = math.gcd(RET_STATE_BLOCK, n)
    args, in_specs = [q], [_resident(q.shape)]
    for idx, (kt_l, v_l) in enumerate(history):
        args += [kt_l, v_l, s0]
        in_specs += [_resident(kt_l.shape), _resident(v_l.shape),
                     pl.BlockSpec((1, block, heads, dk, dv), lambda i, idx=idx: (idx, i, 0, 0, 0))]
    s_new, o = pl.pallas_call(
        functools.partial(_ret_sample_state_kernel, heads=heads, dk=dk, dv=dv, block=block),
        name="ret_sample_state",
        grid=(n // block,),
        in_specs=in_specs,
        out_specs=[pl.BlockSpec((layers, block, heads, dk, dv), lambda i: (0, i, 0, 0, 0)),
                   pl.BlockSpec((1, block, heads * dv), lambda i: (i, 0, 0))],
        out_shape=[jax.ShapeDtypeStruct(s0.shape, F32), jax.ShapeDtypeStruct((n // block, block, heads * dv), F32)],
        compiler_params=_params("arbitrary"),
    )(*args)
    return s_new, o.reshape(n, heads * dv)


def _unit_lower_inverse_minus_identity(lms, ri, ci, chunk):
    def level_mask(lev):
        return ((ri >> lev) == (ci >> lev)) & ((ri >> (lev - 1)) != (ci >> (lev - 1)))

    mask = level_mask(1)
    ns = [jnp.where(mask, -lm, 0.0) for lm in lms]
    for lev in range(2, chunk.bit_length()):
        mask = level_mask(lev)
        cs = [jnp.where(mask, lm, 0.0) for lm in lms]
        nbs = [n.astype(BF16) for n in ns]
        gs = [c + cn for c, cn in zip(cs, _dots([c.astype(BF16) for c in cs], nbs))]
        ns = [n - g - ng for n, g, ng in zip(ns, gs, _dots(nbs, [g.astype(BF16) for g in gs]))]
    return ns


def _gdn_prompt_kernel(x_ref, g_ref, cbuf_ref, win_ref, cw_ref, alog_ref, dtb_ref, nw_ref, wout_ref, *rest,
                       heads, dk, dv, rows, chunk, side_block):
    if side_block:
        side_in, (y_ref, sout_ref, cout_ref, side_o_ref, s_scr, xh_scr, uh_scr, og_scr) = rest[:6], rest[6:]
        _gdn_readout(pl.program_id(0) * pl.num_programs(1) + pl.program_id(1), side_block, *side_in, side_o_ref,
                     heads=heads, dk=dk, dv=dv)
    else:
        y_ref, sout_ref, cout_ref, s_scr, xh_scr, uh_scr, og_scr = rest
    t = pl.program_id(1)
    qkv_w = heads * (2 * dk + dv)
    hd = heads * dv
    cw = [cw_ref[j:j + 1, :] for j in range(4)]

    @pl.when(t == 0)
    def _():
        s_scr[...] = jnp.zeros_like(s_scr)
        xh_scr[...] = jnp.zeros_like(xh_scr)
        uh_scr[...] = jnp.zeros_like(uh_scr)
        xh_scr[0:1, :] = cbuf_ref[0, 2:3, :]
        uh_scr[0:1, :] = cw[1] * cbuf_ref[0, 1:2, :] + cw[0] * cbuf_ref[0, 0:1, :]
        uh_scr[1:2, :] = cw[1] * cbuf_ref[0, 2:3, :] + cw[0] * cbuf_ref[0, 1:2, :]

    x = x_ref[...]
    hn = _rms(x, g_ref[...]).astype(BF16)
    qkv = _dot(hn, win_ref[:, :qkv_w])
    sub = lax.broadcasted_iota(jnp.int32, (SUBLANES, qkv_w), 0)

    def shifted(cur, by, halo_tile):
        rolled = pltpu.roll(cur, by, 0)
        first = jnp.where(sub < by, halo_tile, rolled[:SUBLANES])
        return jnp.concatenate([first, rolled[SUBLANES:]], axis=0)

    x_prev = shifted(qkv, 1, xh_scr[...])
    u = cw[1] * qkv + cw[0] * x_prev
    conv = cw[3] * qkv + cw[2] * x_prev + shifted(u, 2, uh_scr[...])
    xh_scr[0:1, :] = qkv[rows - 1:rows, :]
    uh_scr[0:2, :] = u[rows - 2:rows, :]

    @pl.when(t == pl.num_programs(1) - 1)
    def _():
        cout_ref[0] = qkv[rows - 3:rows, :]

    act = _silu(conv)
    z = _dot(hn, win_ref[:, qkv_w:qkv_w + hd])
    ba = _dot(hn, win_ref[:, qkv_w + hd:])
    a = -jnp.exp(alog_ref[...]) * _softplus(ba + dtb_ref[...])
    ri = lax.broadcasted_iota(jnp.int32, (chunk, chunk), 0)
    ci = lax.broadcasted_iota(jnp.int32, (chunk, chunk), 1)
    tril = (ri >= ci).astype(BF16)
    subs = range(rows // chunk)
    bcum = jnp.concatenate([_exact_dot_left(tril, a[c * chunk:(c + 1) * chunk]) for c in subs], axis=0)
    beta = _expand_heads(jax.nn.sigmoid(ba), 0, heads, dv)
    bcum = _expand_heads(bcum, heads, heads, dv)
    probs = [(c, h) for c in subs for h in range(heads)]

    def tile(arr, c, col, width):
        return arr[c * chunk:(c + 1) * chunk, col:col + width]

    qs = [tile(act, c, h * dk, dk) for c, h in probs]
    ks = [tile(act, c, heads * dk + h * dk, dk) for c, h in probs]
    vs = [tile(act, c, 2 * heads * dk + h * dv, dv) for c, h in probs]
    qs = [q * lax.rsqrt(jnp.sum(q * q, axis=-1, keepdims=True) + EPS) * dk ** -0.5 for q in qs]
    ks = [k * lax.rsqrt(jnp.sum(k * k, axis=-1, keepdims=True) + EPS) for k in ks]
    bts = [tile(beta, c, h * dv, dv) for c, h in probs]
    bs = [tile(bcum, c, h * dv, dv) for c, h in probs]
    es = [jnp.exp(jnp.minimum(b - b.T, 0.0)) for b in bs]
    kbs = [k.astype(BF16) for k in ks]
    qk_kks = _dots([jnp.concatenate([q.astype(BF16), kb], axis=0) for q, kb in zip(qs, kbs)], kbs, nt=True)
    ps = [jnp.where(ri >= ci, qk_kk[:chunk] * e, 0.0).astype(BF16) for qk_kk, e in zip(qk_kks, es)]
    lms = [jnp.where(ri > ci, qk_kk[chunk:] * e, 0.0) * bt for qk_kk, e, bt in zip(qk_kks, es, bts)]
    ns = [n.astype(BF16) for n in _unit_lower_inverse_minus_identity(lms, ri, ci, chunk)]
    kq_es = [jnp.concatenate([(k * jnp.exp(b)).astype(BF16), (q * jnp.exp(b)).astype(BF16)], axis=0)
             for k, q, b in zip(ks, qs, bs)]
    b_lasts = [b[chunk - 1:chunk, :] for b in bs]
    kdts = [(k * jnp.exp(b_last - b)).T.astype(BF16) for k, b, b_last in zip(ks, bs, b_lasts)]
    ss = [s_scr[h] for h in range(heads)]
    for c in subs:
        sel = slice(c * heads, (c + 1) * heads)
        kq_ss = _dots(kq_es[sel], [s.astype(BF16) for s in ss])
        rs = [bt * (v - kq_s[:chunk]) for bt, v, kq_s in zip(bts[sel], vs[sel], kq_ss)]
        wbs = [(r + nr).astype(BF16) for r, nr in zip(rs, _dots(ns[sel], [r.astype(BF16) for r in rs]))]
        os_ = [kq_s[chunk:] + pw for kq_s, pw in zip(kq_ss, _dots(ps[sel], wbs))]
        ss = [jnp.exp(b_last) * s + kw for b_last, s, kw in zip(b_lasts[sel], ss, _dots(kdts[sel], wbs))]
        for h in range(heads):
            o = os_[h]
            o = o * lax.rsqrt(jnp.mean(o * o, axis=-1, keepdims=True) + EPS) * nw_ref[...]
            og_scr[c * chunk:(c + 1) * chunk, h * dv:(h + 1) * dv] = (o * _silu(tile(z, c, h * dv, dv))).astype(BF16)
    for h in range(heads):
        s_scr[h] = ss[h]
    y_ref[...] = x + _dot(og_scr[...], wout_ref[...])

    @pl.when(t == pl.num_programs(1) - 1)
    def _():
        sout_ref[0] = s_scr[...]


def _gdn_prompt(x, g, cbuf, w_in, conv_w, a_log, dt_bias, norm_w, w_out, batch, heads, dk, dv, side=None):
    m, d = x.shape
    seq = m // batch
    chunk = GDN_CHUNK
    taps = conv_w.shape[0]
    qkv_w = heads * (2 * dk + dv)
    rows = math.gcd(GDN_ROWS, seq)
    assert rows % chunk == 0 and dk == chunk and dv == chunk and chunk == LANES and 2 * heads <= LANES and taps == 4
    nt = seq // rows
    args = [x, g, cbuf, w_in, conv_w, a_log, dt_bias, norm_w, w_out]
    in_specs = [
        pl.BlockSpec((rows, d), lambda b, t: (b * nt + t, 0)),
        _resident((1, d)),
        pl.BlockSpec((1, taps - 1, qkv_w), lambda b, t: (b, 0, 0)),
        _resident(w_in.shape),
        _resident(conv_w.shape),
        _resident(a_log.shape),
        _resident(dt_bias.shape),
        _resident(norm_w.shape),
        _resident(w_out.shape),
    ]
    out_specs = [
        pl.BlockSpec((rows, d), lambda b, t: (b * nt + t, 0)),
        pl.BlockSpec((1, heads, dk, dv), lambda b, t: (b, 0, 0, 0)),
        pl.BlockSpec((1, taps - 1, qkv_w), lambda b, t: (b, 0, 0)),
    ]
    out_shape = [
        jax.ShapeDtypeStruct((m, d), F32),
        jax.ShapeDtypeStruct((batch, heads, dk, dv), F32),
        jax.ShapeDtypeStruct((batch, taps - 1, qkv_w), F32),
    ]
    side_block = 0
    if side is not None:
        small, s0, layer = side
        side_block, side_specs, o_spec, o_shape = _side_readout_specs(small, s0, layer, batch * nt,
                                                                      lambda b, t: b * nt + t)
        args += list(small) + [s0]
        in_specs += side_specs
        out_specs.append(o_spec)
        out_shape.append(o_shape)
    outs = pl.pallas_call(
        functools.partial(_gdn_prompt_kernel, heads=heads, dk=dk, dv=dv, rows=rows, chunk=chunk,
                          side_block=side_block),
        name="gdn_prompt",
        grid=(batch, nt),
        in_specs=in_specs,
        out_specs=out_specs,
        out_shape=out_shape,
        scratch_shapes=[
            pltpu.VMEM((heads, dk, dv), F32),
            pltpu.VMEM((SUBLANES, qkv_w), F32),
            pltpu.VMEM((SUBLANES, qkv_w), F32),
            pltpu.VMEM((rows, heads * dv), BF16),
        ],
        compiler_params=_params("arbitrary", "arbitrary"),
    )(*args)
    return outs if side is None else (outs[0], outs[1], outs[2], outs[3].reshape(-1, outs[3].shape[-1]))


def _gdn_sample_proj_kernel(x_ref, g_ref, cbuf_ref, win_ref, cw_ref, alog_ref, dtb_ref,
                            q_ref, kt_ref, k_ref, v_ref, z_ref, beta_ref, ea_ref, cout_ref, *, heads, dk, dv, taps):
    qkv_w = heads * (2 * dk + dv)
    hd = heads * dv
    halo = taps - 1
    hn = _rms(x_ref[...], g_ref[...]).astype(BF16)
    qkv = _dot(hn, win_ref[:, :qkv_w])
    conv = qkv * cw_ref[halo:taps, :]
    for j in range(halo):
        conv = conv + cbuf_ref[j] * cw_ref[j:j + 1, :]
        if j > 0:
            cout_ref[j - 1] = cbuf_ref[j]
    cout_ref[halo - 1] = qkv
    act = _silu(conv)
    for h in range(heads):
        q = act[:, h * dk:(h + 1) * dk]
        k = act[:, heads * dk + h * dk:heads * dk + (h + 1) * dk]
        q_ref[:, h * dk:(h + 1) * dk] = q * lax.rsqrt(jnp.sum(q * q, axis=-1, keepdims=True) + EPS) * dk ** -0.5
        k_ref[:, h * dk:(h + 1) * dk] = k * lax.rsqrt(jnp.sum(k * k, axis=-1, keepdims=True) + EPS)
    kt_ref[...] = k_ref[...].T
    v_ref[...] = act[:, 2 * heads * dk:]
    z_ref[...] = _dot(hn, win_ref[:, qkv_w:qkv_w + hd])
    ba = _dot(hn, win_ref[:, qkv_w + hd:])
    beta_ref[...] = _expand_heads(jax.nn.sigmoid(ba), 0, heads, dv)
    a = -jnp.exp(alog_ref[...]) * _softplus(ba + dtb_ref[...])
    ea_ref[...] = jnp.exp(_expand_heads(a, heads, heads, dv))


def _gdn_sample_state_kernel(q_ref, *refs, heads, dk, dv, block):
    n_layers = (len(refs) - 2) // 6
    ins, (s_ref, o_ref) = refs[:6 * n_layers], refs[6 * n_layers:]
    n = q_ref.shape[0]
    lane = lax.broadcasted_iota(jnp.int32, (dk, n), 1)
    hs = range(heads)
    for i in range(block):
        b = pl.program_id(0) * block + i
        for layer in range(n_layers):
            kt_ref, k_ref, v_ref, beta_ref, ea_ref, s0_ref = ins[6 * layer:6 * layer + 6]
            k_row, v_row = k_ref[pl.ds(b, 1), :], v_ref[pl.ds(b, 1), :]
            beta_row, ea_row = beta_ref[pl.ds(b, 1), :], ea_ref[pl.ds(b, 1), :]
            s0s = [s0_ref[0, i, h] for h in hs]
            eas = [ea_row[:, h * dv:(h + 1) * dv] for h in hs]
            k8s = [jnp.broadcast_to(k_row[:, h * dk:(h + 1) * dk] * ea, (SUBLANES, dk)).astype(BF16)
                   for h, ea in zip(hs, eas)]
            kss = [_dot(k8, s0.astype(BF16))[0:1] for k8, s0 in zip(k8s, s0s)]
            ws = [beta_row[:, h * dv:(h + 1) * dv] * (v_row[:, h * dv:(h + 1) * dv] - ks) for h, ks in zip(hs, kss)]
            kt_bs = [jnp.where(lane == b, kt_ref[h * dk:(h + 1) * dk, :], 0.0).astype(BF16) for h in hs]
            s_news = [ea * s0 + _dot(kt_b, jnp.broadcast_to(w, (n, dv)).astype(BF16))
                      for ea, s0, kt_b, w in zip(eas, s0s, kt_bs, ws)]
            for h in hs:
                s_ref[layer, i, h] = s_news[h]
        q_row = q_ref[pl.ds(b, 1), :]
        q8s = [jnp.broadcast_to(q_row[:, h * dk:(h + 1) * dk], (SUBLANES, dk)).astype(BF16) for h in hs]
        o_ref[i:i + 1, :] = jnp.concatenate(
            [_dot(q8, s_new.astype(BF16))[0:1] for q8, s_new in zip(q8s, s_news)], axis=1)


def _gdn_sample_proj(x, g, cbuf, w_in, conv_w, a_log, dt_bias, heads, dk, dv):
    n = x.shape[0]
    taps = conv_w.shape[0]
    qkv_w = heads * (2 * dk + dv)
    hd = heads * dv
    cbuf_t = jnp.swapaxes(cbuf, 0, 1)
    q, kt, k, v, z, beta, ea, cout_t = pl.pallas_call(
        functools.partial(_gdn_sample_proj_kernel, heads=heads, dk=dk, dv=dv, taps=taps),
        name="gdn_sample_proj",
        out_shape=[
            jax.ShapeDtypeStruct((n, heads * dk), F32),
            jax.ShapeDtypeStruct((heads * dk, n), F32),
            jax.ShapeDtypeStruct((n, heads * dk), F32),
            jax.ShapeDtypeStruct((n, hd), F32),
            jax.ShapeDtypeStruct((n, hd), F32),
            jax.ShapeDtypeStruct((n, hd), F32),
            jax.ShapeDtypeStruct((n, hd), F32),
            jax.ShapeDtypeStruct((taps - 1, n, qkv_w), F32),
        ],
        compiler_params=_params(),
    )(x, g, cbuf_t, w_in, conv_w, a_log, dt_bias)
    return q, kt, k, v, z, beta, ea, jnp.swapaxes(cout_t, 0, 1)


def _gdn_sample_final(q, history, s0):
    n = q.shape[0]
    layers, _, heads, dk, dv = s0.shape
    assert len(history) == layers
    hd = heads * dv
    block = math.gcd(GDN_STATE_BLOCK, n)
    args, in_specs = [q], [_resident(q.shape)]
    for idx, small in enumerate(history):
        args += list(small) + [s0]
        in_specs += [_resident(t.shape) for t in small]
        in_specs.append(pl.BlockSpec((1, block, heads, dk, dv), lambda i, idx=idx: (idx, i, 0, 0, 0)))
    return pl.pallas_call(
        functools.partial(_gdn_sample_state_kernel, heads=heads, dk=dk, dv=dv, block=block),
        name="gdn_sample_state",
        grid=(n // block,),
        in_specs=in_specs,
        out_specs=[pl.BlockSpec((layers, block, heads, dk, dv), lambda i: (0, i, 0, 0, 0)),
                   pl.BlockSpec((block, hd), lambda i: (i, 0))],
        out_shape=[jax.ShapeDtypeStruct(s0.shape, F32), jax.ShapeDtypeStruct((n, hd), F32)],
        compiler_params=_params("arbitrary"),
    )(*args)


def _rope_tables(pos, dk):
    theta = 1.0 / (ROPE_BASE ** jnp.linspace(0.0, 1.0, dk // 2, dtype=F32))
    ang = pos[:, None] * theta[None, :]
    return jnp.cos(ang), jnp.sin(ang)


def _ret_weight_prep_kernel(w_ref, w_out_ref, deint_ref, swap_ref, *, heads, dk):
    qk_w = 2 * heads * dk
    half = dk // 2
    w = w_ref[...].astype(BF16)
    w_out_ref[...] = w
    deint_ref[:, qk_w:] = w[:, qk_w:]
    src = lax.broadcasted_iota(jnp.int32, (dk, dk), 0)
    dst = lax.broadcasted_iota(jnp.int32, (dk, dk), 1)
    to_halves = (src == jnp.where(dst < half, 2 * dst, 2 * (dst - half) + 1)).astype(BF16)
    swap_signed = jnp.where(src == (dst ^ 1), jnp.where((dst & 1) == 0, -1.0, 1.0), 0.0).astype(BF16)
    for h in range(2 * heads):
        block = w[:, h * dk:(h + 1) * dk]
        deint_ref[:, h * dk:(h + 1) * dk] = _dot(block, to_halves).astype(BF16)
        swap_ref[:, h * dk:(h + 1) * dk] = _dot(block, swap_signed).astype(BF16)


def _prep_ret_weights(w_in_all, layer, heads, dk):
    _, d, cols = w_in_all.shape
    qk_w = 2 * heads * dk
    rows = math.gcd(256, d)
    return pl.pallas_call(
        functools.partial(_ret_weight_prep_kernel, heads=heads, dk=dk),
        name="ret_weight_prep",
        grid=(d // rows,),
        in_specs=[pl.BlockSpec((None, rows, cols), lambda i: (layer, i, 0))],
        out_specs=[pl.BlockSpec((rows, cols), lambda i: (i, 0)), pl.BlockSpec((rows, cols), lambda i: (i, 0)),
                   pl.BlockSpec((rows, qk_w), lambda i: (i, 0))],
        out_shape=[jax.ShapeDtypeStruct((d, cols), BF16), jax.ShapeDtypeStruct((d, cols), BF16),
                   jax.ShapeDtypeStruct((d, qk_w), BF16)],
        compiler_params=_params("arbitrary"),
    )(w_in_all)


def _prep_gdn_weights(w_in, a_log, dt_bias, heads, dk, dv):
    pad = LANES - 2 * heads
    w = jnp.pad(w_in, ((0, 0), (0, pad))).astype(BF16)
    place = lambda p: jnp.pad(p, (heads, pad))[None, :]
    return w, place(a_log), place(dt_bias)


def _trunk(x_p, x_s, batch_p, s_ret, s_gdn, s_conv, norm_g, ffn_gu, ffn_down, ret_w, ret_w_out, gdn_w, gdn_conv_w,
           gdn_norm_w, gdn_w_out, final_g):
    depth = norm_g.shape[0]
    n_mixers = 2
    seq = x_p.shape[0] // batch_p
    n_ret, _, ret_heads, ret_dk, ret_dv = s_ret.shape
    n_gdn, _, gdn_heads, gdn_dk, gdn_dv = s_gdn.shape
    cos_p, sin_p = _rope_tables(jnp.arange(seq, dtype=F32), ret_dk)
    cos_s, sin_s = (jnp.tile(jnp.repeat(t, 2, axis=1), (1, 2 * ret_heads))
                    for t in _rope_tables(jnp.full((1,), SAMPLE_PAST_LEN, F32), ret_dk))
    ones_dv = jnp.ones((1, ret_dv), F32)
    fg = final_g[None, :]
    norm_g4 = norm_g[:, :, None, :]
    zero_conv = jnp.zeros((batch_p,) + s_conv.shape[2:], s_conv.dtype)
    ffn_w = (ffn_gu[0, 0].astype(BF16), ffn_down[0, 0].astype(BF16))
    ret_p, gdn_p, conv_p, conv_s = [], [], [], []
    ret_s = gdn_s = None
    ret_hist, gdn_hist = [], []
    for i in range(depth):
        j = i // n_mixers
        for which, which_norm in ((0, 0), (1, 2)):
            if which == 1:
                g = norm_g[i, 1][None, :]
                if i % n_mixers == 0:
                    w_orig, w_deint, w_swap = ret_w[j]
                    q, k, kt, v, gate = _ret_sample_proj(x_s, g, cos_s, sin_s, w_orig, w_swap, ret_heads, ret_dk,
                                                         ret_dv)
                    ret_hist.append((kt, v))
                    prompt_args = (x_p, g, cos_p, sin_p, w_deint, ret_w_out[j], batch_p, ret_heads, ret_dk, ret_dv)
                    if j == n_ret - 1:
                        x_p, s = _ret_prompt(*prompt_args)
                        ret_s, o = _ret_sample_final(q, ret_hist, s_ret)
                    else:
                        x_p, s, o = _ret_prompt(*prompt_args, side=((q, k, v), s_ret, j))
                    ret_p.append(s)
                    x_s = _mixer_out(x_s, o, gate, ones_dv, ret_w_out[j], ret_heads, ret_dv)
                else:
                    w, a_log, dt_bias = gdn_w[j]
                    nw = gdn_norm_w[j][None, :]
                    q, kt, k, v, z, beta, ea, buf = _gdn_sample_proj(x_s, g, s_conv[j], w, gdn_conv_w[j], a_log,
                                                                     dt_bias, gdn_heads, gdn_dk, gdn_dv)
                    conv_s.append(buf)
                    gdn_hist.append((kt, k, v, beta, ea))
                    prompt_args = (x_p, g, zero_conv, w, gdn_conv_w[j], a_log, dt_bias, nw, gdn_w_out[j], batch_p,
                                   gdn_heads, gdn_dk, gdn_dv)
                    if j == n_gdn - 1:
                        x_p, s, buf = _gdn_prompt(*prompt_args)
                        gdn_s, o = _gdn_sample_final(q, gdn_hist, s_gdn)
                    else:
                        x_p, s, buf, o = _gdn_prompt(*prompt_args, side=((q, k, v, beta, ea), s_gdn, j))
                    gdn_p.append(s)
                    conv_p.append(buf)
                    x_s = _mixer_out(x_s, o, z, nw, gdn_w_out[j], gdn_heads, gdn_dv)
            nxt = (i, 1) if which == 0 else (i + 1, 0)
            cast_next = (ffn_gu, ffn_down) + nxt if nxt[0] < depth else None
            x_p, x_s, made = _ffn(x_p, x_s, norm_g4, *ffn_w, fg, i, which_norm, which == 1 and i == depth - 1,
                                  cast_next)
            ffn_w = made
    return (x_p, x_s, jnp.stack(ret_p), jnp.stack(gdn_p), jnp.stack(conv_p), ret_s, gdn_s, jnp.stack(conv_s))


def kernel(x_prompt, x_sample, state_ret, state_gdn, state_gdn_conv, norm_g, ffn_gu, ffn_down, ret_w_in,
           ret_w_out, gdn_w_in, gdn_conv_w, gdn_a_log, gdn_dt_bias, gdn_norm_w, gdn_w_out, final_g):
    bp, seq, d = x_prompt.shape
    bs, dec_seq, _ = x_sample.shape
    assert dec_seq == 1
    n_ret, _, ret_heads, ret_dk, _ = state_ret.shape
    n_gdn, _, gdn_heads, gdn_dk, gdn_dv = state_gdn.shape
    ret_w = [_prep_ret_weights(ret_w_in, j, ret_heads, ret_dk) for j in range(n_ret)]
    gdn_w = [_prep_gdn_weights(gdn_w_in[j], gdn_a_log[j], gdn_dt_bias[j], gdn_heads, gdn_dk, gdn_dv)
             for j in range(n_gdn)]
    y_p, y_s, ret_p, gdn_p, conv_p, ret_s, gdn_s, conv_s = _trunk(
        x_prompt.reshape(bp * seq, d), x_sample.reshape(bs, d), bp, state_ret, state_gdn, state_gdn_conv, norm_g,
        ffn_gu, ffn_down, ret_w, ret_w_out.astype(BF16), gdn_w, gdn_conv_w, gdn_norm_w, gdn_w_out.astype(BF16),
        final_g)
    return (y_p.reshape(bp, seq, d), y_s.reshape(bs, dec_seq, d), ret_p, gdn_p, conv_p, ret_s, gdn_s, conv_s)
```

```python
import functools
import math

import jax
import jax.numpy as jnp
from jax import lax
from jax.experimental import pallas as pl
from jax.experimental.pallas import tpu as pltpu

F32 = jnp.float32
BF16 = jnp.bfloat16
EPS = 1e-6
ROPE_BASE = 10000.0
SAMPLE_PAST_LEN = 16384.0

LANES = 128
SUBLANES = 8
VMEM_LIMIT_BYTES = 56 * 1024 * 1024

FFN_ROWS = 512
RET_CHUNK = 256
GDN_CHUNK = 128
GDN_ROWS = 256
RET_STATE_BLOCK = 2
GDN_STATE_BLOCK = 8


def _dot(a, b):
    return jnp.dot(a, b, preferred_element_type=F32)


def _dot_nt(a, b):
    return lax.dot_general(a, b, (((1,), (1,)), ((), ())), preferred_element_type=F32)


def _rms(x, g):
    return x * lax.rsqrt(jnp.mean(x * x, axis=-1, keepdims=True) + EPS) * g


def _silu(x):
    h = 0.5 * x
    return h * jnp.tanh(h) + h


def _softplus(x):
    return jnp.maximum(x, 0.0) + jnp.log1p(jnp.exp(-jnp.abs(x)))


def _bf16_terms(x):
    t0 = x.astype(BF16)
    r = x - t0.astype(F32)
    t1 = r.astype(BF16)
    t2 = (r - t1.astype(F32)).astype(BF16)
    return t0, t1, t2


def _exact_dot_right(x, sel):
    return _dot(jnp.concatenate(_bf16_terms(x), axis=1), jnp.concatenate([sel] * 3, axis=0))


def _exact_dot_left(sel, x):
    return _dot(jnp.concatenate([sel] * 3, axis=1), jnp.concatenate(_bf16_terms(x), axis=0))


def _expand_heads(x, first, heads, width):
    c = lax.broadcasted_iota(jnp.int32, (x.shape[1], heads * width), 0)
    lane = lax.broadcasted_iota(jnp.int32, (x.shape[1], heads * width), 1)
    e = ((lane >= (c - first) * width) & (lane < (c - first + 1) * width)).astype(BF16)
    return _exact_dot_right(x, e)


def _dots(lhs, rhs, nt=False):
    dot = _dot_nt if nt else _dot
    return [dot(a, b) for a, b in zip(lhs, rhs)]


def _resident(shape):
    nd = len(shape)
    return pl.BlockSpec(shape, lambda *_: (0,) * nd, pipeline_mode=pl.Buffered(1))


def _params(*sem):
    return pltpu.CompilerParams(dimension_semantics=sem, vmem_limit_bytes=VMEM_LIMIT_BYTES)


def _ffn_kernel(x_ref, small_ref, g_ref, wg_ref, wu_ref, wd_ref, fg_ref, *rest, final, cast_next):
    def ffn(x):
        h = _rms(x, g_ref[...]).astype(BF16)
        a = _dot(h, wg_ref[...])
        b = _dot(h, wu_ref[...])
        act = (_silu(a) * b).astype(BF16)
        y = x + 0.5 * _dot(act, wd_ref[...])
        return _rms(y, fg_ref[...]) if final else y

    o_ref, small_out_ref = rest[-4:-2] if cast_next else rest
    o_ref[...] = ffn(x_ref[...])

    @pl.when(pl.program_id(0) == pl.num_programs(0) - 1)
    def _():
        small_out_ref[...] = ffn(small_ref[...])

    if cast_next:
        next_gu_ref, next_down_ref = rest[:2]
        gu_out_ref, down_out_ref = rest[-2:]
        gu_out_ref[...] = next_gu_ref[...].astype(BF16)
        down_out_ref[...] = next_down_ref[...].astype(BF16)


def _slab_count(rows, steps):
    return max(n for n in range(1, steps + 1) if steps % n == 0 and rows % (16 * n) == 0)


def _ffn(x, small, norm_g, w_gu, w_down, final_g, layer, which_norm, final, cast_next=None):
    m, d = x.shape
    f = w_down.shape[0]
    tm = min(FFN_ROWS, m)
    steps = m // tm
    assert m % tm == 0 and f % LANES == 0
    in_specs = [
        pl.BlockSpec((tm, d), lambda i: (i, 0)),
        _resident(small.shape),
        pl.BlockSpec((None, None, 1, d), lambda i: (layer, which_norm, 0, 0), pipeline_mode=pl.Buffered(1)),
        pl.BlockSpec((d, f), lambda i: (0, 0), pipeline_mode=pl.Buffered(1)),
        pl.BlockSpec((d, f), lambda i: (0, 1), pipeline_mode=pl.Buffered(1)),
        _resident((f, d)),
        _resident((1, d)),
    ]
    args = [x, small, norm_g, w_gu, w_gu, w_down, final_g]
    out_specs = [pl.BlockSpec((tm, d), lambda i: (i, 0)), pl.BlockSpec(small.shape, lambda i: (0, 0))]
    out_shape = [jax.ShapeDtypeStruct((m, d), F32), jax.ShapeDtypeStruct(small.shape, F32)]
    if cast_next is not None:
        next_gu, next_down, nl, nw = cast_next
        for w, rows, cols in ((next_gu, d, 2 * f), (next_down, f, d)):
            n = _slab_count(rows, steps)
            in_specs.append(pl.BlockSpec((None, None, rows // n, cols),
                                         lambda i, n=n: (nl, nw, i * n // steps, 0)))
            out_specs.append(pl.BlockSpec((rows // n, cols), lambda i, n=n: (i * n // steps, 0)))
            out_shape.append(jax.ShapeDtypeStruct((rows, cols), BF16))
            args.append(w)
    outs = pl.pallas_call(
        functools.partial(_ffn_kernel, final=final, cast_next=cast_next is not None),
        name="ffn",
        grid=(steps,),
        in_specs=in_specs,
        out_specs=out_specs,
        out_shape=out_shape,
        compiler_params=_params("arbitrary"),
    )(*args)
    return outs[0], outs[1], tuple(outs[2:])


def _ret_readout(step, block, q_ref, k_ref, v_ref, s0_ref, o_ref, *, heads, dk, dv):
    hs = range(heads)
    for i in range(block):
        b = step * block + i
        q_row, k_row, v_row = q_ref[pl.ds(b, 1), :], k_ref[pl.ds(b, 1), :], v_ref[pl.ds(b, 1), :]
        q_hs = [q_row[:, h * dk:(h + 1) * dk] for h in hs]
        q_s = [_dot(jnp.broadcast_to(q_h, (SUBLANES, dk)).astype(BF16), s0_ref[0, i, h].astype(BF16))[0:1]
               for h, q_h in zip(hs, q_hs)]
        q_k = [jnp.sum(q_h * k_row[:, h * dk:(h + 1) * dk], axis=-1, keepdims=True) for h, q_h in zip(hs, q_hs)]
        o_ref[0, i:i + 1, :] = jnp.concatenate(
            [(1.0 - 2.0 ** (-5.0 - h)) * qs + qk * v_row[:, h * dv:(h + 1) * dv] for h, qs, qk in zip(hs, q_s, q_k)],
            axis=1)


def _gdn_readout(step, block, q_ref, k_ref, v_ref, beta_ref, ea_ref, s0_ref, o_ref, *, heads, dk, dv):
    hs = range(heads)
    row = lax.broadcasted_iota(jnp.int32, (SUBLANES, dk), 0)
    for i in range(block):
        b = step * block + i
        q_row, k_row, v_row = q_ref[pl.ds(b, 1), :], k_ref[pl.ds(b, 1), :], v_ref[pl.ds(b, 1), :]
        beta_row, ea_row = beta_ref[pl.ds(b, 1), :], ea_ref[pl.ds(b, 1), :]
        q_hs = [q_row[:, h * dk:(h + 1) * dk] for h in hs]
        k_hs = [k_row[:, h * dk:(h + 1) * dk] for h in hs]
        eas = [ea_row[:, h * dv:(h + 1) * dv] for h in hs]
        lhs = [jnp.where(row == 0, q_h, jnp.where(row == 1, k_h * ea, 0.0)).astype(BF16)
               for q_h, k_h, ea in zip(q_hs, k_hs, eas)]
        prods = [_dot(l, s0_ref[0, i, h].astype(BF16)) for h, l in zip(hs, lhs)]
        ws = [beta_row[:, h * dv:(h + 1) * dv] * (v_row[:, h * dv:(h + 1) * dv] - p[1:2]) for h, p in zip(hs, prods)]
        q_k = [jnp.sum(q_h * k_h, axis=-1, keepdims=True) for q_h, k_h in zip(q_hs, k_hs)]
        o_ref[0, i:i + 1, :] = jnp.concatenate(
            [ea * p[0:1] + qk * w for ea, p, qk, w in zip(eas, prods, q_k, ws)], axis=1)


def _ret_prompt_kernel(x_ref, g_ref, cos_ref, sin_ref, win_ref, wout_ref, *rest, heads, dk, dv, chunk, side_block):
    if side_block:
        side_in, (y_ref, sout_ref, side_o_ref, s_scr, og_scr) = rest[:4], rest[4:]
        _ret_readout(pl.program_id(0) * pl.num_programs(1) + pl.program_id(1), side_block, *side_in, side_o_ref,
                     heads=heads, dk=dk, dv=dv)
    else:
        y_ref, sout_ref, s_scr, og_scr = rest
    t = pl.program_id(1)
    half = dk // 2

    @pl.when(t == 0)
    def _():
        s_scr[...] = jnp.zeros_like(s_scr)

    x = x_ref[...]
    hn = _rms(x, g_ref[...]).astype(BF16)
    cos = cos_ref[...]
    sin = sin_ref[...]
    ri = lax.broadcasted_iota(jnp.int32, (chunk, chunk), 0)
    ci = lax.broadcasted_iota(jnp.int32, (chunk, chunk), 1)
    lag = (ri - ci).astype(F32)
    row = lax.broadcasted_iota(jnp.int32, (chunk, half), 0).astype(F32)
    k_off, v_off, g_off = heads * dk, 2 * heads * dk, 2 * heads * dk + heads * dv
    for h in range(heads):
        log_gamma = math.log(1.0 - 2.0 ** (-5.0 - h))
        q = _dot(hn, win_ref[:, h * dk:(h + 1) * dk])
        k = _dot(hn, win_ref[:, k_off + h * dk:k_off + (h + 1) * dk])
        v = _dot(hn, win_ref[:, v_off + h * dv:v_off + (h + 1) * dv]).astype(BF16)
        gate = _dot(hn, win_ref[:, g_off + h * dv:g_off + (h + 1) * dv])
        q1, q2 = q[:, :half], q[:, half:]
        k1, k2 = k[:, :half], k[:, half:]
        qr1, qr2 = q1 * cos - q2 * sin, q2 * cos + q1 * sin
        kr1, kr2 = (k1 * cos - k2 * sin) * dk ** -0.5, (k2 * cos + k1 * sin) * dk ** -0.5
        qb = jnp.concatenate([qr1, qr2], axis=1).astype(BF16)
        kb = jnp.concatenate([kr1, kr2], axis=1).astype(BF16)
        decay = jnp.where(ri >= ci, jnp.exp(lag * log_gamma), 0.0)
        p = (_dot_nt(qb, kb) * decay).astype(BF16)
        q_scale = jnp.exp((row + 1.0) * log_gamma)
        k_scale = jnp.exp((chunk - 1.0 - row) * log_gamma)
        qd = jnp.concatenate([qr1 * q_scale, qr2 * q_scale], axis=1).astype(BF16)
        kd = jnp.concatenate([kr1 * k_scale, kr2 * k_scale], axis=1)
        s = s_scr[h]
        o = _dot(p, v) + _dot(qd, s.astype(BF16))
        s_scr[h] = math.exp(chunk * log_gamma) * s + _dot(kd.T.astype(BF16), v)
        o = o * lax.rsqrt(jnp.mean(o * o, axis=-1, keepdims=True) + EPS) * _silu(gate)
        og_scr[:, h * dv:(h + 1) * dv] = o.astype(BF16)
    y_ref[...] = x + _dot(og_scr[...], wout_ref[...])

    @pl.when(t == pl.num_programs(1) - 1)
    def _():
        ro = lax.broadcasted_iota(jnp.int32, (dk, dk), 0)
        rin = lax.broadcasted_iota(jnp.int32, (dk, dk), 1)
        perm = (rin == (ro >> 1) + (ro & 1) * half).astype(BF16)
        for h in range(heads):
            sout_ref[0, h] = _exact_dot_left(perm, s_scr[h])


def _side_readout_specs(small, s0, layer, steps, step_of):
    _, n, heads, dk, dv = s0.shape
    assert n % steps == 0
    block = n // steps
    in_specs = [_resident(t.shape) for t in small]
    in_specs.append(pl.BlockSpec((1, block, heads, dk, dv), lambda *ids: (layer, step_of(*ids), 0, 0, 0)))
    o_spec = pl.BlockSpec((1, block, heads * dv), lambda *ids: (step_of(*ids), 0, 0))
    return block, in_specs, o_spec, jax.ShapeDtypeStruct((steps, block, heads * dv), F32)


def _ret_prompt(x, g, cos, sin, w_in, w_out, batch, heads, dk, dv, side=None):
    m, d = x.shape
    seq = m // batch
    chunk = min(RET_CHUNK, seq)
    assert seq % chunk == 0
    nt = seq // chunk
    args = [x, g, cos, sin, w_in, w_out]
    in_specs = [
        pl.BlockSpec((chunk, d), lambda b, t: (b * nt + t, 0)),
        _resident((1, d)),
        pl.BlockSpec((chunk, dk // 2), lambda b, t: (t, 0)),
        pl.BlockSpec((chunk, dk // 2), lambda b, t: (t, 0)),
        _resident(w_in.shape),
        _resident(w_out.shape),
    ]
    out_specs = [
        pl.BlockSpec((chunk, d), lambda b, t: (b * nt + t, 0)),
        pl.BlockSpec((1, heads, dk, dv), lambda b, t: (b, 0, 0, 0)),
    ]
    out_shape = [jax.ShapeDtypeStruct((m, d), F32), jax.ShapeDtypeStruct((batch, heads, dk, dv), F32)]
    side_block = 0
    if side is not None:
        small, s0, layer = side
        side_block, side_specs, o_spec, o_shape = _side_readout_specs(small, s0, layer, batch * nt,
                                                                      lambda b, t: b * nt + t)
        args += list(small) + [s0]
        in_specs += side_specs
        out_specs.append(o_spec)
        out_shape.append(o_shape)
    outs = pl.pallas_call(
        functools.partial(_ret_prompt_kernel, heads=heads, dk=dk, dv=dv, chunk=chunk, side_block=side_block),
        name="ret_prompt",
        grid=(batch, nt),
        in_specs=in_specs,
        out_specs=out_specs,
        out_shape=out_shape,
        scratch_shapes=[pltpu.VMEM((heads, dk, dv), F32), pltpu.VMEM((chunk, heads * dv), BF16)],
        compiler_params=_params("arbitrary", "arbitrary"),
    )(*args)
    return outs if side is None else (outs[0], outs[1], outs[2].reshape(-1, outs[2].shape[-1]))


def _ret_sample_proj_kernel(x_ref, g_ref, cos_ref, sin_ref, win_ref, wsw_ref, q_ref, k_ref, kt_ref, v_ref, gate_ref,
                            *, heads, dk, dv):
    qk_w = 2 * heads * dk
    hn = _rms(x_ref[...], g_ref[...]).astype(BF16)
    qk = _dot(hn, win_ref[:, :qk_w]) * cos_ref[...] + _dot(hn, wsw_ref[...]) * sin_ref[...]
    q_ref[...] = qk[:, :heads * dk]
    k_ref[...] = qk[:, heads * dk:] * dk ** -0.5
    kt_ref[...] = k_ref[...].T
    v_ref[...] = _dot(hn, win_ref[:, qk_w:qk_w + heads * dv])
    gate_ref[...] = _dot(hn, win_ref[:, qk_w + heads * dv:])


def _ret_sample_state_kernel(q_ref, *refs, heads, dk, dv, block):
    n_layers = (len(refs) - 2) // 3
    ins, (s_ref, o_ref) = refs[:3 * n_layers], refs[3 * n_layers:]
    n = q_ref.shape[0]
    lane = lax.broadcasted_iota(jnp.int32, (dk, n), 1)
    hs = range(heads)
    gammas = [1.0 - 2.0 ** (-5.0 - h) for h in hs]
    for i in range(block):
        b = pl.program_id(0) * block + i
        for layer in range(n_layers):
            kt_ref, v_ref, s0_ref = ins[3 * layer:3 * layer + 3]
            kt_bs = [jnp.where(lane == b, kt_ref[h * dk:(h + 1) * dk, :], 0.0).astype(BF16) for h in hs]
            s_news = [gamma * s0_ref[0, i, h] + _dot(kt_b, v_ref[:, h * dv:(h + 1) * dv].astype(BF16))
                      for h, gamma, kt_b in zip(hs, gammas, kt_bs)]
            for h in hs:
                s_ref[layer, i, h] = s_news[h]
        q_row = q_ref[pl.ds(b, 1), :]
        q8s = [jnp.broadcast_to(q_row[:, h * dk:(h + 1) * dk], (SUBLANES, dk)).astype(BF16) for h in hs]
        o_ref[0, i:i + 1, :] = jnp.concatenate(
            [_dot(q8, s_new.astype(BF16))[0:1] for q8, s_new in zip(q8s, s_news)], axis=1)


def _mixer_out_kernel(x_ref, o_ref, gate_ref, nw_ref, wout_ref, y_ref, *, heads, dv):
    parts = []
    for h in range(heads):
        o = o_ref[:, h * dv:(h + 1) * dv]
        o = o * lax.rsqrt(jnp.mean(o * o, axis=-1, keepdims=True) + EPS) * nw_ref[...]
        parts.append((o * _silu(gate_ref[:, h * dv:(h + 1) * dv])).astype(BF16))
    y_ref[...] = x_ref[...] + _dot(jnp.concatenate(parts, axis=1), wout_ref[...])


def _mixer_out(x, o, gate, norm_w, w_out, heads, dv):
    return pl.pallas_call(
        functools.partial(_mixer_out_kernel, heads=heads, dv=dv),
        name="mixer_out",
        out_shape=jax.ShapeDtypeStruct(x.shape, F32),
        compiler_params=_params(),
    )(x, o, gate, norm_w, w_out)


def _ret_sample_proj(x, g, cos, sin, w_in, w_sw, heads, dk, dv):
    n = x.shape[0]
    return pl.pallas_call(
        functools.partial(_ret_sample_proj_kernel, heads=heads, dk=dk, dv=dv),
        name="ret_sample_proj",
        out_shape=[
            jax.ShapeDtypeStruct((n, heads * dk), F32),
            jax.ShapeDtypeStruct((n, heads * dk), F32),
            jax.ShapeDtypeStruct((heads * dk, n), F32),
            jax.ShapeDtypeStruct((n, heads * dv), F32),
            jax.ShapeDtypeStruct((n, heads * dv), F32),
        ],
        compiler_params=_params(),
    )(x, g, cos, sin, w_in, w_sw)


def _ret_sample_final(q, history, s0):
    n = q.shape[0]
    layers, _, heads, dk, dv = s0.shape
    assert len(history) == layers
    block = math.gcd(RET_STATE_BLOCK, n)
    args, in_specs = [q], [_resident(q.shape)]
    for idx, (kt_l, v_l) in enumerate(history):
        args += [kt_l, v_l, s0]
        in_specs += [_resident(kt_l.shape), _resident(v_l.shape),
                     pl.BlockSpec((1, block, heads, dk, dv), lambda i, idx=idx: (idx, i, 0, 0, 0))]
    s_new, o = pl.pallas_call(
        functools.partial(_ret_sample_state_kernel, heads=heads, dk=dk, dv=dv, block=block),
        name="ret_sample_state",
        grid=(n // block,),
        in_specs=in_specs,
        out_specs=[pl.BlockSpec((layers, block, heads, dk, dv), lambda i: (0, i, 0, 0, 0)),
                   pl.BlockSpec((1, block, heads * dv), lambda i: (i, 0, 0))],
        out_shape=[jax.ShapeDtypeStruct(s0.shape, F32), jax.ShapeDtypeStruct((n // block, block, heads * dv), F32)],
        compiler_params=_params("arbitrary"),
    )(*args)
    return s_new, o.reshape(n, heads * dv)


def _unit_lower_inverse_minus_identity(lms, ri, ci, chunk):
    def level_mask(lev):
        return ((ri >> lev) == (ci >> lev)) & ((ri >> (lev - 1)) != (ci >> (lev - 1)))

    mask = level_mask(1)
    ns = [jnp.where(mask, -lm, 0.0) for lm in lms]
    for lev in range(2, chunk.bit_length()):
        mask = level_mask(lev)
        cs = [jnp.where(mask, lm, 0.0) for lm in lms]
        nbs = [n.astype(BF16) for n in ns]
        gs = [c + cn for c, cn in zip(cs, _dots([c.astype(BF16) for c in cs], nbs))]
        ns = [n - g - ng for n, g, ng in zip(ns, gs, _dots(nbs, [g.astype(BF16) for g in gs]))]
    return ns


def _gdn_prompt_kernel(x_ref, g_ref, cbuf_ref, win_ref, cw_ref, alog_ref, dtb_ref, nw_ref, wout_ref, *rest,
                       heads, dk, dv, rows, chunk, side_block):
    if side_block:
        side_in, (y_ref, sout_ref, cout_ref, side_o_ref, s_scr, xh_scr, uh_scr, og_scr) = rest[:6], rest[6:]
        _gdn_readout(pl.program_id(0) * pl.num_programs(1) + pl.program_id(1), side_block, *side_in, side_o_ref,
                     heads=heads, dk=dk, dv=dv)
    else:
        y_ref, sout_ref, cout_ref, s_scr, xh_scr, uh_scr, og_scr = rest
    t = pl.program_id(1)
    qkv_w = heads * (2 * dk + dv)
    hd = heads * dv
    cw = [cw_ref[j:j + 1, :] for j in range(4)]

    @pl.when(t == 0)
    def _():
        s_scr[...] = jnp.zeros_like(s_scr)
        xh_scr[...] = jnp.zeros_like(xh_scr)
        uh_scr[...] = jnp.zeros_like(uh_scr)
        xh_scr[0:1, :] = cbuf_ref[0, 2:3, :]
        uh_scr[0:1, :] = cw[1] * cbuf_ref[0, 1:2, :] + cw[0] * cbuf_ref[0, 0:1, :]
        uh_scr[1:2, :] = cw[1] * cbuf_ref[0, 2:3, :] + cw[0] * cbuf_ref[0, 1:2, :]

    x = x_ref[...]
    hn = _rms(x, g_ref[...]).astype(BF16)
    qkv = _dot(hn, win_ref[:, :qkv_w])
    sub = lax.broadcasted_iota(jnp.int32, (SUBLANES, qkv_w), 0)

    def shifted(cur, by, halo_tile):
        rolled = pltpu.roll(cur, by, 0)
        first = jnp.where(sub < by, halo_tile, rolled[:SUBLANES])
        return jnp.concatenate([first, rolled[SUBLANES:]], axis=0)

    x_prev = shifted(qkv, 1, xh_scr[...])
    u = cw[1] * qkv + cw[0] * x_prev
    conv = cw[3] * qkv + cw[2] * x_prev + shifted(u, 2, uh_scr[...])
    xh_scr[0:1, :] = qkv[rows - 1:rows, :]
    uh_scr[0:2, :] = u[rows - 2:rows, :]

    @pl.when(t == pl.num_programs(1) - 1)
    def _():
        cout_ref[0] = qkv[rows - 3:rows, :]

    act = _silu(conv)
    z = _dot(hn, win_ref[:, qkv_w:qkv_w + hd])
    ba = _dot(hn, win_ref[:, qkv_w + hd:])
    a = -jnp.exp(alog_ref[...]) * _softplus(ba + dtb_ref[...])
    ri = lax.broadcasted_iota(jnp.int32, (chunk, chunk), 0)
    ci = lax.broadcasted_iota(jnp.int32, (chunk, chunk), 1)
    tril = (ri >= ci).astype(BF16)
    subs = range(rows // chunk)
    bcum = jnp.concatenate([_exact_dot_left(tril, a[c * chunk:(c + 1) * chunk]) for c in subs], axis=0)
    beta = _expand_heads(jax.nn.sigmoid(ba), 0, heads, dv)
    bcum = _expand_heads(bcum, heads, heads, dv)
    probs = [(c, h) for c in subs for h in range(heads)]

    def tile(arr, c, col, width):
        return arr[c * chunk:(c + 1) * chunk, col:col + width]

    qs = [tile(act, c, h * dk, dk) for c, h in probs]
    ks = [tile(act, c, heads * dk + h * dk, dk) for c, h in probs]
    vs = [tile(act, c, 2 * heads * dk + h * dv, dv) for c, h in probs]
    qs = [q * lax.rsqrt(jnp.sum(q * q, axis=-1, keepdims=True) + EPS) * dk ** -0.5 for q in qs]
    ks = [k * lax.rsqrt(jnp.sum(k * k, axis=-1, keepdims=True) + EPS) for k in ks]
    bts = [tile(beta, c, h * dv, dv) for c, h in probs]
    bs = [tile(bcum, c, h * dv, dv) for c, h in probs]
    es = [jnp.exp(jnp.minimum(b - b.T, 0.0)) for b in bs]
    kbs = [k.astype(BF16) for k in ks]
    qk_kks = _dots([jnp.concatenate([q.astype(BF16), kb], axis=0) for q, kb in zip(qs, kbs)], kbs, nt=True)
    ps = [jnp.where(ri >= ci, qk_kk[:chunk] * e, 0.0).astype(BF16) for qk_kk, e in zip(qk_kks, es)]
    lms = [jnp.where(ri > ci, qk_kk[chunk:] * e, 0.0) * bt for qk_kk, e, bt in zip(qk_kks, es, bts)]
    ns = [n.astype(BF16) for n in _unit_lower_inverse_minus_identity(lms, ri, ci, chunk)]
    kq_es = [jnp.concatenate([(k * jnp.exp(b)).astype(BF16), (q * jnp.exp(b)).astype(BF16)], axis=0)
             for k, q, b in zip(ks, qs, bs)]
    b_lasts = [b[chunk - 1:chunk, :] for b in bs]
    kdts = [(k * jnp.exp(b_last - b)).T.astype(BF16) for k, b, b_last in zip(ks, bs, b_lasts)]
    ss = [s_scr[h] for h in range(heads)]
    for c in subs:
        sel = slice(c * heads, (c + 1) * heads)
        kq_ss = _dots(kq_es[sel], [s.astype(BF16) for s in ss])
        rs = [bt * (v - kq_s[:chunk]) for bt, v, kq_s in zip(bts[sel], vs[sel], kq_ss)]
        wbs = [(r + nr).astype(BF16) for r, nr in zip(rs, _dots(ns[sel], [r.astype(BF16) for r in rs]))]
        os_ = [kq_s[chunk:] + pw for kq_s, pw in zip(kq_ss, _dots(ps[sel], wbs))]
        ss = [jnp.exp(b_last) * s + kw for b_last, s, kw in zip(b_lasts[sel], ss, _dots(kdts[sel], wbs))]
        for h in range(heads):
            o = os_[h]
            o = o * lax.rsqrt(jnp.mean(o * o, axis=-1, keepdims=True) + EPS) * nw_ref[...]
            og_scr[c * chunk:(c + 1) * chunk, h * dv:(h + 1) * dv] = (o * _silu(tile(z, c, h * dv, dv))).astype(BF16)
    for h in range(heads):
        s_scr[h] = ss[h]
    y_ref[...] = x + _dot(og_scr[...], wout_ref[...])

    @pl.when(t == pl.num_programs(1) - 1)
    def _():
        sout_ref[0] = s_scr[...]


def _gdn_prompt(x, g, cbuf, w_in, conv_w, a_log, dt_bias, norm_w, w_out, batch, heads, dk, dv, side=None):
    m, d = x.shape
    seq = m // batch
    chunk = GDN_CHUNK
    taps = conv_w.shape[0]
    qkv_w = heads * (2 * dk + dv)
    rows = math.gcd(GDN_ROWS, seq)
    assert rows % chunk == 0 and dk == chunk and dv == chunk and chunk == LANES and 2 * heads <= LANES and taps == 4
    nt = seq // rows
    args = [x, g, cbuf, w_in, conv_w, a_log, dt_bias, norm_w, w_out]
    in_specs = [
        pl.BlockSpec((rows, d), lambda b, t: (b * nt + t, 0)),
        _resident((1, d)),
        pl.BlockSpec((1, taps - 1, qkv_w), lambda b, t: (b, 0, 0)),
        _resident(w_in.shape),
        _resident(conv_w.shape),
        _resident(a_log.shape),
        _resident(dt_bias.shape),
        _resident(norm_w.shape),
        _resident(w_out.shape),
    ]
    out_specs = [
        pl.BlockSpec((rows, d), lambda b, t: (b * nt + t, 0)),
        pl.BlockSpec((1, heads, dk, dv), lambda b, t: (b, 0, 0, 0)),
        pl.BlockSpec((1, taps - 1, qkv_w), lambda b, t: (b, 0, 0)),
    ]
    out_shape = [
        jax.ShapeDtypeStruct((m, d), F32),
        jax.ShapeDtypeStruct((batch, heads, dk, dv), F32),
        jax.ShapeDtypeStruct((batch, taps - 1, qkv_w), F32),
    ]
    side_block = 0
    if side is not None:
        small, s0, layer = side
        side_block, side_specs, o_spec, o_shape = _side_readout_specs(small, s0, layer, batch * nt,
                                                                      lambda b, t: b * nt + t)
        args += list(small) + [s0]
        in_specs += side_specs
        out_specs.append(o_spec)
        out_shape.append(o_shape)
    outs = pl.pallas_call(
        functools.partial(_gdn_prompt_kernel, heads=heads, dk=dk, dv=dv, rows=rows, chunk=chunk,
                          side_block=side_block),
        name="gdn_prompt",
        grid=(batch, nt),
        in_specs=in_specs,
        out_specs=out_specs,
        out_shape=out_shape,
        scratch_shapes=[
            pltpu.VMEM((heads, dk, dv), F32),
            pltpu.VMEM((SUBLANES, qkv_w), F32),
            pltpu.VMEM((SUBLANES, qkv_w), F32),
            pltpu.VMEM((rows, heads * dv), BF16),
        ],
        compiler_params=_params("arbitrary", "arbitrary"),
    )(*args)
    return outs if side is None else (outs[0], outs[1], outs[2], outs[3].reshape(-1, outs[3].shape[-1]))


def _gdn_sample_proj_kernel(x_ref, g_ref, cbuf_ref, win_ref, cw_ref, alog_ref, dtb_ref,
                            q_ref, kt_ref, k_ref, v_ref, z_ref, beta_ref, ea_ref, cout_ref, *, heads, dk, dv, taps):
    qkv_w = heads * (2 * dk + dv)
    hd = heads * dv
    halo = taps - 1
    hn = _rms(x_ref[...], g_ref[...]).astype(BF16)
    qkv = _dot(hn, win_ref[:, :qkv_w])
    conv = qkv * cw_ref[halo:taps, :]
    for j in range(halo):
        tap = cbuf_ref[:, j * qkv_w:(j + 1) * qkv_w]
        conv = conv + tap * cw_ref[j:j + 1, :]
        if j > 0:
            cout_ref[:, (j - 1) * qkv_w:j * qkv_w] = tap
    cout_ref[:, (halo - 1) * qkv_w:] = qkv
    act = _silu(conv)
    for h in range(heads):
        q = act[:, h * dk:(h + 1) * dk]
        k = act[:, heads * dk + h * dk:heads * dk + (h + 1) * dk]
        q_ref[:, h * dk:(h + 1) * dk] = q * lax.rsqrt(jnp.sum(q * q, axis=-1, keepdims=True) + EPS) * dk ** -0.5
        k_ref[:, h * dk:(h + 1) * dk] = k * lax.rsqrt(jnp.sum(k * k, axis=-1, keepdims=True) + EPS)
    kt_ref[...] = k_ref[...].T
    v_ref[...] = act[:, 2 * heads * dk:]
    z_ref[...] = _dot(hn, win_ref[:, qkv_w:qkv_w + hd])
    ba = _dot(hn, win_ref[:, qkv_w + hd:])
    beta_ref[...] = _expand_heads(jax.nn.sigmoid(ba), 0, heads, dv)
    a = -jnp.exp(alog_ref[...]) * _softplus(ba + dtb_ref[...])
    ea_ref[...] = jnp.exp(_expand_heads(a, heads, heads, dv))


def _gdn_sample_state_kernel(q_ref, *refs, heads, dk, dv, block):
    n_layers = (len(refs) - 2) // 6
    ins, (s_ref, o_ref) = refs[:6 * n_layers], refs[6 * n_layers:]
    n = q_ref.shape[0]
    lane = lax.broadcasted_iota(jnp.int32, (dk, n), 1)
    hs = range(heads)
    for i in range(block):
        b = pl.program_id(0) * block + i
        for layer in range(n_layers):
            kt_ref, k_ref, v_ref, beta_ref, ea_ref, s0_ref = ins[6 * layer:6 * layer + 6]
            k_row, v_row = k_ref[pl.ds(b, 1), :], v_ref[pl.ds(b, 1), :]
            beta_row, ea_row = beta_ref[pl.ds(b, 1), :], ea_ref[pl.ds(b, 1), :]
            s0s = [s0_ref[0, i, h] for h in hs]
            eas = [ea_row[:, h * dv:(h + 1) * dv] for h in hs]
            k8s = [jnp.broadcast_to(k_row[:, h * dk:(h + 1) * dk] * ea, (SUBLANES, dk)).astype(BF16)
                   for h, ea in zip(hs, eas)]
            kss = [_dot(k8, s0.astype(BF16))[0:1] for k8, s0 in zip(k8s, s0s)]
            ws = [beta_row[:, h * dv:(h + 1) * dv] * (v_row[:, h * dv:(h + 1) * dv] - ks) for h, ks in zip(hs, kss)]
            kt_bs = [jnp.where(lane == b, kt_ref[h * dk:(h + 1) * dk, :], 0.0).astype(BF16) for h in hs]
            s_news = [ea * s0 + _dot(kt_b, jnp.broadcast_to(w, (n, dv)).astype(BF16))
                      for ea, s0, kt_b, w in zip(eas, s0s, kt_bs, ws)]
            for h in hs:
                s_ref[layer, i, h] = s_news[h]
        q_row = q_ref[pl.ds(b, 1), :]
        q8s = [jnp.broadcast_to(q_row[:, h * dk:(h + 1) * dk], (SUBLANES, dk)).astype(BF16) for h in hs]
        o_ref[i:i + 1, :] = jnp.concatenate(
            [_dot(q8, s_new.astype(BF16))[0:1] for q8, s_new in zip(q8s, s_news)], axis=1)


def _gdn_sample_proj(x, g, cbuf, w_in, conv_w, a_log, dt_bias, heads, dk, dv):
    n = x.shape[0]
    taps = conv_w.shape[0]
    qkv_w = heads * (2 * dk + dv)
    hd = heads * dv
    q, kt, k, v, z, beta, ea, cout = pl.pallas_call(
        functools.partial(_gdn_sample_proj_kernel, heads=heads, dk=dk, dv=dv, taps=taps),
        name="gdn_sample_proj",
        out_shape=[
            jax.ShapeDtypeStruct((n, heads * dk), F32),
            jax.ShapeDtypeStruct((heads * dk, n), F32),
            jax.ShapeDtypeStruct((n, heads * dk), F32),
            jax.ShapeDtypeStruct((n, hd), F32),
            jax.ShapeDtypeStruct((n, hd), F32),
            jax.ShapeDtypeStruct((n, hd), F32),
            jax.ShapeDtypeStruct((n, hd), F32),
            jax.ShapeDtypeStruct((n, (taps - 1) * qkv_w), F32),
        ],
        compiler_params=_params(),
    )(x, g, cbuf.reshape(n, (taps - 1) * qkv_w), w_in, conv_w, a_log, dt_bias)
    return q, kt, k, v, z, beta, ea, cout.reshape(n, taps - 1, qkv_w)


def _gdn_sample_final(q, history, s0):
    n = q.shape[0]
    layers, _, heads, dk, dv = s0.shape
    assert len(history) == layers
    hd = heads * dv
    block = math.gcd(GDN_STATE_BLOCK, n)
    args, in_specs = [q], [_resident(q.shape)]
    for idx, small in enumerate(history):
        args += list(small) + [s0]
        in_specs += [_resident(t.shape) for t in small]
        in_specs.append(pl.BlockSpec((1, block, heads, dk, dv), lambda i, idx=idx: (idx, i, 0, 0, 0)))
    return pl.pallas_call(
        functools.partial(_gdn_sample_state_kernel, heads=heads, dk=dk, dv=dv, block=block),
        name="gdn_sample_state",
        grid=(n // block,),
        in_specs=in_specs,
        out_specs=[pl.BlockSpec((layers, block, heads, dk, dv), lambda i: (0, i, 0, 0, 0)),
                   pl.BlockSpec((block, hd), lambda i: (i, 0))],
        out_shape=[jax.ShapeDtypeStruct(s0.shape, F32), jax.ShapeDtypeStruct((n, hd), F32)],
        compiler_params=_params("arbitrary"),
    )(*args)


def _rope_tables(pos, dk):
    theta = 1.0 / (ROPE_BASE ** jnp.linspace(0.0, 1.0, dk // 2, dtype=F32))
    ang = pos[:, None] * theta[None, :]
    return jnp.cos(ang), jnp.sin(ang)


def _ret_weight_prep_kernel(w_ref, w_out_ref, deint_ref, swap_ref, *, heads, dk):
    qk_w = 2 * heads * dk
    half = dk // 2
    w = w_ref[...].astype(BF16)
    w_out_ref[...] = w
    deint_ref[:, qk_w:] = w[:, qk_w:]
    src = lax.broadcasted_iota(jnp.int32, (dk, dk), 0)
    dst = lax.broadcasted_iota(jnp.int32, (dk, dk), 1)
    to_halves = (src == jnp.where(dst < half, 2 * dst, 2 * (dst - half) + 1)).astype(BF16)
    swap_signed = jnp.where(src == (dst ^ 1), jnp.where((dst & 1) == 0, -1.0, 1.0), 0.0).astype(BF16)
    for h in range(2 * heads):
        block = w[:, h * dk:(h + 1) * dk]
        deint_ref[:, h * dk:(h + 1) * dk] = _dot(block, to_halves).astype(BF16)
        swap_ref[:, h * dk:(h + 1) * dk] = _dot(block, swap_signed).astype(BF16)


def _prep_ret_weights(w_in_all, layer, heads, dk):
    _, d, cols = w_in_all.shape
    qk_w = 2 * heads * dk
    rows = math.gcd(256, d)
    return pl.pallas_call(
        functools.partial(_ret_weight_prep_kernel, heads=heads, dk=dk),
        name="ret_weight_prep",
        grid=(d // rows,),
        in_specs=[pl.BlockSpec((None, rows, cols), lambda i: (layer, i, 0))],
        out_specs=[pl.BlockSpec((rows, cols), lambda i: (i, 0)), pl.BlockSpec((rows, cols), lambda i: (i, 0)),
                   pl.BlockSpec((rows, qk_w), lambda i: (i, 0))],
        out_shape=[jax.ShapeDtypeStruct((d, cols), BF16), jax.ShapeDtypeStruct((d, cols), BF16),
                   jax.ShapeDtypeStruct((d, qk_w), BF16)],
        compiler_params=_params("arbitrary"),
    )(w_in_all)


def _prep_gdn_weights(w_in, a_log, dt_bias, heads, dk, dv):
    pad = LANES - 2 * heads
    w = jnp.pad(w_in, ((0, 0), (0, pad))).astype(BF16)
    place = lambda p: jnp.pad(p, (heads, pad))[None, :]
    return w, place(a_log), place(dt_bias)


def _trunk(x_p, x_s, batch_p, s_ret, s_gdn, s_conv, norm_g, ffn_gu, ffn_down, ret_w, ret_w_out, gdn_w, gdn_conv_w,
           gdn_norm_w, gdn_w_out, final_g):
    depth = norm_g.shape[0]
    n_mixers = 2
    seq = x_p.shape[0] // batch_p
    n_ret, _, ret_heads, ret_dk, ret_dv = s_ret.shape
    n_gdn, _, gdn_heads, gdn_dk, gdn_dv = s_gdn.shape
    cos_p, sin_p = _rope_tables(jnp.arange(seq, dtype=F32), ret_dk)
    cos_s, sin_s = (jnp.tile(jnp.repeat(t, 2, axis=1), (1, 2 * ret_heads))
                    for t in _rope_tables(jnp.full((1,), SAMPLE_PAST_LEN, F32), ret_dk))
    ones_dv = jnp.ones((1, ret_dv), F32)
    fg = final_g[None, :]
    norm_g4 = norm_g[:, :, None, :]
    zero_conv = jnp.zeros((batch_p,) + s_conv.shape[2:], s_conv.dtype)
    ffn_w = (ffn_gu[0, 0].astype(BF16), ffn_down[0, 0].astype(BF16))
    ret_p, gdn_p, conv_p, conv_s = [], [], [], []
    ret_s = gdn_s = None
    ret_hist, gdn_hist = [], []
    for i in range(depth):
        j = i // n_mixers
        for which, which_norm in ((0, 0), (1, 2)):
            if which == 1:
                g = norm_g[i, 1][None, :]
                if i % n_mixers == 0:
                    w_orig, w_deint, w_swap = ret_w[j]
                    q, k, kt, v, gate = _ret_sample_proj(x_s, g, cos_s, sin_s, w_orig, w_swap, ret_heads, ret_dk,
                                                         ret_dv)
                    ret_hist.append((kt, v))
                    prompt_args = (x_p, g, cos_p, sin_p, w_deint, ret_w_out[j], batch_p, ret_heads, ret_dk, ret_dv)
                    if j == n_ret - 1:
                        x_p, s = _ret_prompt(*prompt_args)
                        ret_s, o = _ret_sample_final(q, ret_hist, s_ret)
                    else:
                        x_p, s, o = _ret_prompt(*prompt_args, side=((q, k, v), s_ret, j))
                    ret_p.append(s)
                    x_s = _mixer_out(x_s, o, gate, ones_dv, ret_w_out[j], ret_heads, ret_dv)
                else:
                    w, a_log, dt_bias = gdn_w[j]
                    nw = gdn_norm_w[j][None, :]
                    q, kt, k, v, z, beta, ea, buf = _gdn_sample_proj(x_s, g, s_conv[j], w, gdn_conv_w[j], a_log,
                                                                     dt_bias, gdn_heads, gdn_dk, gdn_dv)
                    conv_s.append(buf)
                    gdn_hist.append((kt, k, v, beta, ea))
                    prompt_args = (x_p, g, zero_conv, w, gdn_conv_w[j], a_log, dt_bias, nw, gdn_w_out[j], batch_p,
                                   gdn_heads, gdn_dk, gdn_dv)
                    if j == n_gdn - 1:
                        x_p, s, buf = _gdn_prompt(*prompt_args)
                        gdn_s, o = _gdn_sample_final(q, gdn_hist, s_gdn)
                    else:
                        x_p, s, buf, o = _gdn_prompt(*prompt_args, side=((q, k, v, beta, ea), s_gdn, j))
                    gdn_p.append(s)
                    conv_p.append(buf)
                    x_s = _mixer_out(x_s, o, z, nw, gdn_w_out[j], gdn_heads, gdn_dv)
            nxt = (i, 1) if which == 0 else (i + 1, 0)
            cast_next = (ffn_gu, ffn_down) + nxt if nxt[0] < depth else None
            x_p, x_s, made = _ffn(x_p, x_s, norm_g4, *ffn_w, fg, i, which_norm, which == 1 and i == depth - 1,
                                  cast_next)
            ffn_w = made
    return (x_p, x_s, jnp.stack(ret_p), jnp.stack(gdn_p), jnp.stack(conv_p), ret_s, gdn_s, jnp.stack(conv_s))


def kernel(x_prompt, x_sample, state_ret, state_gdn, state_gdn_conv, norm_g, ffn_gu, ffn_down, ret_w_in,
           ret_w_out, gdn_w_in, gdn_conv_w, gdn_a_log, gdn_dt_bias, gdn_norm_w, gdn_w_out, final_g):
    bp, seq, d = x_prompt.shape
    bs, dec_seq, _ = x_sample.shape
    assert dec_seq == 1
    n_ret, _, ret_heads, ret_dk, _ = state_ret.shape
    n_gdn, _, gdn_heads, gdn_dk, gdn_dv = state_gdn.shape
    ret_w = [_prep_ret_weights(ret_w_in, j, ret_heads, ret_dk) for j in range(n_ret)]
    gdn_w = [_prep_gdn_weights(gdn_w_in[j], gdn_a_log[j], gdn_dt_bias[j], gdn_heads, gdn_dk, gdn_dv)
             for j in range(n_gdn)]
    y_p, y_s, ret_p, gdn_p, conv_p, ret_s, gdn_s, conv_s = _trunk(
        x_prompt.reshape(bp * seq, d), x_sample.reshape(bs, d), bp, state_ret, state_gdn, state_gdn_conv, norm_g,
        ffn_gu, ffn_down, ret_w, ret_w_out.astype(BF16), gdn_w, gdn_conv_w, gdn_norm_w, gdn_w_out.astype(BF16),
        final_g)
    return (y_p.reshape(bp, seq, d), y_s.reshape(bs, dec_seq, d), ret_p, gdn_p, conv_p, ret_s, gdn_s, conv_s)
```

```python
import functools
import math

import jax
import jax.numpy as jnp
from jax import lax
from jax.experimental import pallas as pl
from jax.experimental.pallas import tpu as pltpu

F32 = jnp.float32
BF16 = jnp.bfloat16
EPS = 1e-6
ROPE_BASE = 10000.0
SAMPLE_PAST_LEN = 16384.0

LANES = 128
SUBLANES = 8
VMEM_LIMIT_BYTES = 56 * 1024 * 1024

FFN_ROWS = 512
RET_CHUNK = 256
GDN_CHUNK = 128
GDN_ROWS = 256
RET_STATE_BLOCK = 2
GDN_STATE_BLOCK = 8


def _dot(a, b):
    return jnp.dot(a, b, preferred_element_type=F32)


def _dot_nt(a, b):
    return lax.dot_general(a, b, (((1,), (1,)), ((), ())), preferred_element_type=F32)


def _rms(x, g):
    return x * lax.rsqrt(jnp.mean(x * x, axis=-1, keepdims=True) + EPS) * g


def _silu(x):
    h = 0.5 * x
    return h * jnp.tanh(h) + h


def _softplus(x):
    return jnp.maximum(x, 0.0) + jnp.log1p(jnp.exp(-jnp.abs(x)))


def _bf16_terms(x):
    t0 = x.astype(BF16)
    r = x - t0.astype(F32)
    t1 = r.astype(BF16)
    t2 = (r - t1.astype(F32)).astype(BF16)
    return t0, t1, t2


def _exact_dot_right(x, sel):
    return _dot(jnp.concatenate(_bf16_terms(x), axis=1), jnp.concatenate([sel] * 3, axis=0))


def _exact_dot_left(sel, x):
    return _dot(jnp.concatenate([sel] * 3, axis=1), jnp.concatenate(_bf16_terms(x), axis=0))


def _expand_heads(x, first, heads, width):
    c = lax.broadcasted_iota(jnp.int32, (x.shape[1], heads * width), 0)
    lane = lax.broadcasted_iota(jnp.int32, (x.shape[1], heads * width), 1)
    e = ((lane >= (c - first) * width) & (lane < (c - first + 1) * width)).astype(BF16)
    return _exact_dot_right(x, e)


def _dots(lhs, rhs, nt=False):
    dot = _dot_nt if nt else _dot
    return [dot(a, b) for a, b in zip(lhs, rhs)]


def _resident(shape):
    nd = len(shape)
    return pl.BlockSpec(shape, lambda *_: (0,) * nd, pipeline_mode=pl.Buffered(1))


def _params(*sem):
    return pltpu.CompilerParams(dimension_semantics=sem, vmem_limit_bytes=VMEM_LIMIT_BYTES)


def _ffn_kernel(x_ref, small_ref, g_ref, wg_ref, wu_ref, wd_ref, fg_ref, *rest, final, cast_next):
    def ffn(x):
        h = _rms(x, g_ref[...]).astype(BF16)
        a = _dot(h, wg_ref[...])
        b = _dot(h, wu_ref[...])
        act = (_silu(a) * b).astype(BF16)
        y = x + 0.5 * _dot(act, wd_ref[...])
        return _rms(y, fg_ref[...]) if final else y

    o_ref, small_out_ref = rest[-4:-2] if cast_next else rest
    o_ref[...] = ffn(x_ref[...])

    @pl.when(pl.program_id(0) == pl.num_programs(0) - 1)
    def _():
        small_out_ref[...] = ffn(small_ref[...])

    if cast_next:
        next_gu_ref, next_down_ref = rest[:2]
        gu_out_ref, down_out_ref = rest[-2:]
        gu_out_ref[...] = next_gu_ref[...].astype(BF16)
        down_out_ref[...] = next_down_ref[...].astype(BF16)


def _slab_count(rows, steps):
    return max(n for n in range(1, steps + 1) if steps % n == 0 and rows % (16 * n) == 0)


def _ffn(x, small, norm_g, w_gu, w_down, final_g, layer, which_norm, final, cast_next=None):
    m, d = x.shape
    f = w_down.shape[0]
    tm = min(FFN_ROWS, m)
    steps = m // tm
    assert m % tm == 0 and f % LANES == 0
    in_specs = [
        pl.BlockSpec((tm, d), lambda i: (i, 0)),
        _resident(small.shape),
        pl.BlockSpec((None, None, 1, d), lambda i: (layer, which_norm, 0, 0), pipeline_mode=pl.Buffered(1)),
        pl.BlockSpec((d, f), lambda i: (0, 0), pipeline_mode=pl.Buffered(1)),
        pl.BlockSpec((d, f), lambda i: (0, 1), pipeline_mode=pl.Buffered(1)),
        _resident((f, d)),
        _resident((1, d)),
    ]
    args = [x, small, norm_g, w_gu, w_gu, w_down, final_g]
    out_specs = [pl.BlockSpec((tm, d), lambda i: (i, 0)), pl.BlockSpec(small.shape, lambda i: (0, 0))]
    out_shape = [jax.ShapeDtypeStruct((m, d), F32), jax.ShapeDtypeStruct(small.shape, F32)]
    if cast_next is not None:
        next_gu, next_down, nl, nw = cast_next
        for w, rows, cols in ((next_gu, d, 2 * f), (next_down, f, d)):
            n = _slab_count(rows, steps)
            in_specs.append(pl.BlockSpec((None, None, rows // n, cols),
                                         lambda i, n=n: (nl, nw, i * n // steps, 0)))
            out_specs.append(pl.BlockSpec((rows // n, cols), lambda i, n=n: (i * n // steps, 0)))
            out_shape.append(jax.ShapeDtypeStruct((rows, cols), BF16))
            args.append(w)
    outs = pl.pallas_call(
        functools.partial(_ffn_kernel, final=final, cast_next=cast_next is not None),
        name="ffn",
        grid=(steps,),
        in_specs=in_specs,
        out_specs=out_specs,
        out_shape=out_shape,
        compiler_params=_params("arbitrary"),
    )(*args)
    return outs[0], outs[1], tuple(outs[2:])


def _ret_update_rows(b, q_ref, kt_ref, v_ref, s0_of_head, *, heads, dk, dv):
    n = v_ref.shape[0]
    lane = lax.broadcasted_iota(jnp.int32, (dk, n), 1)
    hs = range(heads)
    kt_bs = [jnp.where(lane == b, kt_ref[h * dk:(h + 1) * dk, :], 0.0).astype(BF16) for h in hs]
    s_news = [(1.0 - 2.0 ** (-5.0 - h)) * s0_of_head(h) + _dot(kt_b, v_ref[:, h * dv:(h + 1) * dv].astype(BF16))
              for h, kt_b in zip(hs, kt_bs)]
    q_row = q_ref[pl.ds(b, 1), :]
    q8s = [jnp.broadcast_to(q_row[:, h * dk:(h + 1) * dk], (SUBLANES, dk)).astype(BF16) for h in hs]
    o_row = jnp.concatenate([_dot(q8, s_new.astype(BF16))[0:1] for q8, s_new in zip(q8s, s_news)], axis=1)
    return s_news, o_row


def _ret_side_update(step, n_steps, block, q_ref, kt_ref, v_ref, s0_ref, s0_any, o_ref, snew_any, stage, sems,
                     *, heads, dk, dv, layer):
    layers = s0_any.shape[0]
    slot = step % 2

    def copies(at_step, at_slot):
        rows = pl.ds(at_step * block, block)
        own = pltpu.make_async_copy(stage.at[at_slot], snew_any.at[layer, rows], sems.at[0, at_slot])
        others = [pltpu.make_async_copy(s0_any.at[other, rows], snew_any.at[other, rows], sems.at[1 + i, at_slot])
                  for i, other in enumerate(l for l in range(layers) if l != layer)]
        return [own] + others

    @pl.when(step >= 2)
    def _():
        for c in copies(step - 2, slot):
            c.wait()

    for i in range(block):
        s_news, o_row = _ret_update_rows(step * block + i, q_ref, kt_ref, v_ref, lambda h: s0_ref[0, i, h],
                                         heads=heads, dk=dk, dv=dv)
        for h in range(heads):
            stage[slot, i, h] = s_news[h]
        o_ref[0, i:i + 1, :] = o_row
    for c in copies(step, slot):
        c.start()

    @pl.when(step == n_steps - 1)
    def _():
        for c in copies(step - 1, 1 - slot) + copies(step, slot):
            c.wait()


def _gdn_readout(step, block, q_ref, k_ref, v_ref, beta_ref, ea_ref, s0_ref, o_ref, *, heads, dk, dv):
    hs = range(heads)
    row = lax.broadcasted_iota(jnp.int32, (SUBLANES, dk), 0)
    for i in range(block):
        b = step * block + i
        q_row, k_row, v_row = q_ref[pl.ds(b, 1), :], k_ref[pl.ds(b, 1), :], v_ref[pl.ds(b, 1), :]
        beta_row, ea_row = beta_ref[pl.ds(b, 1), :], ea_ref[pl.ds(b, 1), :]
        q_hs = [q_row[:, h * dk:(h + 1) * dk] for h in hs]
        k_hs = [k_row[:, h * dk:(h + 1) * dk] for h in hs]
        eas = [ea_row[:, h * dv:(h + 1) * dv] for h in hs]
        lhs = [jnp.where(row == 0, q_h, jnp.where(row == 1, k_h * ea, 0.0)).astype(BF16)
               for q_h, k_h, ea in zip(q_hs, k_hs, eas)]
        prods = [_dot(l, s0_ref[0, i, h].astype(BF16)) for h, l in zip(hs, lhs)]
        ws = [beta_row[:, h * dv:(h + 1) * dv] * (v_row[:, h * dv:(h + 1) * dv] - p[1:2]) for h, p in zip(hs, prods)]
        q_k = [jnp.sum(q_h * k_h, axis=-1, keepdims=True) for q_h, k_h in zip(q_hs, k_hs)]
        o_ref[0, i:i + 1, :] = jnp.concatenate(
            [ea * p[0:1] + qk * w for ea, p, qk, w in zip(eas, prods, q_k, ws)], axis=1)


def _ret_prompt_kernel(x_ref, g_ref, cos_ref, sin_ref, win_ref, wout_ref, *rest, heads, dk, dv, chunk, side_block,
                       side_layer):
    if side_block:
        side_in, (y_ref, sout_ref, side_o_ref, side_s_any, s_scr, og_scr, stage, sems) = rest[:5], rest[5:]
        _ret_side_update(pl.program_id(0) * pl.num_programs(1) + pl.program_id(1),
                         pl.num_programs(0) * pl.num_programs(1), side_block, *side_in, side_o_ref, side_s_any,
                         stage, sems, heads=heads, dk=dk, dv=dv, layer=side_layer)
    else:
        y_ref, sout_ref, s_scr, og_scr = rest
    t = pl.program_id(1)
    half = dk // 2

    @pl.when(t == 0)
    def _():
        s_scr[...] = jnp.zeros_like(s_scr)

    x = x_ref[...]
    hn = _rms(x, g_ref[...]).astype(BF16)
    cos = cos_ref[...]
    sin = sin_ref[...]
    ri = lax.broadcasted_iota(jnp.int32, (chunk, chunk), 0)
    ci = lax.broadcasted_iota(jnp.int32, (chunk, chunk), 1)
    lag = (ri - ci).astype(F32)
    row = lax.broadcasted_iota(jnp.int32, (chunk, half), 0).astype(F32)
    k_off, v_off, g_off = heads * dk, 2 * heads * dk, 2 * heads * dk + heads * dv
    for h in range(heads):
        log_gamma = math.log(1.0 - 2.0 ** (-5.0 - h))
        q = _dot(hn, win_ref[:, h * dk:(h + 1) * dk])
        k = _dot(hn, win_ref[:, k_off + h * dk:k_off + (h + 1) * dk])
        v = _dot(hn, win_ref[:, v_off + h * dv:v_off + (h + 1) * dv]).astype(BF16)
        gate = _dot(hn, win_ref[:, g_off + h * dv:g_off + (h + 1) * dv])
        q1, q2 = q[:, :half], q[:, half:]
        k1, k2 = k[:, :half], k[:, half:]
        qr1, qr2 = q1 * cos - q2 * sin, q2 * cos + q1 * sin
        kr1, kr2 = (k1 * cos - k2 * sin) * dk ** -0.5, (k2 * cos + k1 * sin) * dk ** -0.5
        qb = jnp.concatenate([qr1, qr2], axis=1).astype(BF16)
        kb = jnp.concatenate([kr1, kr2], axis=1).astype(BF16)
        decay = jnp.where(ri >= ci, jnp.exp(lag * log_gamma), 0.0)
        p = (_dot_nt(qb, kb) * decay).astype(BF16)
        q_scale = jnp.exp((row + 1.0) * log_gamma)
        k_scale = jnp.exp((chunk - 1.0 - row) * log_gamma)
        qd = jnp.concatenate([qr1 * q_scale, qr2 * q_scale], axis=1).astype(BF16)
        kd = jnp.concatenate([kr1 * k_scale, kr2 * k_scale], axis=1)
        s = s_scr[h]
        o = _dot(p, v) + _dot(qd, s.astype(BF16))
        s_scr[h] = math.exp(chunk * log_gamma) * s + _dot(kd.T.astype(BF16), v)
        o = o * lax.rsqrt(jnp.mean(o * o, axis=-1, keepdims=True) + EPS) * _silu(gate)
        og_scr[:, h * dv:(h + 1) * dv] = o.astype(BF16)
    y_ref[...] = x + _dot(og_scr[...], wout_ref[...])

    @pl.when(t == pl.num_programs(1) - 1)
    def _():
        ro = lax.broadcasted_iota(jnp.int32, (dk, dk), 0)
        rin = lax.broadcasted_iota(jnp.int32, (dk, dk), 1)
        perm = (rin == (ro >> 1) + (ro & 1) * half).astype(BF16)
        for h in range(heads):
            sout_ref[0, h] = _exact_dot_left(perm, s_scr[h])


def _side_readout_specs(small, s0, layer, steps, step_of):
    _, n, heads, dk, dv = s0.shape
    assert n % steps == 0
    block = n // steps
    in_specs = [_resident(t.shape) for t in small]
    in_specs.append(pl.BlockSpec((1, block, heads, dk, dv), lambda *ids: (layer, step_of(*ids), 0, 0, 0)))
    o_spec = pl.BlockSpec((1, block, heads * dv), lambda *ids: (step_of(*ids), 0, 0))
    return block, in_specs, o_spec, jax.ShapeDtypeStruct((steps, block, heads * dv), F32)


def _ret_prompt(x, g, cos, sin, w_in, w_out, batch, heads, dk, dv, side=None):
    m, d = x.shape
    seq = m // batch
    chunk = min(RET_CHUNK, seq)
    assert seq % chunk == 0
    nt = seq // chunk
    args = [x, g, cos, sin, w_in, w_out]
    in_specs = [
        pl.BlockSpec((chunk, d), lambda b, t: (b * nt + t, 0)),
        _resident((1, d)),
        pl.BlockSpec((chunk, dk // 2), lambda b, t: (t, 0)),
        pl.BlockSpec((chunk, dk // 2), lambda b, t: (t, 0)),
        _resident(w_in.shape),
        _resident(w_out.shape),
    ]
    out_specs = [
        pl.BlockSpec((chunk, d), lambda b, t: (b * nt + t, 0)),
        pl.BlockSpec((1, heads, dk, dv), lambda b, t: (b, 0, 0, 0)),
    ]
    out_shape = [jax.ShapeDtypeStruct((m, d), F32), jax.ShapeDtypeStruct((batch, heads, dk, dv), F32)]
    scratch_shapes = [pltpu.VMEM((heads, dk, dv), F32), pltpu.VMEM((chunk, heads * dv), BF16)]
    side_block, side_layer = 0, 0
    if side is not None:
        small, s0, side_layer = side
        side_block, side_specs, o_spec, o_shape = _side_readout_specs(small, s0, side_layer, batch * nt,
                                                                      lambda b, t: b * nt + t)
        assert batch * nt >= 2
        args += list(small) + [s0, s0]
        in_specs += side_specs + [pl.BlockSpec(memory_space=pl.ANY)]
        out_specs += [o_spec, pl.BlockSpec(memory_space=pl.ANY)]
        out_shape += [o_shape, jax.ShapeDtypeStruct(s0.shape, F32)]
        scratch_shapes += [pltpu.VMEM((2, side_block) + s0.shape[2:], F32),
                           pltpu.SemaphoreType.DMA((s0.shape[0], 2))]
    outs = pl.pallas_call(
        functools.partial(_ret_prompt_kernel, heads=heads, dk=dk, dv=dv, chunk=chunk, side_block=side_block,
                          side_layer=side_layer),
        name="ret_prompt",
        grid=(batch, nt),
        in_specs=in_specs,
        out_specs=out_specs,
        out_shape=out_shape,
        scratch_shapes=scratch_shapes,
        compiler_params=_params("arbitrary", "arbitrary"),
    )(*args)
    return outs if side is None else (outs[0], outs[1], outs[2].reshape(-1, outs[2].shape[-1]), outs[3])


def _ret_sample_proj_kernel(x_ref, g_ref, cos_ref, sin_ref, win_ref, wsw_ref, q_ref, kt_ref, v_ref, gate_ref,
                            *, heads, dk, dv):
    qk_w = 2 * heads * dk
    hn = _rms(x_ref[...], g_ref[...]).astype(BF16)
    qk = _dot(hn, win_ref[:, :qk_w]) * cos_ref[...] + _dot(hn, wsw_ref[...]) * sin_ref[...]
    q_ref[...] = qk[:, :heads * dk]
    kt_ref[...] = (qk[:, heads * dk:] * dk ** -0.5).T
    v_ref[...] = _dot(hn, win_ref[:, qk_w:qk_w + heads * dv])
    gate_ref[...] = _dot(hn, win_ref[:, qk_w + heads * dv:])


def _ret_sample_state_kernel(q_ref, kt_ref, v_ref, s0_ref, stacked_in_ref, s_ref, o_ref, *, heads, dk, dv, block):
    del stacked_in_ref
    for i in range(block):
        s_news, o_row = _ret_update_rows(pl.program_id(0) * block + i, q_ref, kt_ref, v_ref,
                                         lambda h: s0_ref[0, i, h], heads=heads, dk=dk, dv=dv)
        for h in range(heads):
            s_ref[0, i, h] = s_news[h]
        o_ref[0, i:i + 1, :] = o_row


def _mixer_out_kernel(x_ref, o_ref, gate_ref, nw_ref, wout_ref, y_ref, *, heads, dv):
    parts = []
    for h in range(heads):
        o = o_ref[:, h * dv:(h + 1) * dv]
        o = o * lax.rsqrt(jnp.mean(o * o, axis=-1, keepdims=True) + EPS) * nw_ref[...]
        parts.append((o * _silu(gate_ref[:, h * dv:(h + 1) * dv])).astype(BF16))
    y_ref[...] = x_ref[...] + _dot(jnp.concatenate(parts, axis=1), wout_ref[...])


def _mixer_out(x, o, gate, norm_w, w_out, heads, dv):
    return pl.pallas_call(
        functools.partial(_mixer_out_kernel, heads=heads, dv=dv),
        name="mixer_out",
        out_shape=jax.ShapeDtypeStruct(x.shape, F32),
        compiler_params=_params(),
    )(x, o, gate, norm_w, w_out)


def _ret_sample_proj(x, g, cos, sin, w_in, w_sw, heads, dk, dv):
    n = x.shape[0]
    return pl.pallas_call(
        functools.partial(_ret_sample_proj_kernel, heads=heads, dk=dk, dv=dv),
        name="ret_sample_proj",
        out_shape=[
            jax.ShapeDtypeStruct((n, heads * dk), F32),
            jax.ShapeDtypeStruct((heads * dk, n), F32),
            jax.ShapeDtypeStruct((n, heads * dv), F32),
            jax.ShapeDtypeStruct((n, heads * dv), F32),
        ],
        compiler_params=_params(),
    )(x, g, cos, sin, w_in, w_sw)


def _ret_sample_final(q, kt, v, s0, layer, stacked):
    n = q.shape[0]
    _, _, heads, dk, dv = s0.shape
    block = math.gcd(2 * RET_STATE_BLOCK, n)
    state_spec = pl.BlockSpec((1, block, heads, dk, dv), lambda i: (layer, i, 0, 0, 0))
    s_new, o = pl.pallas_call(
        functools.partial(_ret_sample_state_kernel, heads=heads, dk=dk, dv=dv, block=block),
        name="ret_sample_state",
        grid=(n // block,),
        in_specs=[_resident(q.shape), _resident(kt.shape), _resident(v.shape), state_spec,
                  pl.BlockSpec(memory_space=pl.ANY)],
        out_specs=[state_spec, pl.BlockSpec((1, block, heads * dv), lambda i: (i, 0, 0))],
        out_shape=[jax.ShapeDtypeStruct(s0.shape, F32), jax.ShapeDtypeStruct((n // block, block, heads * dv), F32)],
        input_output_aliases={4: 0},
        compiler_params=_params("arbitrary"),
    )(q, kt, v, s0, stacked)
    return s_new, o.reshape(n, heads * dv)


def _unit_lower_inverse_minus_identity(lms, ri, ci, chunk):
    def level_mask(lev):
        return ((ri >> lev) == (ci >> lev)) & ((ri >> (lev - 1)) != (ci >> (lev - 1)))

    mask = level_mask(1)
    ns = [jnp.where(mask, -lm, 0.0) for lm in lms]
    for lev in range(2, chunk.bit_length()):
        mask = level_mask(lev)
        cs = [jnp.where(mask, lm, 0.0) for lm in lms]
        nbs = [n.astype(BF16) for n in ns]
        gs = [c + cn for c, cn in zip(cs, _dots([c.astype(BF16) for c in cs], nbs))]
        ns = [n - g - ng for n, g, ng in zip(ns, gs, _dots(nbs, [g.astype(BF16) for g in gs]))]
    return ns


def _gdn_prompt_kernel(x_ref, g_ref, cbuf_ref, win_ref, cw_ref, alog_ref, dtb_ref, nw_ref, wout_ref, *rest,
                       heads, dk, dv, rows, chunk, side_block):
    if side_block:
        side_in, (y_ref, sout_ref, cout_ref, side_o_ref, s_scr, xh_scr, uh_scr, og_scr) = rest[:6], rest[6:]
        _gdn_readout(pl.program_id(0) * pl.num_programs(1) + pl.program_id(1), side_block, *side_in, side_o_ref,
                     heads=heads, dk=dk, dv=dv)
    else:
        y_ref, sout_ref, cout_ref, s_scr, xh_scr, uh_scr, og_scr = rest
    t = pl.program_id(1)
    qkv_w = heads * (2 * dk + dv)
    hd = heads * dv
    cw = [cw_ref[j:j + 1, :] for j in range(4)]

    @pl.when(t == 0)
    def _():
        s_scr[...] = jnp.zeros_like(s_scr)
        xh_scr[...] = jnp.zeros_like(xh_scr)
        uh_scr[...] = jnp.zeros_like(uh_scr)
        xh_scr[0:1, :] = cbuf_ref[0, 2:3, :]
        uh_scr[0:1, :] = cw[1] * cbuf_ref[0, 1:2, :] + cw[0] * cbuf_ref[0, 0:1, :]
        uh_scr[1:2, :] = cw[1] * cbuf_ref[0, 2:3, :] + cw[0] * cbuf_ref[0, 1:2, :]

    x = x_ref[...]
    hn = _rms(x, g_ref[...]).astype(BF16)
    qkv = _dot(hn, win_ref[:, :qkv_w])
    sub = lax.broadcasted_iota(jnp.int32, (SUBLANES, qkv_w), 0)

    def shifted(cur, by, halo_tile):
        rolled = pltpu.roll(cur, by, 0)
        first = jnp.where(sub < by, halo_tile, rolled[:SUBLANES])
        return jnp.concatenate([first, rolled[SUBLANES:]], axis=0)

    x_prev = shifted(qkv, 1, xh_scr[...])
    u = cw[1] * qkv + cw[0] * x_prev
    conv = cw[3] * qkv + cw[2] * x_prev + shifted(u, 2, uh_scr[...])
    xh_scr[0:1, :] = qkv[rows - 1:rows, :]
    uh_scr[0:2, :] = u[rows - 2:rows, :]

    @pl.when(t == pl.num_programs(1) - 1)
    def _():
        cout_ref[0] = qkv[rows - 3:rows, :]

    act = _silu(conv)
    z = _dot(hn, win_ref[:, qkv_w:qkv_w + hd])
    ba = _dot(hn, win_ref[:, qkv_w + hd:])
    a = -jnp.exp(alog_ref[...]) * _softplus(ba + dtb_ref[...])
    ri = lax.broadcasted_iota(jnp.int32, (chunk, chunk), 0)
    ci = lax.broadcasted_iota(jnp.int32, (chunk, chunk), 1)
    tril = (ri >= ci).astype(BF16)
    subs = range(rows // chunk)
    bcum = jnp.concatenate([_exact_dot_left(tril, a[c * chunk:(c + 1) * chunk]) for c in subs], axis=0)
    beta = _expand_heads(jax.nn.sigmoid(ba), 0, heads, dv)
    bcum = _expand_heads(bcum, heads, heads, dv)
    probs = [(c, h) for c in subs for h in range(heads)]

    def tile(arr, c, col, width):
        return arr[c * chunk:(c + 1) * chunk, col:col + width]

    qs = [tile(act, c, h * dk, dk) for c, h in probs]
    ks = [tile(act, c, heads * dk + h * dk, dk) for c, h in probs]
    vs = [tile(act, c, 2 * heads * dk + h * dv, dv) for c, h in probs]
    qs = [q * lax.rsqrt(jnp.sum(q * q, axis=-1, keepdims=True) + EPS) * dk ** -0.5 for q in qs]
    ks = [k * lax.rsqrt(jnp.sum(k * k, axis=-1, keepdims=True) + EPS) for k in ks]
    bts = [tile(beta, c, h * dv, dv) for c, h in probs]
    bs = [tile(bcum, c, h * dv, dv) for c, h in probs]
    es = [jnp.exp(jnp.minimum(b - b.T, 0.0)) for b in bs]
    kbs = [k.astype(BF16) for k in ks]
    qk_kks = _dots([jnp.concatenate([q.astype(BF16), kb], axis=0) for q, kb in zip(qs, kbs)], kbs, nt=True)
    ps = [jnp.where(ri >= ci, qk_kk[:chunk] * e, 0.0).astype(BF16) for qk_kk, e in zip(qk_kks, es)]
    lms = [jnp.where(ri > ci, qk_kk[chunk:] * e, 0.0) * bt for qk_kk, e, bt in zip(qk_kks, es, bts)]
    ns = [n.astype(BF16) for n in _unit_lower_inverse_minus_identity(lms, ri, ci, chunk)]
    kq_es = [jnp.concatenate([(k * jnp.exp(b)).astype(BF16), (q * jnp.exp(b)).astype(BF16)], axis=0)
             for k, q, b in zip(ks, qs, bs)]
    b_lasts = [b[chunk - 1:chunk, :] for b in bs]
    kdts = [(k * jnp.exp(b_last - b)).T.astype(BF16) for k, b, b_last in zip(ks, bs, b_lasts)]
    ss = [s_scr[h] for h in range(heads)]
    for c in subs:
        sel = slice(c * heads, (c + 1) * heads)
        kq_ss = _dots(kq_es[sel], [s.astype(BF16) for s in ss])
        rs = [bt * (v - kq_s[:chunk]) for bt, v, kq_s in zip(bts[sel], vs[sel], kq_ss)]
        wbs = [(r + nr).astype(BF16) for r, nr in zip(rs, _dots(ns[sel], [r.astype(BF16) for r in rs]))]
        os_ = [kq_s[chunk:] + pw for kq_s, pw in zip(kq_ss, _dots(ps[sel], wbs))]
        ss = [jnp.exp(b_last) * s + kw for b_last, s, kw in zip(b_lasts[sel], ss, _dots(kdts[sel], wbs))]
        for h in range(heads):
            o = os_[h]
            o = o * lax.rsqrt(jnp.mean(o * o, axis=-1, keepdims=True) + EPS) * nw_ref[...]
            og_scr[c * chunk:(c + 1) * chunk, h * dv:(h + 1) * dv] = (o * _silu(tile(z, c, h * dv, dv))).astype(BF16)
    for h in range(heads):
        s_scr[h] = ss[h]
    y_ref[...] = x + _dot(og_scr[...], wout_ref[...])

    @pl.when(t == pl.num_programs(1) - 1)
    def _():
        sout_ref[0] = s_scr[...]


def _gdn_prompt(x, g, cbuf, w_in, conv_w, a_log, dt_bias, norm_w, w_out, batch, heads, dk, dv, side=None):
    m, d = x.shape
    seq = m // batch
    chunk = GDN_CHUNK
    taps = conv_w.shape[0]
    qkv_w = heads * (2 * dk + dv)
    rows = math.gcd(GDN_ROWS, seq)
    assert rows % chunk == 0 and dk == chunk and dv == chunk and chunk == LANES and 2 * heads <= LANES and taps == 4
    nt = seq // rows
    args = [x, g, cbuf, w_in, conv_w, a_log, dt_bias, norm_w, w_out]
    in_specs = [
        pl.BlockSpec((rows, d), lambda b, t: (b * nt + t, 0)),
        _resident((1, d)),
        pl.BlockSpec((1, taps - 1, qkv_w), lambda b, t: (b, 0, 0)),
        _resident(w_in.shape),
        _resident(conv_w.shape),
        _resident(a_log.shape),
        _resident(dt_bias.shape),
        _resident(norm_w.shape),
        _resident(w_out.shape),
    ]
    out_specs = [
        pl.BlockSpec((rows, d), lambda b, t: (b * nt + t, 0)),
        pl.BlockSpec((1, heads, dk, dv), lambda b, t: (b, 0, 0, 0)),
        pl.BlockSpec((1, taps - 1, qkv_w), lambda b, t: (b, 0, 0)),
    ]
    out_shape = [
        jax.ShapeDtypeStruct((m, d), F32),
        jax.ShapeDtypeStruct((batch, heads, dk, dv), F32),
        jax.ShapeDtypeStruct((batch, taps - 1, qkv_w), F32),
    ]
    side_block = 0
    if side is not None:
        small, s0, layer = side
        side_block, side_specs, o_spec, o_shape = _side_readout_specs(small, s0, layer, batch * nt,
                                                                      lambda b, t: b * nt + t)
        args += list(small) + [s0]
        in_specs += side_specs
        out_specs.append(o_spec)
        out_shape.append(o_shape)
    outs = pl.pallas_call(
        functools.partial(_gdn_prompt_kernel, heads=heads, dk=dk, dv=dv, rows=rows, chunk=chunk,
                          side_block=side_block),
        name="gdn_prompt",
        grid=(batch, nt),
        in_specs=in_specs,
        out_specs=out_specs,
        out_shape=out_shape,
        scratch_shapes=[
            pltpu.VMEM((heads, dk, dv), F32),
            pltpu.VMEM((SUBLANES, qkv_w), F32),
            pltpu.VMEM((SUBLANES, qkv_w), F32),
            pltpu.VMEM((rows, heads * dv), BF16),
        ],
        compiler_params=_params("arbitrary", "arbitrary"),
    )(*args)
    return outs if side is None else (outs[0], outs[1], outs[2], outs[3].reshape(-1, outs[3].shape[-1]))


def _gdn_sample_proj_kernel(x_ref, g_ref, cbuf_ref, win_ref, cw_ref, alog_ref, dtb_ref,
                            q_ref, kt_ref, k_ref, v_ref, z_ref, beta_ref, ea_ref, cout_ref, *, heads, dk, dv, taps):
    qkv_w = heads * (2 * dk + dv)
    hd = heads * dv
    halo = taps - 1
    hn = _rms(x_ref[...], g_ref[...]).astype(BF16)
    qkv = _dot(hn, win_ref[:, :qkv_w])
    conv = qkv * cw_ref[halo:taps, :]
    for j in range(halo):
        conv = conv + cbuf_ref[j] * cw_ref[j:j + 1, :]
        if j > 0:
            cout_ref[j - 1] = cbuf_ref[j]
    cout_ref[halo - 1] = qkv
    act = _silu(conv)
    for h in range(heads):
        q = act[:, h * dk:(h + 1) * dk]
        k = act[:, heads * dk + h * dk:heads * dk + (h + 1) * dk]
        q_ref[:, h * dk:(h + 1) * dk] = q * lax.rsqrt(jnp.sum(q * q, axis=-1, keepdims=True) + EPS) * dk ** -0.5
        k_ref[:, h * dk:(h + 1) * dk] = k * lax.rsqrt(jnp.sum(k * k, axis=-1, keepdims=True) + EPS)
    kt_ref[...] = k_ref[...].T
    v_ref[...] = act[:, 2 * heads * dk:]
    z_ref[...] = _dot(hn, win_ref[:, qkv_w:qkv_w + hd])
    ba = _dot(hn, win_ref[:, qkv_w + hd:])
    beta_ref[...] = _expand_heads(jax.nn.sigmoid(ba), 0, heads, dv)
    a = -jnp.exp(alog_ref[...]) * _softplus(ba + dtb_ref[...])
    ea_ref[...] = jnp.exp(_expand_heads(a, heads, heads, dv))


def _gdn_sample_state_kernel(q_ref, *refs, heads, dk, dv, block):
    n_layers = (len(refs) - 2) // 6
    ins, (s_ref, o_ref) = refs[:6 * n_layers], refs[6 * n_layers:]
    n = q_ref.shape[0]
    lane = lax.broadcasted_iota(jnp.int32, (dk, n), 1)
    hs = range(heads)
    for i in range(block):
        b = pl.program_id(0) * block + i
        for layer in range(n_layers):
            kt_ref, k_ref, v_ref, beta_ref, ea_ref, s0_ref = ins[6 * layer:6 * layer + 6]
            k_row, v_row = k_ref[pl.ds(b, 1), :], v_ref[pl.ds(b, 1), :]
            beta_row, ea_row = beta_ref[pl.ds(b, 1), :], ea_ref[pl.ds(b, 1), :]
            s0s = [s0_ref[0, i, h] for h in hs]
            eas = [ea_row[:, h * dv:(h + 1) * dv] for h in hs]
            k8s = [jnp.broadcast_to(k_row[:, h * dk:(h + 1) * dk] * ea, (SUBLANES, dk)).astype(BF16)
                   for h, ea in zip(hs, eas)]
            kss = [_dot(k8, s0.astype(BF16))[0:1] for k8, s0 in zip(k8s, s0s)]
            ws = [beta_row[:, h * dv:(h + 1) * dv] * (v_row[:, h * dv:(h + 1) * dv] - ks) for h, ks in zip(hs, kss)]
            kt_bs = [jnp.where(lane == b, kt_ref[h * dk:(h + 1) * dk, :], 0.0).astype(BF16) for h in hs]
            s_news = [ea * s0 + _dot(kt_b, jnp.broadcast_to(w, (n, dv)).astype(BF16))
                      for ea, s0, kt_b, w in zip(eas, s0s, kt_bs, ws)]
            for h in hs:
                s_ref[layer, i, h] = s_news[h]
        q_row = q_ref[pl.ds(b, 1), :]
        q8s = [jnp.broadcast_to(q_row[:, h * dk:(h + 1) * dk], (SUBLANES, dk)).astype(BF16) for h in hs]
        o_ref[i:i + 1, :] = jnp.concatenate(
            [_dot(q8, s_new.astype(BF16))[0:1] for q8, s_new in zip(q8s, s_news)], axis=1)


def _gdn_sample_proj(x, g, cbuf, w_in, conv_w, a_log, dt_bias, heads, dk, dv):
    n = x.shape[0]
    taps = conv_w.shape[0]
    qkv_w = heads * (2 * dk + dv)
    hd = heads * dv
    cbuf_t = jnp.swapaxes(cbuf, 0, 1)
    q, kt, k, v, z, beta, ea, cout_t = pl.pallas_call(
        functools.partial(_gdn_sample_proj_kernel, heads=heads, dk=dk, dv=dv, taps=taps),
        name="gdn_sample_proj",
        out_shape=[
            jax.ShapeDtypeStruct((n, heads * dk), F32),
            jax.ShapeDtypeStruct((heads * dk, n), F32),
            jax.ShapeDtypeStruct((n, heads * dk), F32),
            jax.ShapeDtypeStruct((n, hd), F32),
            jax.ShapeDtypeStruct((n, hd), F32),
            jax.ShapeDtypeStruct((n, hd), F32),
            jax.ShapeDtypeStruct((n, hd), F32),
            jax.ShapeDtypeStruct((taps - 1, n, qkv_w), F32),
        ],
        compiler_params=_params(),
    )(x, g, cbuf_t, w_in, conv_w, a_log, dt_bias)
    return q, kt, k, v, z, beta, ea, jnp.swapaxes(cout_t, 0, 1)


def _gdn_sample_final(q, history, s0):
    n = q.shape[0]
    layers, _, heads, dk, dv = s0.shape
    assert len(history) == layers
    hd = heads * dv
    block = math.gcd(GDN_STATE_BLOCK, n)
    args, in_specs = [q], [_resident(q.shape)]
    for idx, small in enumerate(history):
        args += list(small) + [s0]
        in_specs += [_resident(t.shape) for t in small]
        in_specs.append(pl.BlockSpec((1, block, heads, dk, dv), lambda i, idx=idx: (idx, i, 0, 0, 0)))
    return pl.pallas_call(
        functools.partial(_gdn_sample_state_kernel, heads=heads, dk=dk, dv=dv, block=block),
        name="gdn_sample_state",
        grid=(n // block,),
        in_specs=in_specs,
        out_specs=[pl.BlockSpec((layers, block, heads, dk, dv), lambda i: (0, i, 0, 0, 0)),
                   pl.BlockSpec((block, hd), lambda i: (i, 0))],
        out_shape=[jax.ShapeDtypeStruct(s0.shape, F32), jax.ShapeDtypeStruct((n, hd), F32)],
        compiler_params=_params("arbitrary"),
    )(*args)


def _rope_tables(pos, dk):
    theta = 1.0 / (ROPE_BASE ** jnp.linspace(0.0, 1.0, dk // 2, dtype=F32))
    ang = pos[:, None] * theta[None, :]
    return jnp.cos(ang), jnp.sin(ang)


def _ret_weight_prep_kernel(w_ref, w_out_ref, deint_ref, swap_ref, *, heads, dk):
    qk_w = 2 * heads * dk
    half = dk // 2
    w = w_ref[...].astype(BF16)
    w_out_ref[...] = w
    deint_ref[:, qk_w:] = w[:, qk_w:]
    src = lax.broadcasted_iota(jnp.int32, (dk, dk), 0)
    dst = lax.broadcasted_iota(jnp.int32, (dk, dk), 1)
    to_halves = (src == jnp.where(dst < half, 2 * dst, 2 * (dst - half) + 1)).astype(BF16)
    swap_signed = jnp.where(src == (dst ^ 1), jnp.where((dst & 1) == 0, -1.0, 1.0), 0.0).astype(BF16)
    for h in range(2 * heads):
        block = w[:, h * dk:(h + 1) * dk]
        deint_ref[:, h * dk:(h + 1) * dk] = _dot(block, to_halves).astype(BF16)
        swap_ref[:, h * dk:(h + 1) * dk] = _dot(block, swap_signed).astype(BF16)


def _prep_ret_weights(w_in_all, layer, heads, dk):
    _, d, cols = w_in_all.shape
    qk_w = 2 * heads * dk
    rows = math.gcd(256, d)
    return pl.pallas_call(
        functools.partial(_ret_weight_prep_kernel, heads=heads, dk=dk),
        name="ret_weight_prep",
        grid=(d // rows,),
        in_specs=[pl.BlockSpec((None, rows, cols), lambda i: (layer, i, 0))],
        out_specs=[pl.BlockSpec((rows, cols), lambda i: (i, 0)), pl.BlockSpec((rows, cols), lambda i: (i, 0)),
                   pl.BlockSpec((rows, qk_w), lambda i: (i, 0))],
        out_shape=[jax.ShapeDtypeStruct((d, cols), BF16), jax.ShapeDtypeStruct((d, cols), BF16),
                   jax.ShapeDtypeStruct((d, qk_w), BF16)],
        compiler_params=_params("arbitrary"),
    )(w_in_all)


def _prep_gdn_weights(w_in, a_log, dt_bias, heads, dk, dv):
    pad = LANES - 2 * heads
    w = jnp.pad(w_in, ((0, 0), (0, pad))).astype(BF16)
    place = lambda p: jnp.pad(p, (heads, pad))[None, :]
    return w, place(a_log), place(dt_bias)


def _trunk(x_p, x_s, batch_p, s_ret, s_gdn, s_conv, norm_g, ffn_gu, ffn_down, ret_w, ret_w_out, gdn_w, gdn_conv_w,
           gdn_norm_w, gdn_w_out, final_g):
    depth = norm_g.shape[0]
    n_mixers = 2
    seq = x_p.shape[0] // batch_p
    n_ret, _, ret_heads, ret_dk, ret_dv = s_ret.shape
    n_gdn, _, gdn_heads, gdn_dk, gdn_dv = s_gdn.shape
    cos_p, sin_p = _rope_tables(jnp.arange(seq, dtype=F32), ret_dk)
    cos_s, sin_s = (jnp.tile(jnp.repeat(t, 2, axis=1), (1, 2 * ret_heads))
                    for t in _rope_tables(jnp.full((1,), SAMPLE_PAST_LEN, F32), ret_dk))
    ones_dv = jnp.ones((1, ret_dv), F32)
    fg = final_g[None, :]
    norm_g4 = norm_g[:, :, None, :]
    zero_conv = jnp.zeros((batch_p,) + s_conv.shape[2:], s_conv.dtype)
    ffn_w = (ffn_gu[0, 0].astype(BF16), ffn_down[0, 0].astype(BF16))
    ret_p, gdn_p, conv_p, conv_s = [], [], [], []
    ret_s = gdn_s = None
    gdn_hist = []
    for i in range(depth):
        j = i // n_mixers
        for which, which_norm in ((0, 0), (1, 2)):
            if which == 1:
                g = norm_g[i, 1][None, :]
                if i % n_mixers == 0:
                    w_orig, w_deint, w_swap = ret_w[j]
                    q, kt, v, gate = _ret_sample_proj(x_s, g, cos_s, sin_s, w_orig, w_swap, ret_heads, ret_dk, ret_dv)
                    prompt_args = (x_p, g, cos_p, sin_p, w_deint, ret_w_out[j], batch_p, ret_heads, ret_dk, ret_dv)
                    if j == 0:
                        x_p, s, o, ret_s = _ret_prompt(*prompt_args, side=((q, kt, v), s_ret, j))
                    else:
                        assert j == n_ret - 1, "sample retention states: first layer rides in the prompt call, last runs alone"
                        x_p, s = _ret_prompt(*prompt_args)
                        ret_s, o = _ret_sample_final(q, kt, v, s_ret, j, ret_s)
                    ret_p.append(s)
                    x_s = _mixer_out(x_s, o, gate, ones_dv, ret_w_out[j], ret_heads, ret_dv)
                else:
                    w, a_log, dt_bias = gdn_w[j]
                    nw = gdn_norm_w[j][None, :]
                    q, kt, k, v, z, beta, ea, buf = _gdn_sample_proj(x_s, g, s_conv[j], w, gdn_conv_w[j], a_log,
                                                                     dt_bias, gdn_heads, gdn_dk, gdn_dv)
                    conv_s.append(buf)
                    gdn_hist.append((kt, k, v, beta, ea))
                    prompt_args = (x_p, g, zero_conv, w, gdn_conv_w[j], a_log, dt_bias, nw, gdn_w_out[j], batch_p,
                                   gdn_heads, gdn_dk, gdn_dv)
                    if j == n_gdn - 1:
                        x_p, s, buf = _gdn_prompt(*prompt_args)
                        gdn_s, o = _gdn_sample_final(q, gdn_hist, s_gdn)
                    else:
                        x_p, s, buf, o = _gdn_prompt(*prompt_args, side=((q, k, v, beta, ea), s_gdn, j))
                    gdn_p.append(s)
                    conv_p.append(buf)
                    x_s = _mixer_out(x_s, o, z, nw, gdn_w_out[j], gdn_heads, gdn_dv)
            nxt = (i, 1) if which == 0 else (i + 1, 0)
            cast_next = (ffn_gu, ffn_down) + nxt if nxt[0] < depth else None
            x_p, x_s, made = _ffn(x_p, x_s, norm_g4, *ffn_w, fg, i, which_norm, which == 1 and i == depth - 1,
                                  cast_next)
            ffn_w = made
    return (x_p, x_s, jnp.stack(ret_p), jnp.stack(gdn_p), jnp.stack(conv_p), ret_s, gdn_s, jnp.stack(conv_s))


def kernel(x_prompt, x_sample, state_ret, state_gdn, state_gdn_conv, norm_g, ffn_gu, ffn_down, ret_w_in,
           ret_w_out, gdn_w_in, gdn_conv_w, gdn_a_log, gdn_dt_bias, gdn_norm_w, gdn_w_out, final_g):
    bp, seq, d = x_prompt.shape
    bs, dec_seq, _ = x_sample.shape
    assert dec_seq == 1
    n_ret, _, ret_heads, ret_dk, _ = state_ret.shape
    n_gdn, _, gdn_heads, gdn_dk, gdn_dv = state_gdn.shape
    ret_w = [_prep_ret_weights(ret_w_in, j, ret_heads, ret_dk) for j in range(n_ret)]
    gdn_w = [_prep_gdn_weights(gdn_w_in[j], gdn_a_log[j], gdn_dt_bias[j], gdn_heads, gdn_dk, gdn_dv)
             for j in range(n_gdn)]
    y_p, y_s, ret_p, gdn_p, conv_p, ret_s, gdn_s, conv_s = _trunk(
        x_prompt.reshape(bp * seq, d), x_sample.reshape(bs, d), bp, state_ret, state_gdn, state_gdn_conv, norm_g,
        ffn_gu, ffn_down, ret_w, ret_w_out.astype(BF16), gdn_w, gdn_conv_w, gdn_norm_w, gdn_w_out.astype(BF16),
        final_g)
    return (y_p.reshape(bp, seq, d), y_s.reshape(bs, dec_seq, d), ret_p, gdn_p, conv_p, ret_s, gdn_s, conv_s)
```

```python
import functools
import math

import jax
import jax.numpy as jnp
from jax import lax
from jax.experimental import pallas as pl
from jax.experimental.pallas import tpu as pltpu

F32 = jnp.float32
BF16 = jnp.bfloat16
EPS = 1e-6
ROPE_BASE = 10000.0
SAMPLE_PAST_LEN = 16384.0

LANES = 128
SUBLANES = 8
VMEM_LIMIT_BYTES = 56 * 1024 * 1024

FFN_ROWS = 1024
RET_CHUNK = 256
GDN_CHUNK = 128
GDN_ROWS = 256
RET_STATE_BLOCK = 2
GDN_STATE_BLOCK = 8


def _dot(a, b):
    return jnp.dot(a, b, preferred_element_type=F32)


def _dot_nt(a, b):
    return lax.dot_general(a, b, (((1,), (1,)), ((), ())), preferred_element_type=F32)


def _rms(x, g):
    return x * lax.rsqrt(jnp.mean(x * x, axis=-1, keepdims=True) + EPS) * g


def _silu(x):
    h = 0.5 * x
    return h * jnp.tanh(h) + h


def _softplus(x):
    return jnp.maximum(x, 0.0) + jnp.log1p(jnp.exp(-jnp.abs(x)))


def _bf16_terms(x):
    t0 = x.astype(BF16)
    r = x - t0.astype(F32)
    t1 = r.astype(BF16)
    t2 = (r - t1.astype(F32)).astype(BF16)
    return t0, t1, t2


def _exact_dot_right(x, sel):
    return _dot(jnp.concatenate(_bf16_terms(x), axis=1), jnp.concatenate([sel] * 3, axis=0))


def _exact_dot_left(sel, x):
    return _dot(jnp.concatenate([sel] * 3, axis=1), jnp.concatenate(_bf16_terms(x), axis=0))


def _expand_heads(x, first, heads, width):
    c = lax.broadcasted_iota(jnp.int32, (x.shape[1], heads * width), 0)
    lane = lax.broadcasted_iota(jnp.int32, (x.shape[1], heads * width), 1)
    e = ((lane >= (c - first) * width) & (lane < (c - first + 1) * width)).astype(BF16)
    return _exact_dot_right(x, e)


def _dots(lhs, rhs, nt=False):
    dot = _dot_nt if nt else _dot
    return [dot(a, b) for a, b in zip(lhs, rhs)]


def _resident(shape):
    nd = len(shape)
    return pl.BlockSpec(shape, lambda *_: (0,) * nd, pipeline_mode=pl.Buffered(1))


def _params(*sem):
    return pltpu.CompilerParams(dimension_semantics=sem, vmem_limit_bytes=VMEM_LIMIT_BYTES)


def _ffn_kernel(x_ref, small_ref, g_ref, wg_ref, wu_ref, wd_ref, fg_ref, *rest, final, cast_next):
    def ffn(x):
        h = _rms(x, g_ref[...]).astype(BF16)
        a = _dot(h, wg_ref[...])
        b = _dot(h, wu_ref[...])
        act = (_silu(a) * b).astype(BF16)
        y = x + 0.5 * _dot(act, wd_ref[...])
        return _rms(y, fg_ref[...]) if final else y

    o_ref, small_out_ref = rest[-4:-2] if cast_next else rest
    o_ref[...] = ffn(x_ref[...])

    @pl.when(pl.program_id(0) == pl.num_programs(0) - 1)
    def _():
        small_out_ref[...] = ffn(small_ref[...])

    if cast_next:
        next_gu_ref, next_down_ref = rest[:2]
        gu_out_ref, down_out_ref = rest[-2:]
        gu_out_ref[...] = next_gu_ref[...].astype(BF16)
        down_out_ref[...] = next_down_ref[...].astype(BF16)


def _slab_count(rows, steps):
    return max(n for n in range(1, steps + 1) if steps % n == 0 and rows % (16 * n) == 0)


def _ffn(x, small, norm_g, w_gu, w_down, final_g, layer, which_norm, final, cast_next=None):
    m, d = x.shape
    f = w_down.shape[0]
    tm = min(FFN_ROWS, m)
    steps = m // tm
    assert m % tm == 0 and f % LANES == 0
    in_specs = [
        pl.BlockSpec((tm, d), lambda i: (i, 0)),
        _resident(small.shape),
        pl.BlockSpec((None, None, 1, d), lambda i: (layer, which_norm, 0, 0), pipeline_mode=pl.Buffered(1)),
        pl.BlockSpec((d, f), lambda i: (0, 0), pipeline_mode=pl.Buffered(1)),
        pl.BlockSpec((d, f), lambda i: (0, 1), pipeline_mode=pl.Buffered(1)),
        _resident((f, d)),
        _resident((1, d)),
    ]
    args = [x, small, norm_g, w_gu, w_gu, w_down, final_g]
    out_specs = [pl.BlockSpec((tm, d), lambda i: (i, 0)), pl.BlockSpec(small.shape, lambda i: (0, 0))]
    out_shape = [jax.ShapeDtypeStruct((m, d), F32), jax.ShapeDtypeStruct(small.shape, F32)]
    if cast_next is not None:
        next_gu, next_down, nl, nw = cast_next
        for w, rows, cols in ((next_gu, d, 2 * f), (next_down, f, d)):
            n = _slab_count(rows, steps)
            in_specs.append(pl.BlockSpec((None, None, rows // n, cols),
                                         lambda i, n=n: (nl, nw, i * n // steps, 0)))
            out_specs.append(pl.BlockSpec((rows // n, cols), lambda i, n=n: (i * n // steps, 0)))
            out_shape.append(jax.ShapeDtypeStruct((rows, cols), BF16))
            args.append(w)
    outs = pl.pallas_call(
        functools.partial(_ffn_kernel, final=final, cast_next=cast_next is not None),
        name="ffn",
        grid=(steps,),
        in_specs=in_specs,
        out_specs=out_specs,
        out_shape=out_shape,
        compiler_params=_params("arbitrary"),
    )(*args)
    return outs[0], outs[1], tuple(outs[2:])


def _ret_readout(step, block, q_ref, k_ref, v_ref, s0_ref, o_ref, *, heads, dk, dv):
    hs = range(heads)
    for i in range(block):
        b = step * block + i
        q_row, k_row, v_row = q_ref[pl.ds(b, 1), :], k_ref[pl.ds(b, 1), :], v_ref[pl.ds(b, 1), :]
        q_hs = [q_row[:, h * dk:(h + 1) * dk] for h in hs]
        q_s = [_dot(jnp.broadcast_to(q_h, (SUBLANES, dk)).astype(BF16), s0_ref[0, i, h].astype(BF16))[0:1]
               for h, q_h in zip(hs, q_hs)]
        q_k = [jnp.sum(q_h * k_row[:, h * dk:(h + 1) * dk], axis=-1, keepdims=True) for h, q_h in zip(hs, q_hs)]
        o_ref[0, i:i + 1, :] = jnp.concatenate(
            [(1.0 - 2.0 ** (-5.0 - h)) * qs + qk * v_row[:, h * dv:(h + 1) * dv] for h, qs, qk in zip(hs, q_s, q_k)],
            axis=1)


def _gdn_readout(step, block, q_ref, k_ref, v_ref, beta_ref, ea_ref, s0_ref, o_ref, *, heads, dk, dv):
    hs = range(heads)
    row = lax.broadcasted_iota(jnp.int32, (SUBLANES, dk), 0)
    for i in range(block):
        b = step * block + i
        q_row, k_row, v_row = q_ref[pl.ds(b, 1), :], k_ref[pl.ds(b, 1), :], v_ref[pl.ds(b, 1), :]
        beta_row, ea_row = beta_ref[pl.ds(b, 1), :], ea_ref[pl.ds(b, 1), :]
        q_hs = [q_row[:, h * dk:(h + 1) * dk] for h in hs]
        k_hs = [k_row[:, h * dk:(h + 1) * dk] for h in hs]
        eas = [ea_row[:, h * dv:(h + 1) * dv] for h in hs]
        lhs = [jnp.where(row == 0, q_h, jnp.where(row == 1, k_h * ea, 0.0)).astype(BF16)
               for q_h, k_h, ea in zip(q_hs, k_hs, eas)]
        prods = [_dot(l, s0_ref[0, i, h].astype(BF16)) for h, l in zip(hs, lhs)]
        ws = [beta_row[:, h * dv:(h + 1) * dv] * (v_row[:, h * dv:(h + 1) * dv] - p[1:2]) for h, p in zip(hs, prods)]
        q_k = [jnp.sum(q_h * k_h, axis=-1, keepdims=True) for q_h, k_h in zip(q_hs, k_hs)]
        o_ref[0, i:i + 1, :] = jnp.concatenate(
            [ea * p[0:1] + qk * w for ea, p, qk, w in zip(eas, prods, q_k, ws)], axis=1)


def _ret_prompt_kernel(x_ref, g_ref, cos_ref, sin_ref, win_ref, wout_ref, *rest, heads, dk, dv, chunk, side_block):
    if side_block:
        side_in, (y_ref, sout_ref, side_o_ref, s_scr, og_scr) = rest[:4], rest[4:]
        _ret_readout(pl.program_id(0) * pl.num_programs(1) + pl.program_id(1), side_block, *side_in, side_o_ref,
                     heads=heads, dk=dk, dv=dv)
    else:
        y_ref, sout_ref, s_scr, og_scr = rest
    t = pl.program_id(1)
    half = dk // 2

    @pl.when(t == 0)
    def _():
        s_scr[...] = jnp.zeros_like(s_scr)

    x = x_ref[...]
    hn = _rms(x, g_ref[...]).astype(BF16)
    cos = cos_ref[...]
    sin = sin_ref[...]
    ri = lax.broadcasted_iota(jnp.int32, (chunk, chunk), 0)
    ci = lax.broadcasted_iota(jnp.int32, (chunk, chunk), 1)
    lag = (ri - ci).astype(F32)
    row = lax.broadcasted_iota(jnp.int32, (chunk, half), 0).astype(F32)
    k_off, v_off, g_off = heads * dk, 2 * heads * dk, 2 * heads * dk + heads * dv
    for h in range(heads):
        log_gamma = math.log(1.0 - 2.0 ** (-5.0 - h))
        q = _dot(hn, win_ref[:, h * dk:(h + 1) * dk])
        k = _dot(hn, win_ref[:, k_off + h * dk:k_off + (h + 1) * dk])
        v = _dot(hn, win_ref[:, v_off + h * dv:v_off + (h + 1) * dv]).astype(BF16)
        gate = _dot(hn, win_ref[:, g_off + h * dv:g_off + (h + 1) * dv])
        q1, q2 = q[:, :half], q[:, half:]
        k1, k2 = k[:, :half], k[:, half:]
        qr1, qr2 = q1 * cos - q2 * sin, q2 * cos + q1 * sin
        kr1, kr2 = (k1 * cos - k2 * sin) * dk ** -0.5, (k2 * cos + k1 * sin) * dk ** -0.5
        qb = jnp.concatenate([qr1, qr2], axis=1).astype(BF16)
        kb = jnp.concatenate([kr1, kr2], axis=1).astype(BF16)
        decay = jnp.where(ri >= ci, jnp.exp(lag * log_gamma), 0.0)
        p = (_dot_nt(qb, kb) * decay).astype(BF16)
        q_scale = jnp.exp((row + 1.0) * log_gamma)
        k_scale = jnp.exp((chunk - 1.0 - row) * log_gamma)
        qd = jnp.concatenate([qr1 * q_scale, qr2 * q_scale], axis=1).astype(BF16)
        kd = jnp.concatenate([kr1 * k_scale, kr2 * k_scale], axis=1)
        s = s_scr[h]
        o = _dot(p, v) + _dot(qd, s.astype(BF16))
        s_scr[h] = math.exp(chunk * log_gamma) * s + _dot(kd.T.astype(BF16), v)
        o = o * lax.rsqrt(jnp.mean(o * o, axis=-1, keepdims=True) + EPS) * _silu(gate)
        og_scr[:, h * dv:(h + 1) * dv] = o.astype(BF16)
    y_ref[...] = x + _dot(og_scr[...], wout_ref[...])

    @pl.when(t == pl.num_programs(1) - 1)
    def _():
        ro = lax.broadcasted_iota(jnp.int32, (dk, dk), 0)
        rin = lax.broadcasted_iota(jnp.int32, (dk, dk), 1)
        perm = (rin == (ro >> 1) + (ro & 1) * half).astype(BF16)
        for h in range(heads):
            sout_ref[0, h] = _exact_dot_left(perm, s_scr[h])


def _side_readout_specs(small, s0, layer, steps, step_of):
    _, n, heads, dk, dv = s0.shape
    assert n % steps == 0
    block = n // steps
    in_specs = [_resident(t.shape) for t in small]
    in_specs.append(pl.BlockSpec((1, block, heads, dk, dv), lambda *ids: (layer, step_of(*ids), 0, 0, 0)))
    o_spec = pl.BlockSpec((1, block, heads * dv), lambda *ids: (step_of(*ids), 0, 0))
    return block, in_specs, o_spec, jax.ShapeDtypeStruct((steps, block, heads * dv), F32)


def _ret_prompt(x, g, cos, sin, w_in, w_out, batch, heads, dk, dv, side=None):
    m, d = x.shape
    seq = m // batch
    chunk = min(RET_CHUNK, seq)
    assert seq % chunk == 0
    nt = seq // chunk
    args = [x, g, cos, sin, w_in, w_out]
    in_specs = [
        pl.BlockSpec((chunk, d), lambda b, t: (b * nt + t, 0)),
        _resident((1, d)),
        pl.BlockSpec((chunk, dk // 2), lambda b, t: (t, 0)),
        pl.BlockSpec((chunk, dk // 2), lambda b, t: (t, 0)),
        _resident(w_in.shape),
        _resident(w_out.shape),
    ]
    out_specs = [
        pl.BlockSpec((chunk, d), lambda b, t: (b * nt + t, 0)),
        pl.BlockSpec((1, heads, dk, dv), lambda b, t: (b, 0, 0, 0)),
    ]
    out_shape = [jax.ShapeDtypeStruct((m, d), F32), jax.ShapeDtypeStruct((batch, heads, dk, dv), F32)]
    side_block = 0
    if side is not None:
        small, s0, layer = side
        side_block, side_specs, o_spec, o_shape = _side_readout_specs(small, s0, layer, batch * nt,
                                                                      lambda b, t: b * nt + t)
        args += list(small) + [s0]
        in_specs += side_specs
        out_specs.append(o_spec)
        out_shape.append(o_shape)
    outs = pl.pallas_call(
        functools.partial(_ret_prompt_kernel, heads=heads, dk=dk, dv=dv, chunk=chunk, side_block=side_block),
        name="ret_prompt",
        grid=(batch, nt),
        in_specs=in_specs,
        out_specs=out_specs,
        out_shape=out_shape,
        scratch_shapes=[pltpu.VMEM((heads, dk, dv), F32), pltpu.VMEM((chunk, heads * dv), BF16)],
        compiler_params=_params("arbitrary", "arbitrary"),
    )(*args)
    return outs if side is None else (outs[0], outs[1], outs[2].reshape(-1, outs[2].shape[-1]))


def _ret_sample_proj_kernel(x_ref, g_ref, cos_ref, sin_ref, win_ref, wsw_ref, q_ref, k_ref, kt_ref, v_ref, gate_ref,
                            *, heads, dk, dv):
    qk_w = 2 * heads * dk
    hn = _rms(x_ref[...], g_ref[...]).astype(BF16)
    qk = _dot(hn, win_ref[:, :qk_w]) * cos_ref[...] + _dot(hn, wsw_ref[...]) * sin_ref[...]
    q_ref[...] = qk[:, :heads * dk]
    k_ref[...] = qk[:, heads * dk:] * dk ** -0.5
    kt_ref[...] = k_ref[...].T
    v_ref[...] = _dot(hn, win_ref[:, qk_w:qk_w + heads * dv])
    gate_ref[...] = _dot(hn, win_ref[:, qk_w + heads * dv:])


def _ret_sample_state_kernel(q_ref, *refs, heads, dk, dv, block):
    n_layers = (len(refs) - 2) // 3
    ins, (s_ref, o_ref) = refs[:3 * n_layers], refs[3 * n_layers:]
    n = q_ref.shape[0]
    lane = lax.broadcasted_iota(jnp.int32, (dk, n), 1)
    hs = range(heads)
    gammas = [1.0 - 2.0 ** (-5.0 - h) for h in hs]
    for i in range(block):
        b = pl.program_id(0) * block + i
        for layer in range(n_layers):
            kt_ref, v_ref, s0_ref = ins[3 * layer:3 * layer + 3]
            kt_bs = [jnp.where(lane == b, kt_ref[h * dk:(h + 1) * dk, :], 0.0).astype(BF16) for h in hs]
            s_news = [gamma * s0_ref[0, i, h] + _dot(kt_b, v_ref[:, h * dv:(h + 1) * dv].astype(BF16))
                      for h, gamma, kt_b in zip(hs, gammas, kt_bs)]
            for h in hs:
                s_ref[layer, i, h] = s_news[h]
        q_row = q_ref[pl.ds(b, 1), :]
        q8s = [jnp.broadcast_to(q_row[:, h * dk:(h + 1) * dk], (SUBLANES, dk)).astype(BF16) for h in hs]
        o_ref[0, i:i + 1, :] = jnp.concatenate(
            [_dot(q8, s_new.astype(BF16))[0:1] for q8, s_new in zip(q8s, s_news)], axis=1)


def _mixer_out_kernel(x_ref, o_ref, gate_ref, nw_ref, wout_ref, y_ref, *, heads, dv):
    parts = []
    for h in range(heads):
        o = o_ref[:, h * dv:(h + 1) * dv]
        o = o * lax.rsqrt(jnp.mean(o * o, axis=-1, keepdims=True) + EPS) * nw_ref[...]
        parts.append((o * _silu(gate_ref[:, h * dv:(h + 1) * dv])).astype(BF16))
    y_ref[...] = x_ref[...] + _dot(jnp.concatenate(parts, axis=1), wout_ref[...])


def _mixer_out(x, o, gate, norm_w, w_out, heads, dv):
    return pl.pallas_call(
        functools.partial(_mixer_out_kernel, heads=heads, dv=dv),
        name="mixer_out",
        out_shape=jax.ShapeDtypeStruct(x.shape, F32),
        compiler_params=_params(),
    )(x, o, gate, norm_w, w_out)


def _ret_sample_proj(x, g, cos, sin, w_in, w_sw, heads, dk, dv):
    n = x.shape[0]
    return pl.pallas_call(
        functools.partial(_ret_sample_proj_kernel, heads=heads, dk=dk, dv=dv),
        name="ret_sample_proj",
        out_shape=[
            jax.ShapeDtypeStruct((n, heads * dk), F32),
            jax.ShapeDtypeStruct((n, heads * dk), F32),
            jax.ShapeDtypeStruct((heads * dk, n), F32),
            jax.ShapeDtypeStruct((n, heads * dv), F32),
            jax.ShapeDtypeStruct((n, heads * dv), F32),
        ],
        compiler_params=_params(),
    )(x, g, cos, sin, w_in, w_sw)


def _ret_sample_final(q, history, s0):
    n = q.shape[0]
    layers, _, heads, dk, dv = s0.shape
    assert len(history) == layers
    block = math.gcd(RET_STATE_BLOCK, n)
    args, in_specs = [q], [_resident(q.shape)]
    for idx, (kt_l, v_l) in enumerate(history):
        args += [kt_l, v_l, s0]
        in_specs += [_resident(kt_l.shape), _resident(v_l.shape),
                     pl.BlockSpec((1, block, heads, dk, dv), lambda i, idx=idx: (idx, i, 0, 0, 0))]
    s_new, o = pl.pallas_call(
        functools.partial(_ret_sample_state_kernel, heads=heads, dk=dk, dv=dv, block=block),
        name="ret_sample_state",
        grid=(n // block,),
        in_specs=in_specs,
        out_specs=[pl.BlockSpec((layers, block, heads, dk, dv), lambda i: (0, i, 0, 0, 0)),
                   pl.BlockSpec((1, block, heads * dv), lambda i: (i, 0, 0))],
        out_shape=[jax.ShapeDtypeStruct(s0.shape, F32), jax.ShapeDtypeStruct((n // block, block, heads * dv), F32)],
        compiler_params=_params("arbitrary"),
    )(*args)
    return s_new, o.reshape(n, heads * dv)


def _unit_lower_inverse_minus_identity(lms, ri, ci, chunk):
    def level_mask(lev):
        return ((ri >> lev) == (ci >> lev)) & ((ri >> (lev - 1)) != (ci >> (lev - 1)))

    mask = level_mask(1)
    ns = [jnp.where(mask, -lm, 0.0) for lm in lms]
    for lev in range(2, chunk.bit_length()):
        mask = level_mask(lev)
        cs = [jnp.where(mask, lm, 0.0) for lm in lms]
        nbs = [n.astype(BF16) for n in ns]
        gs = [c + cn for c, cn in zip(cs, _dots([c.astype(BF16) for c in cs], nbs))]
        ns = [n - g - ng for n, g, ng in zip(ns, gs, _dots(nbs, [g.astype(BF16) for g in gs]))]
    return ns


def _gdn_prompt_kernel(x_ref, g_ref, cbuf_ref, win_ref, cw_ref, alog_ref, dtb_ref, nw_ref, wout_ref, *rest,
                       heads, dk, dv, rows, chunk, side_block):
    if side_block:
        side_in, (y_ref, sout_ref, cout_ref, side_o_ref, s_scr, xh_scr, uh_scr, og_scr) = rest[:6], rest[6:]
        _gdn_readout(pl.program_id(0) * pl.num_programs(1) + pl.program_id(1), side_block, *side_in, side_o_ref,
                     heads=heads, dk=dk, dv=dv)
    else:
        y_ref, sout_ref, cout_ref, s_scr, xh_scr, uh_scr, og_scr = rest
    t = pl.program_id(1)
    qkv_w = heads * (2 * dk + dv)
    hd = heads * dv
    cw = [cw_ref[j:j + 1, :] for j in range(4)]

    @pl.when(t == 0)
    def _():
        s_scr[...] = jnp.zeros_like(s_scr)
        xh_scr[...] = jnp.zeros_like(xh_scr)
        uh_scr[...] = jnp.zeros_like(uh_scr)
        xh_scr[0:1, :] = cbuf_ref[0, 2:3, :]
        uh_scr[0:1, :] = cw[1] * cbuf_ref[0, 1:2, :] + cw[0] * cbuf_ref[0, 0:1, :]
        uh_scr[1:2, :] = cw[1] * cbuf_ref[0, 2:3, :] + cw[0] * cbuf_ref[0, 1:2, :]

    x = x_ref[...]
    hn = _rms(x, g_ref[...]).astype(BF16)
    qkv = _dot(hn, win_ref[:, :qkv_w])
    sub = lax.broadcasted_iota(jnp.int32, (SUBLANES, qkv_w), 0)

    def shifted(cur, by, halo_tile):
        rolled = pltpu.roll(cur, by, 0)
        first = jnp.where(sub < by, halo_tile, rolled[:SUBLANES])
        return jnp.concatenate([first, rolled[SUBLANES:]], axis=0)

    x_prev = shifted(qkv, 1, xh_scr[...])
    u = cw[1] * qkv + cw[0] * x_prev
    conv = cw[3] * qkv + cw[2] * x_prev + shifted(u, 2, uh_scr[...])
    xh_scr[0:1, :] = qkv[rows - 1:rows, :]
    uh_scr[0:2, :] = u[rows - 2:rows, :]

    @pl.when(t == pl.num_programs(1) - 1)
    def _():
        cout_ref[0] = qkv[rows - 3:rows, :]

    act = _silu(conv)
    z = _dot(hn, win_ref[:, qkv_w:qkv_w + hd])
    ba = _dot(hn, win_ref[:, qkv_w + hd:])
    a = -jnp.exp(alog_ref[...]) * _softplus(ba + dtb_ref[...])
    ri = lax.broadcasted_iota(jnp.int32, (chunk, chunk), 0)
    ci = lax.broadcasted_iota(jnp.int32, (chunk, chunk), 1)
    tril = (ri >= ci).astype(BF16)
    subs = range(rows // chunk)
    bcum = jnp.concatenate([_exact_dot_left(tril, a[c * chunk:(c + 1) * chunk]) for c in subs], axis=0)
    beta = _expand_heads(jax.nn.sigmoid(ba), 0, heads, dv)
    bcum = _expand_heads(bcum, heads, heads, dv)
    probs = [(c, h) for c in subs for h in range(heads)]

    def tile(arr, c, col, width):
        return arr[c * chunk:(c + 1) * chunk, col:col + width]

    qs = [tile(act, c, h * dk, dk) for c, h in probs]
    ks = [tile(act, c, heads * dk + h * dk, dk) for c, h in probs]
    vs = [tile(act, c, 2 * heads * dk + h * dv, dv) for c, h in probs]
    qs = [q * lax.rsqrt(jnp.sum(q * q, axis=-1, keepdims=True) + EPS) * dk ** -0.5 for q in qs]
    ks = [k * lax.rsqrt(jnp.sum(k * k, axis=-1, keepdims=True) + EPS) for k in ks]
    bts = [tile(beta, c, h * dv, dv) for c, h in probs]
    bs = [tile(bcum, c, h * dv, dv) for c, h in probs]
    es = [jnp.exp(jnp.minimum(b - b.T, 0.0)) for b in bs]
    kbs = [k.astype(BF16) for k in ks]
    qk_kks = _dots([jnp.concatenate([q.astype(BF16), kb], axis=0) for q, kb in zip(qs, kbs)], kbs, nt=True)
    ps = [jnp.where(ri >= ci, qk_kk[:chunk] * e, 0.0).astype(BF16) for qk_kk, e in zip(qk_kks, es)]
    lms = [jnp.where(ri > ci, qk_kk[chunk:] * e, 0.0) * bt for qk_kk, e, bt in zip(qk_kks, es, bts)]
    ns = [n.astype(BF16) for n in _unit_lower_inverse_minus_identity(lms, ri, ci, chunk)]
    kq_es = [jnp.concatenate([(k * jnp.exp(b)).astype(BF16), (q * jnp.exp(b)).astype(BF16)], axis=0)
             for k, q, b in zip(ks, qs, bs)]
    b_lasts = [b[chunk - 1:chunk, :] for b in bs]
    kdts = [(k * jnp.exp(b_last - b)).T.astype(BF16) for k, b, b_last in zip(ks, bs, b_lasts)]
    ss = [s_scr[h] for h in range(heads)]
    for c in subs:
        sel = slice(c * heads, (c + 1) * heads)
        kq_ss = _dots(kq_es[sel], [s.astype(BF16) for s in ss])
        rs = [bt * (v - kq_s[:chunk]) for bt, v, kq_s in zip(bts[sel], vs[sel], kq_ss)]
        wbs = [(r + nr).astype(BF16) for r, nr in zip(rs, _dots(ns[sel], [r.astype(BF16) for r in rs]))]
        os_ = [kq_s[chunk:] + pw for kq_s, pw in zip(kq_ss, _dots(ps[sel], wbs))]
        ss = [jnp.exp(b_last) * s + kw for b_last, s, kw in zip(b_lasts[sel], ss, _dots(kdts[sel], wbs))]
        for h in range(heads):
            o = os_[h]
            o = o * lax.rsqrt(jnp.mean(o * o, axis=-1, keepdims=True) + EPS) * nw_ref[...]
            og_scr[c * chunk:(c + 1) * chunk, h * dv:(h + 1) * dv] = (o * _silu(tile(z, c, h * dv, dv))).astype(BF16)
    for h in range(heads):
        s_scr[h] = ss[h]
    y_ref[...] = x + _dot(og_scr[...], wout_ref[...])

    @pl.when(t == pl.num_programs(1) - 1)
    def _():
        sout_ref[0] = s_scr[...]


def _gdn_prompt(x, g, cbuf, w_in, conv_w, a_log, dt_bias, norm_w, w_out, batch, heads, dk, dv, side=None):
    m, d = x.shape
    seq = m // batch
    chunk = GDN_CHUNK
    taps = conv_w.shape[0]
    qkv_w = heads * (2 * dk + dv)
    rows = math.gcd(GDN_ROWS, seq)
    assert rows % chunk == 0 and dk == chunk and dv == chunk and chunk == LANES and 2 * heads <= LANES and taps == 4
    nt = seq // rows
    args = [x, g, cbuf, w_in, conv_w, a_log, dt_bias, norm_w, w_out]
    in_specs = [
        pl.BlockSpec((rows, d), lambda b, t: (b * nt + t, 0)),
        _resident((1, d)),
        pl.BlockSpec((1, taps - 1, qkv_w), lambda b, t: (b, 0, 0)),
        _resident(w_in.shape),
        _resident(conv_w.shape),
        _resident(a_log.shape),
        _resident(dt_bias.shape),
        _resident(norm_w.shape),
        _resident(w_out.shape),
    ]
    out_specs = [
        pl.BlockSpec((rows, d), lambda b, t: (b * nt + t, 0)),
        pl.BlockSpec((1, heads, dk, dv), lambda b, t: (b, 0, 0, 0)),
        pl.BlockSpec((1, taps - 1, qkv_w), lambda b, t: (b, 0, 0)),
    ]
    out_shape = [
        jax.ShapeDtypeStruct((m, d), F32),
        jax.ShapeDtypeStruct((batch, heads, dk, dv), F32),
        jax.ShapeDtypeStruct((batch, taps - 1, qkv_w), F32),
    ]
    side_block = 0
    if side is not None:
        small, s0, layer = side
        side_block, side_specs, o_spec, o_shape = _side_readout_specs(small, s0, layer, batch * nt,
                                                                      lambda b, t: b * nt + t)
        args += list(small) + [s0]
        in_specs += side_specs
        out_specs.append(o_spec)
        out_shape.append(o_shape)
    outs = pl.pallas_call(
        functools.partial(_gdn_prompt_kernel, heads=heads, dk=dk, dv=dv, rows=rows, chunk=chunk,
                          side_block=side_block),
        name="gdn_prompt",
        grid=(batch, nt),
        in_specs=in_specs,
        out_specs=out_specs,
        out_shape=out_shape,
        scratch_shapes=[
            pltpu.VMEM((heads, dk, dv), F32),
            pltpu.VMEM((SUBLANES, qkv_w), F32),
            pltpu.VMEM((SUBLANES, qkv_w), F32),
            pltpu.VMEM((rows, heads * dv), BF16),
        ],
        compiler_params=_params("arbitrary", "arbitrary"),
    )(*args)
    return outs if side is None else (outs[0], outs[1], outs[2], outs[3].reshape(-1, outs[3].shape[-1]))


def _gdn_sample_proj_kernel(x_ref, g_ref, cbuf_ref, win_ref, cw_ref, alog_ref, dtb_ref,
                            q_ref, kt_ref, k_ref, v_ref, z_ref, beta_ref, ea_ref, cout_ref, *, heads, dk, dv, taps):
    qkv_w = heads * (2 * dk + dv)
    hd = heads * dv
    halo = taps - 1
    hn = _rms(x_ref[...], g_ref[...]).astype(BF16)
    qkv = _dot(hn, win_ref[:, :qkv_w])
    conv = qkv * cw_ref[halo:taps, :]
    for j in range(halo):
        conv = conv + cbuf_ref[j] * cw_ref[j:j + 1, :]
        if j > 0:
            cout_ref[j - 1] = cbuf_ref[j]
    cout_ref[halo - 1] = qkv
    act = _silu(conv)
    for h in range(heads):
        q = act[:, h * dk:(h + 1) * dk]
        k = act[:, heads * dk + h * dk:heads * dk + (h + 1) * dk]
        q_ref[:, h * dk:(h + 1) * dk] = q * lax.rsqrt(jnp.sum(q * q, axis=-1, keepdims=True) + EPS) * dk ** -0.5
        k_ref[:, h * dk:(h + 1) * dk] = k * lax.rsqrt(jnp.sum(k * k, axis=-1, keepdims=True) + EPS)
    kt_ref[...] = k_ref[...].T
    v_ref[...] = act[:, 2 * heads * dk:]
    z_ref[...] = _dot(hn, win_ref[:, qkv_w:qkv_w + hd])
    ba = _dot(hn, win_ref[:, qkv_w + hd:])
    beta_ref[...] = _expand_heads(jax.nn.sigmoid(ba), 0, heads, dv)
    a = -jnp.exp(alog_ref[...]) * _softplus(ba + dtb_ref[...])
    ea_ref[...] = jnp.exp(_expand_heads(a, heads, heads, dv))


def _gdn_sample_state_kernel(q_ref, *refs, heads, dk, dv, block):
    n_layers = (len(refs) - 2) // 6
    ins, (s_ref, o_ref) = refs[:6 * n_layers], refs[6 * n_layers:]
    n = q_ref.shape[0]
    lane = lax.broadcasted_iota(jnp.int32, (dk, n), 1)
    hs = range(heads)
    for i in range(block):
        b = pl.program_id(0) * block + i
        for layer in range(n_layers):
            kt_ref, k_ref, v_ref, beta_ref, ea_ref, s0_ref = ins[6 * layer:6 * layer + 6]
            k_row, v_row = k_ref[pl.ds(b, 1), :], v_ref[pl.ds(b, 1), :]
            beta_row, ea_row = beta_ref[pl.ds(b, 1), :], ea_ref[pl.ds(b, 1), :]
            s0s = [s0_ref[0, i, h] for h in hs]
            eas = [ea_row[:, h * dv:(h + 1) * dv] for h in hs]
            k8s = [jnp.broadcast_to(k_row[:, h * dk:(h + 1) * dk] * ea, (SUBLANES, dk)).astype(BF16)
                   for h, ea in zip(hs, eas)]
            kss = [_dot(k8, s0.astype(BF16))[0:1] for k8, s0 in zip(k8s, s0s)]
            ws = [beta_row[:, h * dv:(h + 1) * dv] * (v_row[:, h * dv:(h + 1) * dv] - ks) for h, ks in zip(hs, kss)]
            kt_bs = [jnp.where(lane == b, kt_ref[h * dk:(h + 1) * dk, :], 0.0).astype(BF16) for h in hs]
            s_news = [ea * s0 + _dot(kt_b, jnp.broadcast_to(w, (n, dv)).astype(BF16))
                      for ea, s0, kt_b, w in zip(eas, s0s, kt_bs, ws)]
            for h in hs:
                s_ref[layer, i, h] = s_news[h]
        q_row = q_ref[pl.ds(b, 1), :]
        q8s = [jnp.broadcast_to(q_row[:, h * dk:(h + 1) * dk], (SUBLANES, dk)).astype(BF16) for h in hs]
        o_ref[i:i + 1, :] = jnp.concatenate(
            [_dot(q8, s_new.astype(BF16))[0:1] for q8, s_new in zip(q8s, s_news)], axis=1)


def _gdn_sample_proj(x, g, cbuf, w_in, conv_w, a_log, dt_bias, heads, dk, dv):
    n = x.shape[0]
    taps = conv_w.shape[0]
    qkv_w = heads * (2 * dk + dv)
    hd = heads * dv
    cbuf_t = jnp.swapaxes(cbuf, 0, 1)
    q, kt, k, v, z, beta, ea, cout_t = pl.pallas_call(
        functools.partial(_gdn_sample_proj_kernel, heads=heads, dk=dk, dv=dv, taps=taps),
        name="gdn_sample_proj",
        out_shape=[
            jax.ShapeDtypeStruct((n, heads * dk), F32),
            jax.ShapeDtypeStruct((heads * dk, n), F32),
            jax.ShapeDtypeStruct((n, heads * dk), F32),
            jax.ShapeDtypeStruct((n, hd), F32),
            jax.ShapeDtypeStruct((n, hd), F32),
            jax.ShapeDtypeStruct((n, hd), F32),
            jax.ShapeDtypeStruct((n, hd), F32),
            jax.ShapeDtypeStruct((taps - 1, n, qkv_w), F32),
        ],
        compiler_params=_params(),
    )(x, g, cbuf_t, w_in, conv_w, a_log, dt_bias)
    return q, kt, k, v, z, beta, ea, jnp.swapaxes(cout_t, 0, 1)


def _gdn_sample_final(q, history, s0):
    n = q.shape[0]
    layers, _, heads, dk, dv = s0.shape
    assert len(history) == layers
    hd = heads * dv
    block = math.gcd(GDN_STATE_BLOCK, n)
    args, in_specs = [q], [_resident(q.shape)]
    for idx, small in enumerate(history):
        args += list(small) + [s0]
        in_specs += [_resident(t.shape) for t in small]
        in_specs.append(pl.BlockSpec((1, block, heads, dk, dv), lambda i, idx=idx: (idx, i, 0, 0, 0)))
    return pl.pallas_call(
        functools.partial(_gdn_sample_state_kernel, heads=heads, dk=dk, dv=dv, block=block),
        name="gdn_sample_state",
        grid=(n // block,),
        in_specs=in_specs,
        out_specs=[pl.BlockSpec((layers, block, heads, dk, dv), lambda i: (0, i, 0, 0, 0)),
                   pl.BlockSpec((block, hd), lambda i: (i, 0))],
        out_shape=[jax.ShapeDtypeStruct(s0.shape, F32), jax.ShapeDtypeStruct((n, hd), F32)],
        compiler_params=_params("arbitrary"),
    )(*args)


def _rope_tables(pos, dk):
    theta = 1.0 / (ROPE_BASE ** jnp.linspace(0.0, 1.0, dk // 2, dtype=F32))
    ang = pos[:, None] * theta[None, :]
    return jnp.cos(ang), jnp.sin(ang)


def _ret_weight_prep_kernel(w_ref, w_out_ref, deint_ref, swap_ref, *, heads, dk):
    qk_w = 2 * heads * dk
    half = dk // 2
    w = w_ref[...].astype(BF16)
    w_out_ref[...] = w
    deint_ref[:, qk_w:] = w[:, qk_w:]
    src = lax.broadcasted_iota(jnp.int32, (dk, dk), 0)
    dst = lax.broadcasted_iota(jnp.int32, (dk, dk), 1)
    to_halves = (src == jnp.where(dst < half, 2 * dst, 2 * (dst - half) + 1)).astype(BF16)
    swap_signed = jnp.where(src == (dst ^ 1), jnp.where((dst & 1) == 0, -1.0, 1.0), 0.0).astype(BF16)
    for h in range(2 * heads):
        block = w[:, h * dk:(h + 1) * dk]
        deint_ref[:, h * dk:(h + 1) * dk] = _dot(block, to_halves).astype(BF16)
        swap_ref[:, h * dk:(h + 1) * dk] = _dot(block, swap_signed).astype(BF16)


def _prep_ret_weights(w_in_all, layer, heads, dk):
    _, d, cols = w_in_all.shape
    qk_w = 2 * heads * dk
    rows = math.gcd(256, d)
    return pl.pallas_call(
        functools.partial(_ret_weight_prep_kernel, heads=heads, dk=dk),
        name="ret_weight_prep",
        grid=(d // rows,),
        in_specs=[pl.BlockSpec((None, rows, cols), lambda i: (layer, i, 0))],
        out_specs=[pl.BlockSpec((rows, cols), lambda i: (i, 0)), pl.BlockSpec((rows, cols), lambda i: (i, 0)),
                   pl.BlockSpec((rows, qk_w), lambda i: (i, 0))],
        out_shape=[jax.ShapeDtypeStruct((d, cols), BF16), jax.ShapeDtypeStruct((d, cols), BF16),
                   jax.ShapeDtypeStruct((d, qk_w), BF16)],
        compiler_params=_params("arbitrary"),
    )(w_in_all)


def _prep_gdn_weights(w_in, a_log, dt_bias, heads, dk, dv):
    pad = LANES - 2 * heads
    w = jnp.pad(w_in, ((0, 0), (0, pad))).astype(BF16)
    place = lambda p: jnp.pad(p, (heads, pad))[None, :]
    return w, place(a_log), place(dt_bias)


def _trunk(x_p, x_s, batch_p, s_ret, s_gdn, s_conv, norm_g, ffn_gu, ffn_down, ret_w, ret_w_out, gdn_w, gdn_conv_w,
           gdn_norm_w, gdn_w_out, final_g):
    depth = norm_g.shape[0]
    n_mixers = 2
    seq = x_p.shape[0] // batch_p
    n_ret, _, ret_heads, ret_dk, ret_dv = s_ret.shape
    n_gdn, _, gdn_heads, gdn_dk, gdn_dv = s_gdn.shape
    cos_p, sin_p = _rope_tables(jnp.arange(seq, dtype=F32), ret_dk)
    cos_s, sin_s = (jnp.tile(jnp.repeat(t, 2, axis=1), (1, 2 * ret_heads))
                    for t in _rope_tables(jnp.full((1,), SAMPLE_PAST_LEN, F32), ret_dk))
    ones_dv = jnp.ones((1, ret_dv), F32)
    fg = final_g[None, :]
    norm_g4 = norm_g[:, :, None, :]
    zero_conv = jnp.zeros((batch_p,) + s_conv.shape[2:], s_conv.dtype)
    ffn_w = (ffn_gu[0, 0].astype(BF16), ffn_down[0, 0].astype(BF16))
    ret_p, gdn_p, conv_p, conv_s = [], [], [], []
    ret_s = gdn_s = None
    ret_hist, gdn_hist = [], []
    for i in range(depth):
        j = i // n_mixers
        for which, which_norm in ((0, 0), (1, 2)):
            if which == 1:
                g = norm_g[i, 1][None, :]
                if i % n_mixers == 0:
                    w_orig, w_deint, w_swap = ret_w[j]
                    q, k, kt, v, gate = _ret_sample_proj(x_s, g, cos_s, sin_s, w_orig, w_swap, ret_heads, ret_dk,
                                                         ret_dv)
                    ret_hist.append((kt, v))
                    prompt_args = (x_p, g, cos_p, sin_p, w_deint, ret_w_out[j], batch_p, ret_heads, ret_dk, ret_dv)
                    if j == n_ret - 1:
                        x_p, s = _ret_prompt(*prompt_args)
                        ret_s, o = _ret_sample_final(q, ret_hist, s_ret)
                    else:
                        x_p, s, o = _ret_prompt(*prompt_args, side=((q, k, v), s_ret, j))
                    ret_p.append(s)
                    x_s = _mixer_out(x_s, o, gate, ones_dv, ret_w_out[j], ret_heads, ret_dv)
                else:
                    w, a_log, dt_bias = gdn_w[j]
                    nw = gdn_norm_w[j][None, :]
                    q, kt, k, v, z, beta, ea, buf = _gdn_sample_proj(x_s, g, s_conv[j], w, gdn_conv_w[j], a_log,
                                                                     dt_bias, gdn_heads, gdn_dk, gdn_dv)
                    conv_s.append(buf)
                    gdn_hist.append((kt, k, v, beta, ea))
                    prompt_args = (x_p, g, zero_conv, w, gdn_conv_w[j], a_log, dt_bias, nw, gdn_w_out[j], batch_p,
                                   gdn_heads, gdn_dk, gdn_dv)
                    if j == n_gdn - 1:
                        x_p, s, buf = _gdn_prompt(*prompt_args)
                        gdn_s, o = _gdn_sample_final(q, gdn_hist, s_gdn)
                    else:
                        x_p, s, buf, o = _gdn_prompt(*prompt_args, side=((q, k, v, beta, ea), s_gdn, j))
                    gdn_p.append(s)
                    conv_p.append(buf)
                    x_s = _mixer_out(x_s, o, z, nw, gdn_w_out[j], gdn_heads, gdn_dv)
            nxt = (i, 1) if which == 0 else (i + 1, 0)
            cast_next = (ffn_gu, ffn_down) + nxt if nxt[0] < depth else None
            x_p, x_s, made = _ffn(x_p, x_s, norm_g4, *ffn_w, fg, i, which_norm, which == 1 and i == depth - 1,
                                  cast_next)
            ffn_w = made
    return (x_p, x_s, jnp.stack(ret_p), jnp.stack(gdn_p), jnp.stack(conv_p), ret_s, gdn_s, jnp.stack(conv_s))


def kernel(x_prompt, x_sample, state_ret, state_gdn, state_gdn_conv, norm_g, ffn_gu, ffn_down, ret_w_in,
           ret_w_out, gdn_w_in, gdn_conv_w, gdn_a_log, gdn_dt_bias, gdn_norm_w, gdn_w_out, final_g):
    bp, seq, d = x_prompt.shape
    bs, dec_seq, _ = x_sample.shape
    assert dec_seq == 1
    n_ret, _, ret_heads, ret_dk, _ = state_ret.shape
    n_gdn, _, gdn_heads, gdn_dk, gdn_dv = state_gdn.shape
    ret_w = [_prep_ret_weights(ret_w_in, j, ret_heads, ret_dk) for j in range(n_ret)]
    gdn_w = [_prep_gdn_weights(gdn_w_in[j], gdn_a_log[j], gdn_dt_bias[j], gdn_heads, gdn_dk, gdn_dv)
             for j in range(n_gdn)]
    y_p, y_s, ret_p, gdn_p, conv_p, ret_s, gdn_s, conv_s = _trunk(
        x_prompt.reshape(bp * seq, d), x_sample.reshape(bs, d), bp, state_ret, state_gdn, state_gdn_conv, norm_g,
        ffn_gu, ffn_down, ret_w, ret_w_out.astype(BF16), gdn_w, gdn_conv_w, gdn_norm_w, gdn_w_out.astype(BF16),
        final_g)
    return (y_p.reshape(bp, seq, d), y_s.reshape(bs, dec_seq, d), ret_p, gdn_p, conv_p, ret_s, gdn_s, conv_s)
```

```python
import functools
import math

import jax
import jax.numpy as jnp
from jax import lax
from jax.experimental import pallas as pl
from jax.experimental.pallas import tpu as pltpu

F32 = jnp.float32
BF16 = jnp.bfloat16
EPS = 1e-6
ROPE_BASE = 10000.0
SAMPLE_PAST_LEN = 16384.0

LANES = 128
SUBLANES = 8
VMEM_LIMIT_BYTES = 56 * 1024 * 1024

FFN_ROWS = 1024
RET_CHUNK = 256
GDN_CHUNK = 128
GDN_ROWS = 256
RET_STATE_BLOCK = 2
GDN_STATE_BLOCK = 8


def _dot(a, b):
    return jnp.dot(a, b, preferred_element_type=F32)


def _dot_nt(a, b):
    return lax.dot_general(a, b, (((1,), (1,)), ((), ())), preferred_element_type=F32)


def _rms(x, g):
    return x * lax.rsqrt(jnp.mean(x * x, axis=-1, keepdims=True) + EPS) * g


def _silu(x):
    h = 0.5 * x
    return h * jnp.tanh(h) + h


def _softplus(x):
    return jnp.maximum(x, 0.0) + jnp.log1p(jnp.exp(-jnp.abs(x)))


def _bf16_terms(x):
    t0 = x.astype(BF16)
    r = x - t0.astype(F32)
    t1 = r.astype(BF16)
    t2 = (r - t1.astype(F32)).astype(BF16)
    return t0, t1, t2


def _exact_dot_right(x, sel):
    return _dot(jnp.concatenate(_bf16_terms(x), axis=1), jnp.concatenate([sel] * 3, axis=0))


def _exact_dot_left(sel, x):
    return _dot(jnp.concatenate([sel] * 3, axis=1), jnp.concatenate(_bf16_terms(x), axis=0))


def _expand_heads(x, first, heads, width):
    return jnp.concatenate([jnp.broadcast_to(x[:, first + h:first + h + 1], (x.shape[0], width))
                            for h in range(heads)], axis=1)


def _dots(lhs, rhs, nt=False):
    dot = _dot_nt if nt else _dot
    return [dot(a, b) for a, b in zip(lhs, rhs)]


def _resident(shape):
    nd = len(shape)
    return pl.BlockSpec(shape, lambda *_: (0,) * nd, pipeline_mode=pl.Buffered(1))


def _params(*sem):
    return pltpu.CompilerParams(dimension_semantics=sem, vmem_limit_bytes=VMEM_LIMIT_BYTES)


def _ffn_kernel(x_ref, small_ref, g_ref, wg_ref, wu_ref, wd_ref, fg_ref, *rest, final, cast_next):
    def ffn(x):
        h = _rms(x, g_ref[...]).astype(BF16)
        a = _dot(h, wg_ref[...])
        b = _dot(h, wu_ref[...])
        act = (_silu(a) * b).astype(BF16)
        y = x + 0.5 * _dot(act, wd_ref[...])
        return _rms(y, fg_ref[...]) if final else y

    o_ref, small_out_ref = rest[-4:-2] if cast_next else rest
    o_ref[...] = ffn(x_ref[...])

    @pl.when(pl.program_id(0) == pl.num_programs(0) - 1)
    def _():
        small_out_ref[...] = ffn(small_ref[...])

    if cast_next:
        next_gu_ref, next_down_ref = rest[:2]
        gu_out_ref, down_out_ref = rest[-2:]
        gu_out_ref[...] = next_gu_ref[...].astype(BF16)
        down_out_ref[...] = next_down_ref[...].astype(BF16)


def _slab_count(rows, steps):
    return max(n for n in range(1, steps + 1) if steps % n == 0 and rows % (16 * n) == 0)


def _ffn(x, small, norm_g, w_gu, w_down, final_g, layer, which_norm, final, cast_next=None):
    m, d = x.shape
    f = w_down.shape[0]
    tm = min(FFN_ROWS, m)
    steps = m // tm
    assert m % tm == 0 and f % LANES == 0
    in_specs = [
        pl.BlockSpec((tm, d), lambda i: (i, 0)),
        _resident(small.shape),
        pl.BlockSpec((None, None, 1, d), lambda i: (layer, which_norm, 0, 0), pipeline_mode=pl.Buffered(1)),
        pl.BlockSpec((d, f), lambda i: (0, 0), pipeline_mode=pl.Buffered(1)),
        pl.BlockSpec((d, f), lambda i: (0, 1), pipeline_mode=pl.Buffered(1)),
        _resident((f, d)),
        _resident((1, d)),
    ]
    args = [x, small, norm_g, w_gu, w_gu, w_down, final_g]
    out_specs = [pl.BlockSpec((tm, d), lambda i: (i, 0)), pl.BlockSpec(small.shape, lambda i: (0, 0))]
    out_shape = [jax.ShapeDtypeStruct((m, d), F32), jax.ShapeDtypeStruct(small.shape, F32)]
    if cast_next is not None:
        next_gu, next_down, nl, nw = cast_next
        for w, rows, cols in ((next_gu, d, 2 * f), (next_down, f, d)):
            n = _slab_count(rows, steps)
            in_specs.append(pl.BlockSpec((None, None, rows // n, cols),
                                         lambda i, n=n: (nl, nw, i * n // steps, 0)))
            out_specs.append(pl.BlockSpec((rows // n, cols), lambda i, n=n: (i * n // steps, 0)))
            out_shape.append(jax.ShapeDtypeStruct((rows, cols), BF16))
            args.append(w)
    outs = pl.pallas_call(
        functools.partial(_ffn_kernel, final=final, cast_next=cast_next is not None),
        name="ffn",
        grid=(steps,),
        in_specs=in_specs,
        out_specs=out_specs,
        out_shape=out_shape,
        compiler_params=_params("arbitrary"),
    )(*args)
    return outs[0], outs[1], tuple(outs[2:])


def _ret_readout(step, block, q_ref, k_ref, v_ref, s0_ref, o_ref, *, heads, dk, dv):
    hs = range(heads)
    for i in range(block):
        b = step * block + i
        q_row, k_row, v_row = q_ref[pl.ds(b, 1), :], k_ref[pl.ds(b, 1), :], v_ref[pl.ds(b, 1), :]
        q_hs = [q_row[:, h * dk:(h + 1) * dk] for h in hs]
        q_s = [_dot(jnp.broadcast_to(q_h, (SUBLANES, dk)).astype(BF16), s0_ref[0, i, h].astype(BF16))[0:1]
               for h, q_h in zip(hs, q_hs)]
        q_k = [jnp.sum(q_h * k_row[:, h * dk:(h + 1) * dk], axis=-1, keepdims=True) for h, q_h in zip(hs, q_hs)]
        o_ref[0, i:i + 1, :] = jnp.concatenate(
            [(1.0 - 2.0 ** (-5.0 - h)) * qs + qk * v_row[:, h * dv:(h + 1) * dv] for h, qs, qk in zip(hs, q_s, q_k)],
            axis=1)


def _gdn_readout(step, block, q_ref, k_ref, v_ref, beta_ref, ea_ref, s0_ref, o_ref, *, heads, dk, dv):
    hs = range(heads)
    row = lax.broadcasted_iota(jnp.int32, (SUBLANES, dk), 0)
    for i in range(block):
        b = step * block + i
        q_row, k_row, v_row = q_ref[pl.ds(b, 1), :], k_ref[pl.ds(b, 1), :], v_ref[pl.ds(b, 1), :]
        beta_row, ea_row = beta_ref[pl.ds(b, 1), :], ea_ref[pl.ds(b, 1), :]
        q_hs = [q_row[:, h * dk:(h + 1) * dk] for h in hs]
        k_hs = [k_row[:, h * dk:(h + 1) * dk] for h in hs]
        eas = [ea_row[:, h * dv:(h + 1) * dv] for h in hs]
        lhs = [jnp.where(row == 0, q_h, jnp.where(row == 1, k_h * ea, 0.0)).astype(BF16)
               for q_h, k_h, ea in zip(q_hs, k_hs, eas)]
        prods = [_dot(l, s0_ref[0, i, h].astype(BF16)) for h, l in zip(hs, lhs)]
        ws = [beta_row[:, h * dv:(h + 1) * dv] * (v_row[:, h * dv:(h + 1) * dv] - p[1:2]) for h, p in zip(hs, prods)]
        q_k = [jnp.sum(q_h * k_h, axis=-1, keepdims=True) for q_h, k_h in zip(q_hs, k_hs)]
        o_ref[0, i:i + 1, :] = jnp.concatenate(
            [ea * p[0:1] + qk * w for ea, p, qk, w in zip(eas, prods, q_k, ws)], axis=1)


def _ret_prompt_kernel(x_ref, g_ref, cos_ref, sin_ref, win_ref, wout_ref, *rest, heads, dk, dv, chunk, side_block):
    if side_block:
        side_in, (y_ref, sout_ref, side_o_ref, s_scr, og_scr) = rest[:4], rest[4:]
        _ret_readout(pl.program_id(0) * pl.num_programs(1) + pl.program_id(1), side_block, *side_in, side_o_ref,
                     heads=heads, dk=dk, dv=dv)
    else:
        y_ref, sout_ref, s_scr, og_scr = rest
    t = pl.program_id(1)
    half = dk // 2

    @pl.when(t == 0)
    def _():
        s_scr[...] = jnp.zeros_like(s_scr)

    x = x_ref[...]
    hn = _rms(x, g_ref[...]).astype(BF16)
    cos = cos_ref[...]
    sin = sin_ref[...]
    ri = lax.broadcasted_iota(jnp.int32, (chunk, chunk), 0)
    ci = lax.broadcasted_iota(jnp.int32, (chunk, chunk), 1)
    lag = (ri - ci).astype(F32)
    row = lax.broadcasted_iota(jnp.int32, (chunk, half), 0).astype(F32)
    k_off, v_off, g_off = heads * dk, 2 * heads * dk, 2 * heads * dk + heads * dv
    for h in range(heads):
        log_gamma = math.log(1.0 - 2.0 ** (-5.0 - h))
        q = _dot(hn, win_ref[:, h * dk:(h + 1) * dk])
        k = _dot(hn, win_ref[:, k_off + h * dk:k_off + (h + 1) * dk])
        v = _dot(hn, win_ref[:, v_off + h * dv:v_off + (h + 1) * dv]).astype(BF16)
        gate = _dot(hn, win_ref[:, g_off + h * dv:g_off + (h + 1) * dv])
        q1, q2 = q[:, :half], q[:, half:]
        k1, k2 = k[:, :half], k[:, half:]
        qr1, qr2 = q1 * cos - q2 * sin, q2 * cos + q1 * sin
        kr1, kr2 = (k1 * cos - k2 * sin) * dk ** -0.5, (k2 * cos + k1 * sin) * dk ** -0.5
        qb = jnp.concatenate([qr1, qr2], axis=1).astype(BF16)
        kb = jnp.concatenate([kr1, kr2], axis=1).astype(BF16)
        decay = jnp.where(ri >= ci, jnp.exp(lag * log_gamma), 0.0)
        p = (_dot_nt(qb, kb) * decay).astype(BF16)
        q_scale = jnp.exp((row + 1.0) * log_gamma)
        k_scale = jnp.exp((chunk - 1.0 - row) * log_gamma)
        qd = jnp.concatenate([qr1 * q_scale, qr2 * q_scale], axis=1).astype(BF16)
        kd = jnp.concatenate([kr1 * k_scale, kr2 * k_scale], axis=1)
        s = s_scr[h]
        o = _dot(p, v) + _dot(qd, s.astype(BF16))
        s_scr[h] = math.exp(chunk * log_gamma) * s + _dot(kd.T.astype(BF16), v)
        o = o * lax.rsqrt(jnp.mean(o * o, axis=-1, keepdims=True) + EPS) * _silu(gate)
        og_scr[:, h * dv:(h + 1) * dv] = o.astype(BF16)
    y_ref[...] = x + _dot(og_scr[...], wout_ref[...])

    @pl.when(t == pl.num_programs(1) - 1)
    def _():
        ro = lax.broadcasted_iota(jnp.int32, (dk, dk), 0)
        rin = lax.broadcasted_iota(jnp.int32, (dk, dk), 1)
        perm = (rin == (ro >> 1) + (ro & 1) * half).astype(BF16)
        for h in range(heads):
            sout_ref[0, h] = _exact_dot_left(perm, s_scr[h])


def _side_readout_specs(small, s0, layer, steps, step_of):
    _, n, heads, dk, dv = s0.shape
    assert n % steps == 0
    block = n // steps
    in_specs = [_resident(t.shape) for t in small]
    in_specs.append(pl.BlockSpec((1, block, heads, dk, dv), lambda *ids: (layer, step_of(*ids), 0, 0, 0)))
    o_spec = pl.BlockSpec((1, block, heads * dv), lambda *ids: (step_of(*ids), 0, 0))
    return block, in_specs, o_spec, jax.ShapeDtypeStruct((steps, block, heads * dv), F32)


def _ret_prompt(x, g, cos, sin, w_in, w_out, batch, heads, dk, dv, side=None):
    m, d = x.shape
    seq = m // batch
    chunk = min(RET_CHUNK, seq)
    assert seq % chunk == 0
    nt = seq // chunk
    args = [x, g, cos, sin, w_in, w_out]
    in_specs = [
        pl.BlockSpec((chunk, d), lambda b, t: (b * nt + t, 0)),
        _resident((1, d)),
        pl.BlockSpec((chunk, dk // 2), lambda b, t: (t, 0)),
        pl.BlockSpec((chunk, dk // 2), lambda b, t: (t, 0)),
        _resident(w_in.shape),
        _resident(w_out.shape),
    ]
    out_specs = [
        pl.BlockSpec((chunk, d), lambda b, t: (b * nt + t, 0)),
        pl.BlockSpec((1, heads, dk, dv), lambda b, t: (b, 0, 0, 0)),
    ]
    out_shape = [jax.ShapeDtypeStruct((m, d), F32), jax.ShapeDtypeStruct((batch, heads, dk, dv), F32)]
    side_block = 0
    if side is not None:
        small, s0, layer = side
        side_block, side_specs, o_spec, o_shape = _side_readout_specs(small, s0, layer, batch * nt,
                                                                      lambda b, t: b * nt + t)
        args += list(small) + [s0]
        in_specs += side_specs
        out_specs.append(o_spec)
        out_shape.append(o_shape)
    outs = pl.pallas_call(
        functools.partial(_ret_prompt_kernel, heads=heads, dk=dk, dv=dv, chunk=chunk, side_block=side_block),
        name="ret_prompt",
        grid=(batch, nt),
        in_specs=in_specs,
        out_specs=out_specs,
        out_shape=out_shape,
        scratch_shapes=[pltpu.VMEM((heads, dk, dv), F32), pltpu.VMEM((chunk, heads * dv), BF16)],
        compiler_params=_params("arbitrary", "arbitrary"),
    )(*args)
    return outs if side is None else (outs[0], outs[1], outs[2].reshape(-1, outs[2].shape[-1]))


def _ret_sample_proj_kernel(x_ref, g_ref, cos_ref, sin_ref, win_ref, wsw_ref, q_ref, k_ref, kt_ref, v_ref, gate_ref,
                            *, heads, dk, dv):
    qk_w = 2 * heads * dk
    hn = _rms(x_ref[...], g_ref[...]).astype(BF16)
    qk = _dot(hn, win_ref[:, :qk_w]) * cos_ref[...] + _dot(hn, wsw_ref[...]) * sin_ref[...]
    q_ref[...] = qk[:, :heads * dk]
    k_ref[...] = qk[:, heads * dk:] * dk ** -0.5
    kt_ref[...] = k_ref[...].T
    v_ref[...] = _dot(hn, win_ref[:, qk_w:qk_w + heads * dv])
    gate_ref[...] = _dot(hn, win_ref[:, qk_w + heads * dv:])


def _ret_sample_state_kernel(q_ref, *refs, heads, dk, dv, block):
    n_layers = (len(refs) - 2) // 3
    ins, (s_ref, o_ref) = refs[:3 * n_layers], refs[3 * n_layers:]
    n = q_ref.shape[0]
    lane = lax.broadcasted_iota(jnp.int32, (dk, n), 1)
    hs = range(heads)
    gammas = [1.0 - 2.0 ** (-5.0 - h) for h in hs]
    for i in range(block):
        b = pl.program_id(0) * block + i
        for layer in range(n_layers):
            kt_ref, v_ref, s0_ref = ins[3 * layer:3 * layer + 3]
            kt_bs = [jnp.where(lane == b, kt_ref[h * dk:(h + 1) * dk, :], 0.0).astype(BF16) for h in hs]
            s_news = [gamma * s0_ref[0, i, h] + _dot(kt_b, v_ref[:, h * dv:(h + 1) * dv].astype(BF16))
                      for h, gamma, kt_b in zip(hs, gammas, kt_bs)]
            for h in hs:
                s_ref[layer, i, h] = s_news[h]
        q_row = q_ref[pl.ds(b, 1), :]
        q8s = [jnp.broadcast_to(q_row[:, h * dk:(h + 1) * dk], (SUBLANES, dk)).astype(BF16) for h in hs]
        o_ref[0, i:i + 1, :] = jnp.concatenate(
            [_dot(q8, s_new.astype(BF16))[0:1] for q8, s_new in zip(q8s, s_news)], axis=1)


def _mixer_out_kernel(x_ref, o_ref, gate_ref, nw_ref, wout_ref, y_ref, *, heads, dv):
    parts = []
    for h in range(heads):
        o = o_ref[:, h * dv:(h + 1) * dv]
        o = o * lax.rsqrt(jnp.mean(o * o, axis=-1, keepdims=True) + EPS) * nw_ref[...]
        parts.append((o * _silu(gate_ref[:, h * dv:(h + 1) * dv])).astype(BF16))
    y_ref[...] = x_ref[...] + _dot(jnp.concatenate(parts, axis=1), wout_ref[...])


def _mixer_out(x, o, gate, norm_w, w_out, heads, dv):
    return pl.pallas_call(
        functools.partial(_mixer_out_kernel, heads=heads, dv=dv),
        name="mixer_out",
        out_shape=jax.ShapeDtypeStruct(x.shape, F32),
        compiler_params=_params(),
    )(x, o, gate, norm_w, w_out)


def _ret_sample_proj(x, g, cos, sin, w_in, w_sw, heads, dk, dv):
    n = x.shape[0]
    return pl.pallas_call(
        functools.partial(_ret_sample_proj_kernel, heads=heads, dk=dk, dv=dv),
        name="ret_sample_proj",
        out_shape=[
            jax.ShapeDtypeStruct((n, heads * dk), F32),
            jax.ShapeDtypeStruct((n, heads * dk), F32),
            jax.ShapeDtypeStruct((heads * dk, n), F32),
            jax.ShapeDtypeStruct((n, heads * dv), F32),
            jax.ShapeDtypeStruct((n, heads * dv), F32),
        ],
        compiler_params=_params(),
    )(x, g, cos, sin, w_in, w_sw)


def _ret_sample_final(q, history, s0):
    n = q.shape[0]
    layers, _, heads, dk, dv = s0.shape
    assert len(history) == layers
    block = math.gcd(RET_STATE_BLOCK, n)
    args, in_specs = [q], [_resident(q.shape)]
    for idx, (kt_l, v_l) in enumerate(history):
        args += [kt_l, v_l, s0]
        in_specs += [_resident(kt_l.shape), _resident(v_l.shape),
                     pl.BlockSpec((1, block, heads, dk, dv), lambda i, idx=idx: (idx, i, 0, 0, 0))]
    s_new, o = pl.pallas_call(
        functools.partial(_ret_sample_state_kernel, heads=heads, dk=dk, dv=dv, block=block),
        name="ret_sample_state",
        grid=(n // block,),
        in_specs=in_specs,
        out_specs=[pl.BlockSpec((layers, block, heads, dk, dv), lambda i: (0, i, 0, 0, 0)),
                   pl.BlockSpec((1, block, heads * dv), lambda i: (i, 0, 0))],
        out_shape=[jax.ShapeDtypeStruct(s0.shape, F32), jax.ShapeDtypeStruct((n // block, block, heads * dv), F32)],
        compiler_params=_params("arbitrary"),
    )(*args)
    return s_new, o.reshape(n, heads * dv)


def _unit_lower_inverse_minus_identity(lms, ri, ci, chunk):
    def level_mask(lev):
        return ((ri >> lev) == (ci >> lev)) & ((ri >> (lev - 1)) != (ci >> (lev - 1)))

    mask = level_mask(1)
    ns = [jnp.where(mask, -lm, 0.0) for lm in lms]
    for lev in range(2, chunk.bit_length()):
        mask = level_mask(lev)
        cs = [jnp.where(mask, lm, 0.0) for lm in lms]
        nbs = [n.astype(BF16) for n in ns]
        gs = [c + cn for c, cn in zip(cs, _dots([c.astype(BF16) for c in cs], nbs))]
        ns = [n - g - ng for n, g, ng in zip(ns, gs, _dots(nbs, [g.astype(BF16) for g in gs]))]
    return ns


def _gdn_prompt_kernel(x_ref, g_ref, cbuf_ref, win_ref, cw_ref, alog_ref, dtb_ref, nw_ref, wout_ref, *rest,
                       heads, dk, dv, rows, chunk, side_block):
    if side_block:
        side_in, (y_ref, sout_ref, cout_ref, side_o_ref, s_scr, xh_scr, uh_scr, og_scr) = rest[:6], rest[6:]
        _gdn_readout(pl.program_id(0) * pl.num_programs(1) + pl.program_id(1), side_block, *side_in, side_o_ref,
                     heads=heads, dk=dk, dv=dv)
    else:
        y_ref, sout_ref, cout_ref, s_scr, xh_scr, uh_scr, og_scr = rest
    t = pl.program_id(1)
    qkv_w = heads * (2 * dk + dv)
    hd = heads * dv
    cw = [cw_ref[j:j + 1, :] for j in range(4)]

    @pl.when(t == 0)
    def _():
        s_scr[...] = jnp.zeros_like(s_scr)
        xh_scr[...] = jnp.zeros_like(xh_scr)
        uh_scr[...] = jnp.zeros_like(uh_scr)
        xh_scr[0:1, :] = cbuf_ref[0, 2:3, :]
        uh_scr[0:1, :] = cw[1] * cbuf_ref[0, 1:2, :] + cw[0] * cbuf_ref[0, 0:1, :]
        uh_scr[1:2, :] = cw[1] * cbuf_ref[0, 2:3, :] + cw[0] * cbuf_ref[0, 1:2, :]

    x = x_ref[...]
    hn = _rms(x, g_ref[...]).astype(BF16)
    qkv = _dot(hn, win_ref[:, :qkv_w])
    sub = lax.broadcasted_iota(jnp.int32, (SUBLANES, qkv_w), 0)

    def shifted(cur, by, halo_tile):
        rolled = pltpu.roll(cur, by, 0)
        first = jnp.where(sub < by, halo_tile, rolled[:SUBLANES])
        return jnp.concatenate([first, rolled[SUBLANES:]], axis=0)

    x_prev = shifted(qkv, 1, xh_scr[...])
    u = cw[1] * qkv + cw[0] * x_prev
    conv = cw[3] * qkv + cw[2] * x_prev + shifted(u, 2, uh_scr[...])
    xh_scr[0:1, :] = qkv[rows - 1:rows, :]
    uh_scr[0:2, :] = u[rows - 2:rows, :]

    @pl.when(t == pl.num_programs(1) - 1)
    def _():
        cout_ref[0] = qkv[rows - 3:rows, :]

    act = _silu(conv)
    z = _dot(hn, win_ref[:, qkv_w:qkv_w + hd])
    ba = _dot(hn, win_ref[:, qkv_w + hd:])
    a = -jnp.exp(alog_ref[...]) * _softplus(ba + dtb_ref[...])
    ri = lax.broadcasted_iota(jnp.int32, (chunk, chunk), 0)
    ci = lax.broadcasted_iota(jnp.int32, (chunk, chunk), 1)
    tril = (ri >= ci).astype(BF16)
    subs = range(rows // chunk)
    bcum = jnp.concatenate([_exact_dot_left(tril, a[c * chunk:(c + 1) * chunk]) for c in subs], axis=0)
    beta = _expand_heads(jax.nn.sigmoid(ba), 0, heads, dv)
    bcum = _expand_heads(bcum, heads, heads, dv)
    probs = [(c, h) for c in subs for h in range(heads)]

    def tile(arr, c, col, width):
        return arr[c * chunk:(c + 1) * chunk, col:col + width]

    qs = [tile(act, c, h * dk, dk) for c, h in probs]
    ks = [tile(act, c, heads * dk + h * dk, dk) for c, h in probs]
    vs = [tile(act, c, 2 * heads * dk + h * dv, dv) for c, h in probs]
    qs = [q * lax.rsqrt(jnp.sum(q * q, axis=-1, keepdims=True) + EPS) * dk ** -0.5 for q in qs]
    ks = [k * lax.rsqrt(jnp.sum(k * k, axis=-1, keepdims=True) + EPS) for k in ks]
    bts = [tile(beta, c, h * dv, dv) for c, h in probs]
    bs = [tile(bcum, c, h * dv, dv) for c, h in probs]
    es = [jnp.exp(jnp.minimum(b - b.T, 0.0)) for b in bs]
    kbs = [k.astype(BF16) for k in ks]
    qk_kks = _dots([jnp.concatenate([q.astype(BF16), kb], axis=0) for q, kb in zip(qs, kbs)], kbs, nt=True)
    ps = [jnp.where(ri >= ci, qk_kk[:chunk] * e, 0.0).astype(BF16) for qk_kk, e in zip(qk_kks, es)]
    lms = [jnp.where(ri > ci, qk_kk[chunk:] * e, 0.0) * bt for qk_kk, e, bt in zip(qk_kks, es, bts)]
    ns = [n.astype(BF16) for n in _unit_lower_inverse_minus_identity(lms, ri, ci, chunk)]
    kq_es = [jnp.concatenate([(k * jnp.exp(b)).astype(BF16), (q * jnp.exp(b)).astype(BF16)], axis=0)
             for k, q, b in zip(ks, qs, bs)]
    b_lasts = [b[chunk - 1:chunk, :] for b in bs]
    kdts = [(k * jnp.exp(b_last - b)).T.astype(BF16) for k, b, b_last in zip(ks, bs, b_lasts)]
    ss = [s_scr[h] for h in range(heads)]
    for c in subs:
        sel = slice(c * heads, (c + 1) * heads)
        kq_ss = _dots(kq_es[sel], [s.astype(BF16) for s in ss])
        rs = [bt * (v - kq_s[:chunk]) for bt, v, kq_s in zip(bts[sel], vs[sel], kq_ss)]
        wbs = [(r + nr).astype(BF16) for r, nr in zip(rs, _dots(ns[sel], [r.astype(BF16) for r in rs]))]
        os_ = [kq_s[chunk:] + pw for kq_s, pw in zip(kq_ss, _dots(ps[sel], wbs))]
        ss = [jnp.exp(b_last) * s + kw for b_last, s, kw in zip(b_lasts[sel], ss, _dots(kdts[sel], wbs))]
        for h in range(heads):
            o = os_[h]
            o = o * lax.rsqrt(jnp.mean(o * o, axis=-1, keepdims=True) + EPS) * nw_ref[...]
            og_scr[c * chunk:(c + 1) * chunk, h * dv:(h + 1) * dv] = (o * _silu(tile(z, c, h * dv, dv))).astype(BF16)
    for h in range(heads):
        s_scr[h] = ss[h]
    y_ref[...] = x + _dot(og_scr[...], wout_ref[...])

    @pl.when(t == pl.num_programs(1) - 1)
    def _():
        sout_ref[0] = s_scr[...]


def _gdn_prompt(x, g, cbuf, w_in, conv_w, a_log, dt_bias, norm_w, w_out, batch, heads, dk, dv, side=None):
    m, d = x.shape
    seq = m // batch
    chunk = GDN_CHUNK
    taps = conv_w.shape[0]
    qkv_w = heads * (2 * dk + dv)
    rows = math.gcd(GDN_ROWS, seq)
    assert rows % chunk == 0 and dk == chunk and dv == chunk and chunk == LANES and 2 * heads <= LANES and taps == 4
    nt = seq // rows
    args = [x, g, cbuf, w_in, conv_w, a_log, dt_bias, norm_w, w_out]
    in_specs = [
        pl.BlockSpec((rows, d), lambda b, t: (b * nt + t, 0)),
        _resident((1, d)),
        pl.BlockSpec((1, taps - 1, qkv_w), lambda b, t: (b, 0, 0)),
        _resident(w_in.shape),
        _resident(conv_w.shape),
        _resident(a_log.shape),
        _resident(dt_bias.shape),
        _resident(norm_w.shape),
        _resident(w_out.shape),
    ]
    out_specs = [
        pl.BlockSpec((rows, d), lambda b, t: (b * nt + t, 0)),
        pl.BlockSpec((1, heads, dk, dv), lambda b, t: (b, 0, 0, 0)),
        pl.BlockSpec((1, taps - 1, qkv_w), lambda b, t: (b, 0, 0)),
    ]
    out_shape = [
        jax.ShapeDtypeStruct((m, d), F32),
        jax.ShapeDtypeStruct((batch, heads, dk, dv), F32),
        jax.ShapeDtypeStruct((batch, taps - 1, qkv_w), F32),
    ]
    side_block = 0
    if side is not None:
        small, s0, layer = side
        side_block, side_specs, o_spec, o_shape = _side_readout_specs(small, s0, layer, batch * nt,
                                                                      lambda b, t: b * nt + t)
        args += list(small) + [s0]
        in_specs += side_specs
        out_specs.append(o_spec)
        out_shape.append(o_shape)
    outs = pl.pallas_call(
        functools.partial(_gdn_prompt_kernel, heads=heads, dk=dk, dv=dv, rows=rows, chunk=chunk,
                          side_block=side_block),
        name="gdn_prompt",
        grid=(batch, nt),
        in_specs=in_specs,
        out_specs=out_specs,
        out_shape=out_shape,
        scratch_shapes=[
            pltpu.VMEM((heads, dk, dv), F32),
            pltpu.VMEM((SUBLANES, qkv_w), F32),
            pltpu.VMEM((SUBLANES, qkv_w), F32),
            pltpu.VMEM((rows, heads * dv), BF16),
        ],
        compiler_params=_params("arbitrary", "arbitrary"),
    )(*args)
    return outs if side is None else (outs[0], outs[1], outs[2], outs[3].reshape(-1, outs[3].shape[-1]))


def _gdn_sample_proj_kernel(x_ref, g_ref, cbuf_ref, win_ref, cw_ref, alog_ref, dtb_ref,
                            q_ref, kt_ref, k_ref, v_ref, z_ref, beta_ref, ea_ref, cout_ref, *, heads, dk, dv, taps):
    qkv_w = heads * (2 * dk + dv)
    hd = heads * dv
    halo = taps - 1
    hn = _rms(x_ref[...], g_ref[...]).astype(BF16)
    qkv = _dot(hn, win_ref[:, :qkv_w])
    conv = qkv * cw_ref[halo:taps, :]
    for j in range(halo):
        conv = conv + cbuf_ref[j] * cw_ref[j:j + 1, :]
        if j > 0:
            cout_ref[j - 1] = cbuf_ref[j]
    cout_ref[halo - 1] = qkv
    act = _silu(conv)
    for h in range(heads):
        q = act[:, h * dk:(h + 1) * dk]
        k = act[:, heads * dk + h * dk:heads * dk + (h + 1) * dk]
        q_ref[:, h * dk:(h + 1) * dk] = q * lax.rsqrt(jnp.sum(q * q, axis=-1, keepdims=True) + EPS) * dk ** -0.5
        k_ref[:, h * dk:(h + 1) * dk] = k * lax.rsqrt(jnp.sum(k * k, axis=-1, keepdims=True) + EPS)
    kt_ref[...] = k_ref[...].T
    v_ref[...] = act[:, 2 * heads * dk:]
    z_ref[...] = _dot(hn, win_ref[:, qkv_w:qkv_w + hd])
    ba = _dot(hn, win_ref[:, qkv_w + hd:])
    beta_ref[...] = _expand_heads(jax.nn.sigmoid(ba), 0, heads, dv)
    a = -jnp.exp(alog_ref[...]) * _softplus(ba + dtb_ref[...])
    ea_ref[...] = jnp.exp(_expand_heads(a, heads, heads, dv))


def _gdn_sample_state_kernel(q_ref, *refs, heads, dk, dv, block):
    n_layers = (len(refs) - 2) // 6
    ins, (s_ref, o_ref) = refs[:6 * n_layers], refs[6 * n_layers:]
    n = q_ref.shape[0]
    lane = lax.broadcasted_iota(jnp.int32, (dk, n), 1)
    hs = range(heads)
    for i in range(block):
        b = pl.program_id(0) * block + i
        for layer in range(n_layers):
            kt_ref, k_ref, v_ref, beta_ref, ea_ref, s0_ref = ins[6 * layer:6 * layer + 6]
            k_row, v_row = k_ref[pl.ds(b, 1), :], v_ref[pl.ds(b, 1), :]
            beta_row, ea_row = beta_ref[pl.ds(b, 1), :], ea_ref[pl.ds(b, 1), :]
            s0s = [s0_ref[0, i, h] for h in hs]
            eas = [ea_row[:, h * dv:(h + 1) * dv] for h in hs]
            k8s = [jnp.broadcast_to(k_row[:, h * dk:(h + 1) * dk] * ea, (SUBLANES, dk)).astype(BF16)
                   for h, ea in zip(hs, eas)]
            kss = [_dot(k8, s0.astype(BF16))[0:1] for k8, s0 in zip(k8s, s0s)]
            ws = [beta_row[:, h * dv:(h + 1) * dv] * (v_row[:, h * dv:(h + 1) * dv] - ks) for h, ks in zip(hs, kss)]
            kt_bs = [jnp.where(lane == b, kt_ref[h * dk:(h + 1) * dk, :], 0.0).astype(BF16) for h in hs]
            s_news = [ea * s0 + _dot(kt_b, jnp.broadcast_to(w, (n, dv)).astype(BF16))
                      for ea, s0, kt_b, w in zip(eas, s0s, kt_bs, ws)]
            for h in hs:
                s_ref[layer, i, h] = s_news[h]
        q_row = q_ref[pl.ds(b, 1), :]
        q8s = [jnp.broadcast_to(q_row[:, h * dk:(h + 1) * dk], (SUBLANES, dk)).astype(BF16) for h in hs]
        o_ref[i:i + 1, :] = jnp.concatenate(
            [_dot(q8, s_new.astype(BF16))[0:1] for q8, s_new in zip(q8s, s_news)], axis=1)


def _gdn_sample_proj(x, g, cbuf, w_in, conv_w, a_log, dt_bias, heads, dk, dv):
    n = x.shape[0]
    taps = conv_w.shape[0]
    qkv_w = heads * (2 * dk + dv)
    hd = heads * dv
    cbuf_t = jnp.swapaxes(cbuf, 0, 1)
    q, kt, k, v, z, beta, ea, cout_t = pl.pallas_call(
        functools.partial(_gdn_sample_proj_kernel, heads=heads, dk=dk, dv=dv, taps=taps),
        name="gdn_sample_proj",
        out_shape=[
            jax.ShapeDtypeStruct((n, heads * dk), F32),
            jax.ShapeDtypeStruct((heads * dk, n), F32),
            jax.ShapeDtypeStruct((n, heads * dk), F32),
            jax.ShapeDtypeStruct((n, hd), F32),
            jax.ShapeDtypeStruct((n, hd), F32),
            jax.ShapeDtypeStruct((n, hd), F32),
            jax.ShapeDtypeStruct((n, hd), F32),
            jax.ShapeDtypeStruct((taps - 1, n, qkv_w), F32),
        ],
        compiler_params=_params(),
    )(x, g, cbuf_t, w_in, conv_w, a_log, dt_bias)
    return q, kt, k, v, z, beta, ea, jnp.swapaxes(cout_t, 0, 1)


def _gdn_sample_final(q, history, s0):
    n = q.shape[0]
    layers, _, heads, dk, dv = s0.shape
    assert len(history) == layers
    hd = heads * dv
    block = math.gcd(GDN_STATE_BLOCK, n)
    args, in_specs = [q], [_resident(q.shape)]
    for idx, small in enumerate(history):
        args += list(small) + [s0]
        in_specs += [_resident(t.shape) for t in small]
        in_specs.append(pl.BlockSpec((1, block, heads, dk, dv), lambda i, idx=idx: (idx, i, 0, 0, 0)))
    return pl.pallas_call(
        functools.partial(_gdn_sample_state_kernel, heads=heads, dk=dk, dv=dv, block=block),
        name="gdn_sample_state",
        grid=(n // block,),
        in_specs=in_specs,
        out_specs=[pl.BlockSpec((layers, block, heads, dk, dv), lambda i: (0, i, 0, 0, 0)),
                   pl.BlockSpec((block, hd), lambda i: (i, 0))],
        out_shape=[jax.ShapeDtypeStruct(s0.shape, F32), jax.ShapeDtypeStruct((n, hd), F32)],
        compiler_params=_params("arbitrary"),
    )(*args)


def _rope_tables(pos, dk):
    theta = 1.0 / (ROPE_BASE ** jnp.linspace(0.0, 1.0, dk // 2, dtype=F32))
    ang = pos[:, None] * theta[None, :]
    return jnp.cos(ang), jnp.sin(ang)


def _ret_weight_prep_kernel(w_ref, w_out_ref, deint_ref, swap_ref, *, heads, dk):
    qk_w = 2 * heads * dk
    half = dk // 2
    w = w_ref[...].astype(BF16)
    w_out_ref[...] = w
    deint_ref[:, qk_w:] = w[:, qk_w:]
    src = lax.broadcasted_iota(jnp.int32, (dk, dk), 0)
    dst = lax.broadcasted_iota(jnp.int32, (dk, dk), 1)
    to_halves = (src == jnp.where(dst < half, 2 * dst, 2 * (dst - half) + 1)).astype(BF16)
    swap_signed = jnp.where(src == (dst ^ 1), jnp.where((dst & 1) == 0, -1.0, 1.0), 0.0).astype(BF16)
    for h in range(2 * heads):
        block = w[:, h * dk:(h + 1) * dk]
        deint_ref[:, h * dk:(h + 1) * dk] = _dot(block, to_halves).astype(BF16)
        swap_ref[:, h * dk:(h + 1) * dk] = _dot(block, swap_signed).astype(BF16)


def _prep_ret_weights(w_in_all, layer, heads, dk):
    _, d, cols = w_in_all.shape
    qk_w = 2 * heads * dk
    rows = math.gcd(256, d)
    return pl.pallas_call(
        functools.partial(_ret_weight_prep_kernel, heads=heads, dk=dk),
        name="ret_weight_prep",
        grid=(d // rows,),
        in_specs=[pl.BlockSpec((None, rows, cols), lambda i: (layer, i, 0))],
        out_specs=[pl.BlockSpec((rows, cols), lambda i: (i, 0)), pl.BlockSpec((rows, cols), lambda i: (i, 0)),
                   pl.BlockSpec((rows, qk_w), lambda i: (i, 0))],
        out_shape=[jax.ShapeDtypeStruct((d, cols), BF16), jax.ShapeDtypeStruct((d, cols), BF16),
                   jax.ShapeDtypeStruct((d, qk_w), BF16)],
        compiler_params=_params("arbitrary"),
    )(w_in_all)


def _prep_gdn_weights(w_in, a_log, dt_bias, heads, dk, dv):
    pad = LANES - 2 * heads
    w = jnp.pad(w_in, ((0, 0), (0, pad))).astype(BF16)
    place = lambda p: jnp.pad(p, (heads, pad))[None, :]
    return w, place(a_log), place(dt_bias)


def _trunk(x_p, x_s, batch_p, s_ret, s_gdn, s_conv, norm_g, ffn_gu, ffn_down, ret_w, ret_w_out, gdn_w, gdn_conv_w,
           gdn_norm_w, gdn_w_out, final_g):
    depth = norm_g.shape[0]
    n_mixers = 2
    seq = x_p.shape[0] // batch_p
    n_ret, _, ret_heads, ret_dk, ret_dv = s_ret.shape
    n_gdn, _, gdn_heads, gdn_dk, gdn_dv = s_gdn.shape
    cos_p, sin_p = _rope_tables(jnp.arange(seq, dtype=F32), ret_dk)
    cos_s, sin_s = (jnp.tile(jnp.repeat(t, 2, axis=1), (1, 2 * ret_heads))
                    for t in _rope_tables(jnp.full((1,), SAMPLE_PAST_LEN, F32), ret_dk))
    ones_dv = jnp.ones((1, ret_dv), F32)
    fg = final_g[None, :]
    norm_g4 = norm_g[:, :, None, :]
    zero_conv = jnp.zeros((batch_p,) + s_conv.shape[2:], s_conv.dtype)
    ffn_w = (ffn_gu[0, 0].astype(BF16), ffn_down[0, 0].astype(BF16))
    ret_p, gdn_p, conv_p, conv_s = [], [], [], []
    ret_s = gdn_s = None
    ret_hist, gdn_hist = [], []
    for i in range(depth):
        j = i // n_mixers
        for which, which_norm in ((0, 0), (1, 2)):
            if which == 1:
                g = norm_g[i, 1][None, :]
                if i % n_mixers == 0:
                    w_orig, w_deint, w_swap = ret_w[j]
                    q, k, kt, v, gate = _ret_sample_proj(x_s, g, cos_s, sin_s, w_orig, w_swap, ret_heads, ret_dk,
                                                         ret_dv)
                    ret_hist.append((kt, v))
                    prompt_args = (x_p, g, cos_p, sin_p, w_deint, ret_w_out[j], batch_p, ret_heads, ret_dk, ret_dv)
                    if j == n_ret - 1:
                        x_p, s = _ret_prompt(*prompt_args)
                        ret_s, o = _ret_sample_final(q, ret_hist, s_ret)
                    else:
                        x_p, s, o = _ret_prompt(*prompt_args, side=((q, k, v), s_ret, j))
                    ret_p.append(s)
                    x_s = _mixer_out(x_s, o, gate, ones_dv, ret_w_out[j], ret_heads, ret_dv)
                else:
                    w, a_log, dt_bias = gdn_w[j]
                    nw = gdn_norm_w[j][None, :]
                    q, kt, k, v, z, beta, ea, buf = _gdn_sample_proj(x_s, g, s_conv[j], w, gdn_conv_w[j], a_log,
                                                                     dt_bias, gdn_heads, gdn_dk, gdn_dv)
                    conv_s.append(buf)
                    gdn_hist.append((kt, k, v, beta, ea))
                    prompt_args = (x_p, g, zero_conv, w, gdn_conv_w[j], a_log, dt_bias, nw, gdn_w_out[j], batch_p,
                                   gdn_heads, gdn_dk, gdn_dv)
                    if j == n_gdn - 1:
                        x_p, s, buf = _gdn_prompt(*prompt_args)
                        gdn_s, o = _gdn_sample_final(q, gdn_hist, s_gdn)
                    else:
                        x_p, s, buf, o = _gdn_prompt(*prompt_args, side=((q, k, v, beta, ea), s_gdn, j))
                    gdn_p.append(s)
                    conv_p.append(buf)
                    x_s = _mixer_out(x_s, o, z, nw, gdn_w_out[j], gdn_heads, gdn_dv)
            nxt = (i, 1) if which == 0 else (i + 1, 0)
            cast_next = (ffn_gu, ffn_down) + nxt if nxt[0] < depth else None
            x_p, x_s, made = _ffn(x_p, x_s, norm_g4, *ffn_w, fg, i, which_norm, which == 1 and i == depth - 1,
                                  cast_next)
            ffn_w = made
    return (x_p, x_s, jnp.stack(ret_p), jnp.stack(gdn_p), jnp.stack(conv_p), ret_s, gdn_s, jnp.stack(conv_s))


def kernel(x_prompt, x_sample, state_ret, state_gdn, state_gdn_conv, norm_g, ffn_gu, ffn_down, ret_w_in,
           ret_w_out, gdn_w_in, gdn_conv_w, gdn_a_log, gdn_dt_bias, gdn_norm_w, gdn_w_out, final_g):
    bp, seq, d = x_prompt.shape
    bs, dec_seq, _ = x_sample.shape
    assert dec_seq == 1
    n_ret, _, ret_heads, ret_dk, _ = state_ret.shape
    n_gdn, _, gdn_heads, gdn_dk, gdn_dv = state_gdn.shape
    ret_w = [_prep_ret_weights(ret_w_in, j, ret_heads, ret_dk) for j in range(n_ret)]
    gdn_w = [_prep_gdn_weights(gdn_w_in[j], gdn_a_log[j], gdn_dt_bias[j], gdn_heads, gdn_dk, gdn_dv)
             for j in range(n_gdn)]
    y_p, y_s, ret_p, gdn_p, conv_p, ret_s, gdn_s, conv_s = _trunk(
        x_prompt.reshape(bp * seq, d), x_sample.reshape(bs, d), bp, state_ret, state_gdn, state_gdn_conv, norm_g,
        ffn_gu, ffn_down, ret_w, ret_w_out.astype(BF16), gdn_w, gdn_conv_w, gdn_norm_w, gdn_w_out.astype(BF16),
        final_g)
    return (y_p.reshape(bp, seq, d), y_s.reshape(bs, dec_seq, d), ret_p, gdn_p, conv_p, ret_s, gdn_s, conv_s)
```
